```python
import math
import jax, jax.numpy as jnp
from jax import lax
import numpy as np

D_MODEL = 1024
BATCH = 4
SEQ = 8192
DEPTH = 2
DEC_BATCH = 8
DEC_SEQ = 32
PAST_LEN = 4096

CHUNK = 64
N_HEADS = 8
N_KV_HEADS = 2
HEAD_DIM = 64
Q_PER_KV = N_HEADS // N_KV_HEADS
WINDOW = 128
WINDOW_CHUNKS = WINDOW // CHUNK
ROPE_THETA = 10000.0
D_SSM = 512
SSM_GROUP_CH = 16
SSM_GROUPS = D_SSM // SSM_GROUP_CH
SSM_STATE = 64
N_EXPERTS = 64
N_EXPERT_GROUPS = 8
EXPERTS_PER_GROUP = N_EXPERTS // N_EXPERT_GROUPS
TOPK_GROUPS = 4
TOP_K = 8
D_EXPERT = 256
D_SHARED = 256
ROUTED_SCALE = 2.5
DISPATCH_BLOCK = 128
RMS_EPS = 1e-6
NEG_INF = -1e30
ATTN_W = N_HEADS * HEAD_DIM
KV_W = N_KV_HEADS * HEAD_DIM
IN_COLS = D_SSM + ATTN_W + 2 * KV_W + 2 * D_MODEL
IN_SPLITS = (D_SSM, D_SSM + ATTN_W, D_SSM + ATTN_W + KV_W, D_SSM + ATTN_W + 2 * KV_W,
             D_SSM + ATTN_W + 2 * KV_W + D_MODEL)

kernel_name = "hybrid_s5_swa_sink_moe_stream_step"


def rmsnorm(x, g):
    x32 = x.astype(jnp.float32)
    y = x32 * lax.rsqrt(jnp.mean(x32 * x32, axis=-1, keepdims=True) + RMS_EPS)
    return (y * g.astype(jnp.float32)).astype(x.dtype)


def rope(x, pos):
    half = HEAD_DIM // 2
    inv_freq = ROPE_THETA ** (-jnp.arange(half, dtype=jnp.float32) / half)
    ang = pos.astype(jnp.float32)[:, None] * inv_freq[None, :]
    cos = jnp.cos(ang)[:, None, :]
    sin = jnp.sin(ang)[:, None, :]
    x1, x2 = x[..., :half], x[..., half:]
    return jnp.concatenate([x1 * cos - x2 * sin, x2 * cos + x1 * sin], axis=-1).astype(x.dtype)


def sink_attention(q, k, v, valid, sink):
    *lead, tq, _, _ = q.shape
    qg = q.reshape(*lead, tq, N_KV_HEADS, Q_PER_KV, HEAD_DIM)
    s = jnp.einsum('...qkgd,...skd->...kgqs', qg, k).astype(jnp.float32) * (HEAD_DIM ** -0.5)
    if valid is not None:
        s = jnp.where(valid[..., None, None, None, :], s, NEG_INF)
    sk = sink.astype(jnp.float32).reshape(N_KV_HEADS, Q_PER_KV, 1, 1)
    m = jnp.maximum(jnp.max(s, axis=-1, keepdims=True), sk)
    p = jnp.exp(s - m)
    p = p / (jnp.sum(p, axis=-1, keepdims=True) + jnp.exp(sk - m))
    o = jnp.einsum('...kgqs,...skd->...qkgd', p, v.astype(jnp.float32))
    return o.reshape(*lead, tq, ATTN_W).astype(q.dtype)


def banded_attention(q, k, v, sink):
    bt, t, _, _ = q.shape
    n_chunks = t // CHUNK
    qb = q.reshape(bt, n_chunks, CHUNK, N_HEADS, HEAD_DIM)
    pad = ((0, 0), (WINDOW, 0), (0, 0), (0, 0))
    kp = jnp.pad(k, pad).reshape(bt, n_chunks + WINDOW_CHUNKS, CHUNK, N_KV_HEADS, HEAD_DIM)
    vp = jnp.pad(v, pad).reshape(bt, n_chunks + WINDOW_CHUNKS, CHUNK, N_KV_HEADS, HEAD_DIM)
    kb = jnp.concatenate([kp[:, j:j + n_chunks] for j in range(WINDOW_CHUNKS + 1)], axis=2)
    vb = jnp.concatenate([vp[:, j:j + n_chunks] for j in range(WINDOW_CHUNKS + 1)], axis=2)
    band = (WINDOW_CHUNKS + 1) * CHUNK
    key_pos = jnp.arange(n_chunks)[:, None] * CHUNK - WINDOW + jnp.arange(band)[None, :]
    valid = key_pos >= 0
    o = sink_attention(qb, kb, vb, valid, sink)
    return o.reshape(bt, t, ATTN_W)


def s5_layer(u, h0, a_re, a_im, log_dt, b_re, b_im, c_re, c_im, d):
    bt, t, _ = u.shape
    u32 = u.astype(jnp.float32)
    ug = u32.reshape(bt, t, SSM_GROUPS, SSM_GROUP_CH)
    a_re = a_re.astype(jnp.float32); a_im = a_im.astype(jnp.float32)
    dt = jnp.exp(log_dt.astype(jnp.float32))[:, None]
    mag = jnp.exp(a_re * dt)
    ab_re = mag * jnp.cos(a_im * dt)
    ab_im = mag * jnp.sin(a_im * dt)
    den = a_re * a_re + a_im * a_im
    n_re = ab_re - 1.0
    f_re = (n_re * a_re + ab_im * a_im) / den
    f_im = (ab_im * a_re - n_re * a_im) / den
    b_re = b_re.astype(jnp.float32); b_im = b_im.astype(jnp.float32)
    bb_re = f_re[..., None] * b_re - f_im[..., None] * b_im
    bb_im = f_re[..., None] * b_im + f_im[..., None] * b_re
    bu_re = jnp.einsum('btgc,gpc->btgp', ug, bb_re)
    bu_im = jnp.einsum('btgc,gpc->btgp', ug, bb_im)
    h0_re = h0[..., 0].astype(jnp.float32); h0_im = h0[..., 1].astype(jnp.float32)
    bu_re = bu_re.at[:, 0].add(ab_re * h0_re - ab_im * h0_im)
    bu_im = bu_im.at[:, 0].add(ab_re * h0_im + ab_im * h0_re)
    a_t_re = jnp.broadcast_to(ab_re, (1, t, SSM_GROUPS, SSM_STATE))
    a_t_im = jnp.broadcast_to(ab_im, (1, t, SSM_GROUPS, SSM_STATE))

    def combine(e1, e2):
        a1r, a1i, b1r, b1i = e1
        a2r, a2i, b2r, b2i = e2
        return (a2r * a1r - a2i * a1i, a2r * a1i + a2i * a1r,
                a2r * b1r - a2i * b1i + b2r, a2r * b1i + a2i * b1r + b2i)

    _, _, h_re, h_im = lax.associative_scan(combine, (a_t_re, a_t_im, bu_re, bu_im), axis=1)
    y = (jnp.einsum('btgp,gcp->btgc', h_re, c_re.astype(jnp.float32))
         - jnp.einsum('btgp,gcp->btgc', h_im, c_im.astype(jnp.float32)))
    y = y.reshape(bt, t, D_SSM) + d.astype(jnp.float32) * u32
    h_last = jnp.stack([h_re[:, -1], h_im[:, -1]], axis=-1)
    return y.astype(u.dtype), h_last


def routed_moe(hf, router_w, router_bias, w_gate, w_up, w_down):
    t = hf.shape[0]
    scores = jax.nn.sigmoid((hf @ router_w).astype(jnp.float32))
    biased = scores + router_bias.astype(jnp.float32)
    grp = biased.reshape(t, N_EXPERT_GROUPS, EXPERTS_PER_GROUP)
    grp_score = jnp.sum(lax.top_k(grp, 2)[0], axis=-1)
    _, gidx = lax.top_k(grp_score, TOPK_GROUPS)
    gmask = jnp.sum(jax.nn.one_hot(gidx, N_EXPERT_GROUPS, dtype=jnp.float32), axis=-2) > 0
    emask = jnp.repeat(gmask, EXPERTS_PER_GROUP, axis=-1)
    _, top_idx = lax.top_k(jnp.where(emask, biased, NEG_INF), TOP_K)
    w = jnp.take_along_axis(scores, top_idx, axis=-1)
    w = w / jnp.sum(w, axis=-1, keepdims=True) * ROUTED_SCALE

    n_assign = t * TOP_K
    n_blocks = -(-n_assign // DISPATCH_BLOCK) + N_EXPERTS
    cap = n_blocks * DISPATCH_BLOCK
    e_flat = top_idx.reshape(n_assign).astype(jnp.int32)
    tok = jnp.arange(n_assign, dtype=jnp.int32) // TOP_K
    order = jnp.argsort(e_flat, stable=True)
    e_s = e_flat[order]; tok_s = tok[order]; w_s = w.reshape(n_assign)[order]
    counts = jnp.bincount(e_flat, length=N_EXPERTS).astype(jnp.int32)
    padded = (counts + DISPATCH_BLOCK - 1) // DISPATCH_BLOCK * DISPATCH_BLOCK
    pad_end = jnp.cumsum(padded)
    pad_start = pad_end - padded
    start = jnp.cumsum(counts) - counts
    dest = pad_start[e_s] + jnp.arange(n_assign, dtype=jnp.int32) - start[e_s]
    slot_tok = jnp.zeros((cap,), jnp.int32).at[dest].set(tok_s)
    slot_w = jnp.zeros((cap,), jnp.float32).at[dest].set(w_s)
    block_exp = jnp.minimum(
        jnp.searchsorted(pad_end, jnp.arange(n_blocks, dtype=jnp.int32) * DISPATCH_BLOCK, side='right'),
        N_EXPERTS - 1)

    def body(acc, blk):
        idx, wts, e = blk
        xb = hf[idx]
        hb = jax.nn.silu(xb @ w_gate[e]) * (xb @ w_up[e])
        yb = (hb @ w_down[e]).astype(jnp.float32)
        return acc.at[idx].add(yb * wts[:, None]), None

    out, _ = lax.scan(body, jnp.zeros(hf.shape, jnp.float32),
                      (slot_tok.reshape(n_blocks, DISPATCH_BLOCK),
                       slot_w.reshape(n_blocks, DISPATCH_BLOCK), block_exp))
    return out.astype(hf.dtype)


def trunk(x, c, pos, past_k, past_v, ssm_h0, p):
    bt, t, _ = x.shape
    cond = jax.nn.silu(c.astype(jnp.float32))
    new_k, new_v, new_h = [], [], []
    for l in range(DEPTH):
        mod = cond @ p['ada_w'][l] + p['ada_b'][l]
        sh1, sc1, g1, sh2, sc2, g2 = [m[:, None, :] for m in jnp.split(mod, 6, axis=-1)]
        h = rmsnorm(x, p['norm_attn_g'][l]) * (1.0 + sc1) + sh1
        proj = h @ p['w_in'][l]
        u, q, k, v, gate_s, gate_a = jnp.split(proj, IN_SPLITS, axis=-1)
        q = rope(q.reshape(bt, t, N_HEADS, HEAD_DIM), pos)
        k = rope(k.reshape(bt, t, N_KV_HEADS, HEAD_DIM), pos)
        v = v.reshape(bt, t, N_KV_HEADS, HEAD_DIM)
        if past_k is None:
            attn = banded_attention(q, k, v, p['attn_sink'][l])
            new_k.append(k[:, -WINDOW:]); new_v.append(v[:, -WINDOW:])
            h0 = jnp.zeros((bt, SSM_GROUPS, SSM_STATE, 2), jnp.float32)
        else:
            kk = jnp.concatenate([past_k[l].astype(k.dtype), k], axis=1)
            vv = jnp.concatenate([past_v[l].astype(v.dtype), v], axis=1)
            attn = sink_attention(q, kk, vv, None, p['attn_sink'][l])
            new_k.append(k); new_v.append(v)
            h0 = ssm_h0[l]
        y_ssm, h_last = s5_layer(u, h0, p['ssm_a_re'][l], p['ssm_a_im'][l], p['ssm_log_dt'][l],
                                 p['ssm_b_re'][l], p['ssm_b_im'][l], p['ssm_c_re'][l],
                                 p['ssm_c_im'][l], p['ssm_d'][l])
        new_h.append(h_last)
        z = jax.nn.gelu(y_ssm)
        y_ssm = z * jax.nn.sigmoid(z @ p['ssm_w_glu'][l] + p['ssm_b_glu'][l])
        merged = (jax.nn.sigmoid(gate_s) * (y_ssm @ p['w_branch_ssm'][l])
                  + jax.nn.sigmoid(gate_a) * (attn @ p['w_branch_attn'][l]))
        x = x + g1 * (merged @ p['w_out'][l])
        h = rmsnorm(x, p['norm_ffn_g'][l]) * (1.0 + sc2) + sh2
        hf = h.reshape(bt * t, D_MODEL)
        routed = routed_moe(hf, p['router_w'][l], p['router_bias'][l], p['expert_w_gate'][l],
                            p['expert_w_up'][l], p['expert_w_down'][l])
        shared = (jax.nn.silu(hf @ p['shared_w_gate'][l]) * (hf @ p['shared_w_up'][l])) @ p['shared_w_down'][l]
        x = x + g2 * (routed + shared).reshape(bt, t, D_MODEL)
    y = rmsnorm(x, p['final_g'])
    return y, jnp.stack(new_k), jnp.stack(new_v), jnp.stack(new_h)


def setup_inputs(seed: int = 0) -> dict:
    key = jax.random.key(seed)
    ks = iter(jax.random.split(key, 48))

    def nrm(shape, scale):
        return jax.random.normal(next(ks), shape, jnp.float32) * scale

    L = DEPTH
    rows = min(WINDOW, PAST_LEN)
    n_idx = jnp.arange(SSM_STATE, dtype=jnp.float32)
    return {
        'x_prompt': nrm((BATCH, SEQ, D_MODEL), 1.0),
        'x_sample': nrm((DEC_BATCH, DEC_SEQ, D_MODEL), 1.0),
        'cache_k': nrm((DEPTH, DEC_BATCH, rows, N_KV_HEADS, HEAD_DIM), 1.0),
        'cache_v': nrm((DEPTH, DEC_BATCH, rows, N_KV_HEADS, HEAD_DIM), 1.0),
        'state_ssm': nrm((DEPTH, DEC_BATCH, SSM_GROUPS, SSM_STATE, 2), 0.5),
        'c_prompt': nrm((BATCH, D_MODEL), 1.0),
        'c_sample': nrm((DEC_BATCH, D_MODEL), 1.0),
        'ada_w': nrm((L, D_MODEL, 6 * D_MODEL), 0.5 * D_MODEL ** -0.5),
        'ada_b': nrm((L, 6 * D_MODEL), 0.02),
        'norm_attn_g': 1.0 + nrm((L, D_MODEL), 0.05),
        'norm_ffn_g': 1.0 + nrm((L, D_MODEL), 0.05),
        'w_in': nrm((L, D_MODEL, IN_COLS), D_MODEL ** -0.5),
        'ssm_a_re': -0.5 + nrm((L, SSM_GROUPS, SSM_STATE), 0.01),
        'ssm_a_im': math.pi * n_idx + nrm((L, SSM_GROUPS, SSM_STATE), 0.01),
        'ssm_log_dt': jax.random.uniform(next(ks), (L, SSM_GROUPS), jnp.float32,
                                         minval=math.log(1e-3), maxval=math.log(1e-1)),
        'ssm_b_re': nrm((L, SSM_GROUPS, SSM_STATE, SSM_GROUP_CH), (2 * SSM_GROUP_CH) ** -0.5),
        'ssm_b_im': nrm((L, SSM_GROUPS, SSM_STATE, SSM_GROUP_CH), (2 * SSM_GROUP_CH) ** -0.5),
        'ssm_c_re': nrm((L, SSM_GROUPS, SSM_GROUP_CH, SSM_STATE), (2 * SSM_STATE) ** -0.5),
        'ssm_c_im': nrm((L, SSM_GROUPS, SSM_GROUP_CH, SSM_STATE), (2 * SSM_STATE) ** -0.5),
        'ssm_d': nrm((L, D_SSM), 0.5),
        'ssm_w_glu': nrm((L, D_SSM, D_SSM), D_SSM ** -0.5),
        'ssm_b_glu': nrm((L, D_SSM), 0.02),
        'attn_sink': nrm((L, N_HEADS), 1.0),
        'w_branch_ssm': nrm((L, D_SSM, D_MODEL), D_SSM ** -0.5),
        'w_branch_attn': nrm((L, ATTN_W, D_MODEL), ATTN_W ** -0.5),
        'w_out': nrm((L, D_MODEL, D_MODEL), D_MODEL ** -0.5),
        'router_w': nrm((L, D_MODEL, N_EXPERTS), D_MODEL ** -0.5),
        'router_bias': nrm((L, N_EXPERTS), 0.01),
        'expert_w_gate': nrm((L, N_EXPERTS, D_MODEL, D_EXPERT), D_MODEL ** -0.5),
        'expert_w_up': nrm((L, N_EXPERTS, D_MODEL, D_EXPERT), D_MODEL ** -0.5),
        'expert_w_down': nrm((L, N_EXPERTS, D_EXPERT, D_MODEL), D_EXPERT ** -0.5),
        'shared_w_gate': nrm((L, D_MODEL, D_SHARED), D_MODEL ** -0.5),
        'shared_w_up': nrm((L, D_MODEL, D_SHARED), D_MODEL ** -0.5),
        'shared_w_down': nrm((L, D_SHARED, D_MODEL), D_SHARED ** -0.5),
        'final_g': 1.0 + nrm((D_MODEL,), 0.05),
    }


def reference(x_prompt, x_sample, cache_k, cache_v, state_ssm, c_prompt, c_sample,
              ada_w, ada_b, norm_attn_g, norm_ffn_g, w_in,
              ssm_a_re, ssm_a_im, ssm_log_dt, ssm_b_re, ssm_b_im, ssm_c_re, ssm_c_im,
              ssm_d, ssm_w_glu, ssm_b_glu, attn_sink, w_branch_ssm, w_branch_attn, w_out,
              router_w, router_bias, expert_w_gate, expert_w_up, expert_w_down,
              shared_w_gate, shared_w_up, shared_w_down, final_g):
    p = {
        'ada_w': ada_w, 'ada_b': ada_b, 'norm_attn_g': norm_attn_g, 'norm_ffn_g': norm_ffn_g,
        'w_in': w_in, 'ssm_a_re': ssm_a_re, 'ssm_a_im': ssm_a_im, 'ssm_log_dt': ssm_log_dt,
        'ssm_b_re': ssm_b_re, 'ssm_b_im': ssm_b_im, 'ssm_c_re': ssm_c_re, 'ssm_c_im': ssm_c_im,
        'ssm_d': ssm_d, 'ssm_w_glu': ssm_w_glu, 'ssm_b_glu': ssm_b_glu, 'attn_sink': attn_sink,
        'w_branch_ssm': w_branch_ssm, 'w_branch_attn': w_branch_attn, 'w_out': w_out,
        'router_w': router_w, 'router_bias': router_bias, 'expert_w_gate': expert_w_gate,
        'expert_w_up': expert_w_up, 'expert_w_down': expert_w_down,
        'shared_w_gate': shared_w_gate, 'shared_w_up': shared_w_up, 'shared_w_down': shared_w_down,
        'final_g': final_g,
    }
    pos_prompt = jnp.arange(x_prompt.shape[1], dtype=jnp.int32)
    pos_sample = PAST_LEN + jnp.arange(x_sample.shape[1], dtype=jnp.int32)
    y_prompt, new_k_prompt, new_v_prompt, new_ssm_prompt = trunk(
        x_prompt, c_prompt, pos_prompt, None, None, None, p)
    y_sample, new_k_sample, new_v_sample, new_ssm_sample = trunk(
        x_sample, c_sample, pos_sample, cache_k, cache_v, state_ssm, p)
    return (y_prompt, y_sample, new_k_prompt, new_v_prompt, new_ssm_prompt,
            new_k_sample, new_v_sample, new_ssm_sample)
```

```python
import functools
import math

import jax
import jax.numpy as jnp
from jax import lax
from jax.experimental import pallas as pl
from jax.experimental.pallas import tpu as pltpu

F32 = jnp.float32
BF16 = jnp.bfloat16

D_MODEL = 1024
DEPTH = 2
PAST_LEN = 4096
CHUNK = 64
N_HEADS = 8
N_KV_HEADS = 2
HEAD_DIM = 64
WINDOW = 128
ROPE_THETA = 10000.0
D_SSM = 512
SSM_GROUP_CH = 16
SSM_GROUPS = D_SSM // SSM_GROUP_CH
SSM_STATE = 64
N_EXPERTS = 64
N_EXPERT_GROUPS = 8
EXPERTS_PER_GROUP = N_EXPERTS // N_EXPERT_GROUPS
TOPK_GROUPS = 4
TOP_K = 8
D_EXPERT = 256
D_SHARED = 256
ROUTED_SCALE = 2.5
RMS_EPS = 1e-6
NEG_INF = -1e30
ATTN_W = N_HEADS * HEAD_DIM
KV_W = N_KV_HEADS * HEAD_DIM
IN_COLS = D_SSM + ATTN_W + 2 * KV_W + 2 * D_MODEL
IN_SPLITS = (0, D_SSM, D_SSM + ATTN_W, D_SSM + ATTN_W + KV_W, D_SSM + ATTN_W + 2 * KV_W,
             D_SSM + ATTN_W + 2 * KV_W + D_MODEL, IN_COLS)

LANES = 128
SSM_L = 16
SSM_PAIRS = SSM_GROUPS // 2
SSM_FLAT = 2 * SSM_L * SSM_GROUP_CH
SSM_POW_ROWS = 16
VMEM_LIMIT = 56 * 1024 * 1024


def _cparams(*sem):
    return pltpu.CompilerParams(dimension_semantics=sem, vmem_limit_bytes=VMEM_LIMIT)


def _dot(a, b):
    return jnp.dot(a, b, preferred_element_type=F32)


def _dot_nt(a, b):
    return lax.dot_general(a, b, (((1,), (1,)), ((), ())), preferred_element_type=F32)


def _split(x):
    hi = x.astype(BF16)
    lo = (x - hi.astype(F32)).astype(BF16)
    return hi, lo


def _rmsnorm(x, g):
    return x * lax.rsqrt(jnp.mean(x * x, axis=-1, keepdims=True) + RMS_EPS) * g


def _mod_kernel(c_ref, w_ref, b_ref, o_ref):
    cond = jax.nn.silu(c_ref[...])
    ch, cl = _split(cond)
    wh, wl = _split(w_ref[...])
    o_ref[...] = _dot(ch, wh) + (_dot(ch, wl) + _dot(cl, wh)) + b_ref[...]


def _mod_call(c_all, ada_w, ada_b):
    depth = ada_w.shape[0]
    rows = c_all.shape[0]
    nj = 6
    return pl.pallas_call(
        _mod_kernel,
        grid=(depth, nj),
        in_specs=[
            pl.BlockSpec((rows, D_MODEL), lambda l, j: (0, 0)),
            pl.BlockSpec((None, D_MODEL, D_MODEL), lambda l, j: (l, 0, j)),
            pl.BlockSpec((None, 1, D_MODEL), lambda l, j: (l, 0, j)),
        ],
        out_specs=pl.BlockSpec((None, rows, D_MODEL), lambda l, j: (l, 0, j)),
        out_shape=jax.ShapeDtypeStruct((depth, rows, nj * D_MODEL), F32),
        compiler_params=_cparams("parallel", "parallel"),
        name="mod",
    )(c_all, ada_w, ada_b.reshape(depth, 1, nj * D_MODEL))


def _rope2(t, cos, sin_signed, first_half):
    swapped = jnp.where(first_half, pltpu.roll(t, LANES - HEAD_DIM // 2, axis=1),
                        pltpu.roll(t, HEAD_DIM // 2, axis=1))
    return t * cos + swapped * sin_signed


def _inproj_kernel(x_ref, sh_ref, sc_ref, g_ref, w_ref, cos_ref, sin_ref,
                   u_ref, q_ref, k_ref, v_ref, gs_ref, ga_ref):
    h = _rmsnorm(x_ref[...], g_ref[...]) * (1.0 + sc_ref[...]) + sh_ref[...]
    hb = h.astype(BF16)

    def proj(i):
        return _dot(hb, w_ref[:, IN_SPLITS[i]:IN_SPLITS[i + 1]])

    u_ref[...] = proj(0)
    cos = cos_ref[...]
    sin = sin_ref[...]
    lane = lax.broadcasted_iota(jnp.int32, cos.shape, 1)
    first_half = (lane % HEAD_DIM) < (HEAD_DIM // 2)
    q = proj(1)
    for j in range(ATTN_W // LANES):
        sl = slice(j * LANES, (j + 1) * LANES)
        q_ref[:, sl] = (_rope2(q[:, sl], cos, sin, first_half) * (HEAD_DIM ** -0.5)).astype(BF16)
    k_ref[...] = _rope2(proj(2), cos, sin, first_half)
    v_ref[...] = proj(3)
    gs_ref[...] = proj(4)
    ga_ref[...] = proj(5)


def _inproj_call(x, sh1, sc1, norm_g, w_in_bf, cos_t, sin_t, tt):
    bx, tx, _ = x.shape
    nt = tx // tt
    per_tok = sh1.shape[1] != 1
    mod_spec = (pl.BlockSpec((None, tt, D_MODEL), lambda b, i: (b, i, 0)) if per_tok
                else pl.BlockSpec((None, 1, D_MODEL), lambda b, i: (b, 0, 0)))

    def tok_spec(w):
        return pl.BlockSpec((None, tt, w), lambda b, i: (b, i, 0))

    def tok_shape(w, dt):
        return jax.ShapeDtypeStruct((bx, tx, w), dt)

    return pl.pallas_call(
        _inproj_kernel,
        grid=(bx, nt),
        in_specs=[
            tok_spec(D_MODEL), mod_spec, mod_spec,
            pl.BlockSpec((1, D_MODEL), lambda b, i: (0, 0)),
            pl.BlockSpec((D_MODEL, IN_COLS), lambda b, i: (0, 0)),
            pl.BlockSpec((tt, LANES), lambda b, i: (i, 0)),
            pl.BlockSpec((tt, LANES), lambda b, i: (i, 0)),
        ],
        out_specs=[tok_spec(D_SSM), tok_spec(ATTN_W), tok_spec(KV_W), tok_spec(KV_W),
                   tok_spec(D_MODEL), tok_spec(D_MODEL)],
        out_shape=[tok_shape(D_SSM, F32), tok_shape(ATTN_W, BF16), tok_shape(KV_W, F32),
                   tok_shape(KV_W, F32), tok_shape(D_MODEL, F32), tok_shape(D_MODEL, F32)],
        compiler_params=_cparams("parallel", "parallel"),
        name="inproj",
    )(x, sh1, sc1, norm_g, w_in_bf, cos_t, sin_t)


def _ssm_kernel(u_ref, t_ref, bl_ref, cl_ref, ap_ref, h0_ref, y_ref, s_ref, *, nb, cb):
    nc = nb * cb
    uf = u_ref[...]
    y = _dot(uf, t_ref[...])
    v = _dot(uf, bl_ref[...])
    xre = v[:, :LANES]
    xim = v[:, LANES:]

    row = lax.broadcasted_iota(jnp.int32, (nc, LANES), 0)
    cidx = row & (cb - 1)
    bidx = row // cb
    h0 = h0_ref[...]
    h0re = jnp.zeros((nc, LANES), F32)
    h0im = jnp.zeros((nc, LANES), F32)
    for b in range(nb):
        h0re = jnp.where(bidx == b, h0[b:b + 1, :LANES], h0re)
        h0im = jnp.where(bidx == b, h0[b:b + 1, LANES:], h0im)
    first = cidx == 0
    are = ap_ref[0:1, :LANES]
    aim = ap_ref[0:1, LANES:]
    xre = xre + jnp.where(first, are * h0re - aim * h0im, 0.0)
    xim = xim + jnp.where(first, are * h0im + aim * h0re, 0.0)

    d = 1
    k = 0
    while d < cb:
        are = ap_ref[k:k + 1, :LANES]
        aim = ap_ref[k:k + 1, LANES:]
        keep = cidx >= d
        sre = jnp.where(keep, pltpu.roll(xre, d, axis=0), 0.0)
        sim = jnp.where(keep, pltpu.roll(xim, d, axis=0), 0.0)
        xre, xim = xre + (are * sre - aim * sim), xim + (are * sim + aim * sre)
        d *= 2
        k += 1

    for b in range(nb):
        r = b * cb + cb - 1
        s_ref[b:b + 1, :LANES] = xre[r:r + 1, :]
        s_ref[b:b + 1, LANES:] = xim[r:r + 1, :]

    pre = jnp.where(first, h0re, pltpu.roll(xre, 1, axis=0))
    pim = jnp.where(first, h0im, pltpu.roll(xim, 1, axis=0))
    y = y + _dot(pre.astype(BF16), cl_ref[:LANES, :]) + _dot(pim.astype(BF16), cl_ref[LANES:, :])
    y_ref[...] = y


def _ssm_call(ug, mats, h0p, nb, cb):
    tp, bl, cl, ap = mats
    nc = nb * cb

    def pair_spec(r, c):
        return pl.BlockSpec((None, r, c), lambda p: (p, 0, 0))

    return pl.pallas_call(
        functools.partial(_ssm_kernel, nb=nb, cb=cb),
        grid=(SSM_PAIRS,),
        in_specs=[pair_spec(nc, SSM_FLAT), pair_spec(SSM_FLAT, SSM_FLAT), pair_spec(SSM_FLAT, 2 * LANES),
                  pair_spec(2 * LANES, SSM_FLAT), pair_spec(SSM_POW_ROWS, 2 * LANES),
                  pair_spec(nb, 2 * LANES)],
        out_specs=[pair_spec(nc, SSM_FLAT), pair_spec(nb, 2 * LANES)],
        out_shape=[jax.ShapeDtypeStruct((SSM_PAIRS, nc, SSM_FLAT), F32),
                   jax.ShapeDtypeStruct((SSM_PAIRS, nb, 2 * LANES), F32)],
        compiler_params=_cparams("parallel"),
        name="ssm",
    )(ug, tp, bl, cl, ap, h0p)


def _pair_lanes(a):
    lead = a.shape[:-2]
    a = a.reshape(*lead, SSM_PAIRS, 2 * SSM_STATE)
    return jnp.moveaxis(a, -2, 0)


def _block_diag_pairs(m):
    g, r, c = m.shape
    m = m.reshape(SSM_PAIRS, 2, r, c)
    z = jnp.zeros((SSM_PAIRS, r, c), m.dtype)
    top = jnp.concatenate([m[:, 0], z], axis=2)
    bot = jnp.concatenate([z, m[:, 1]], axis=2)
    return jnp.concatenate([top, bot], axis=1)


def _ssm_mats(a_re, a_im, log_dt, b_re, b_im, c_re, c_im):
    hp = lax.Precision.HIGHEST
    dt = jnp.exp(log_dt)[:, None]
    lre = a_re * dt
    lim = a_im * dt
    mag = jnp.exp(lre)
    ab_re = mag * jnp.cos(lim)
    ab_im = mag * jnp.sin(lim)
    den = a_re * a_re + a_im * a_im
    n_re = ab_re - 1.0
    f_re = (n_re * a_re + ab_im * a_im) / den
    f_im = (ab_im * a_re - n_re * a_im) / den
    bb_re = f_re[..., None] * b_re - f_im[..., None] * b_im
    bb_im = f_re[..., None] * b_im + f_im[..., None] * b_re

    def lam_pow(tau):
        tau = tau[:, None, None]
        m = jnp.exp(tau * lre)
        return m * jnp.cos(tau * lim), m * jnp.sin(tau * lim)

    pw_re, pw_im = lam_pow(jnp.arange(SSM_L + 1, dtype=F32))
    cp_re = c_re[None] * pw_re[:, :, None, :] - c_im[None] * pw_im[:, :, None, :]
    cp_im = c_re[None] * pw_im[:, :, None, :] + c_im[None] * pw_re[:, :, None, :]
    kern = (jnp.einsum('tgcn,gnd->tgcd', cp_re, bb_re, precision=hp)
            - jnp.einsum('tgcn,gnd->tgcd', cp_im, bb_im, precision=hp))
    s_i = jnp.arange(SSM_L)[:, None]
    t_i = jnp.arange(SSM_L)[None, :]
    tg = kern[jnp.clip(t_i - s_i, 0, SSM_L)]
    tg = jnp.where((t_i >= s_i)[:, :, None, None, None], tg, 0.0)
    tg = tg.transpose(2, 0, 4, 1, 3).reshape(SSM_GROUPS, SSM_FLAT // 2, SSM_FLAT // 2)
    tp = _block_diag_pairs(tg)

    rev_re = pw_re[SSM_L - 1 - jnp.arange(SSM_L)]
    rev_im = pw_im[SSM_L - 1 - jnp.arange(SSM_L)]
    bbr = bb_re.transpose(0, 2, 1)[None]
    bbi = bb_im.transpose(0, 2, 1)[None]
    blr = rev_re[:, :, None, :] * bbr - rev_im[:, :, None, :] * bbi
    bli = rev_re[:, :, None, :] * bbi + rev_im[:, :, None, :] * bbr
    blr = _block_diag_pairs(blr.transpose(1, 0, 2, 3).reshape(SSM_GROUPS, SSM_FLAT // 2, SSM_STATE))
    bli = _block_diag_pairs(bli.transpose(1, 0, 2, 3).reshape(SSM_GROUPS, SSM_FLAT // 2, SSM_STATE))
    bl = jnp.concatenate([blr, bli], axis=2)

    clr = cp_re[1:].transpose(1, 3, 0, 2).reshape(SSM_GROUPS, SSM_STATE, SSM_FLAT // 2)
    cli = -cp_im[1:].transpose(1, 3, 0, 2).reshape(SSM_GROUPS, SSM_STATE, SSM_FLAT // 2)
    cl = jnp.concatenate([_block_diag_pairs(clr), _block_diag_pairs(cli)], axis=1)

    steps = float(SSM_L) * (2.0 ** jnp.arange(SSM_POW_ROWS, dtype=F32))
    ap_re, ap_im = lam_pow(steps)
    ap = jnp.concatenate([_pair_lanes(ap_re), _pair_lanes(ap_im)], axis=-1)
    return tp.astype(BF16), bl.astype(BF16), cl.astype(BF16), ap


def _ssm_branch(u, mats, h0):
    b, t, _ = u.shape
    cb = t // SSM_L
    ug = u.reshape(b, cb, SSM_L, SSM_PAIRS, 2, SSM_GROUP_CH).transpose(3, 0, 1, 4, 2, 5)
    ug = ug.reshape(SSM_PAIRS, b * cb, SSM_FLAT).astype(BF16)
    h0p = jnp.concatenate([_pair_lanes(h0[..., 0]), _pair_lanes(h0[..., 1])], axis=-1)
    y, s = _ssm_call(ug, mats, h0p, b, cb)
    y = y.reshape(SSM_PAIRS, b, cb, 2, SSM_L, SSM_GROUP_CH).transpose(1, 2, 4, 0, 3, 5)
    y = y.reshape(b, t, D_SSM)
    s = s.reshape(SSM_PAIRS, b, 2, 2, SSM_STATE).transpose(1, 0, 3, 4, 2)
    return y, s.reshape(b, SSM_GROUPS, SSM_STATE, 2)


def _attn_kernel(sink_ref, q_ref, ka_ref, kb_ref, va_ref, vb_ref, o_ref, *, banded):
    rows = q_ref.shape[0]
    nk = 2 * LANES
    if banded:
        kfull = jnp.concatenate([ka_ref[...], kb_ref[...]], axis=0)
        vfull = jnp.concatenate([va_ref[...], vb_ref[...]], axis=0)
    else:
        pad = jnp.zeros((nk - ka_ref.shape[0] - kb_ref.shape[0], KV_W), F32)
        kfull = jnp.concatenate([ka_ref[...], kb_ref[...], pad], axis=0)
        vfull = jnp.concatenate([va_ref[...], vb_ref[...], pad], axis=0)

    r_i = lax.broadcasted_iota(jnp.int32, (rows, nk), 0)
    c_i = lax.broadcasted_iota(jnp.int32, (rows, nk), 1)
    if banded:
        lo = (r_i // CHUNK) * CHUNK
        first_frame = jnp.where(pl.program_id(1) > 0, 0, WINDOW)
        valid = (c_i >= lo) & (c_i < lo + WINDOW + CHUNK) & (c_i >= first_frame)
    else:
        valid = c_i < (ka_ref.shape[0] + kb_ref.shape[0])

    lane = lax.broadcasted_iota(jnp.int32, (nk, KV_W), 1)
    low = lane < HEAD_DIM
    kroll = pltpu.roll(kfull, HEAD_DIM, axis=1)
    vroll = pltpu.roll(vfull, HEAD_DIM, axis=1)

    for g in range(N_KV_HEADS):
        k_lo, k_hi = (kfull, kroll) if g == 0 else (kroll, kfull)
        v_lo, v_hi = (vfull, vroll) if g == 0 else (vroll, vfull)
        kpad = (jnp.where(low, k_lo, 0.0).astype(BF16), jnp.where(low, 0.0, k_hi).astype(BF16))
        vpad = (jnp.where(low, v_lo, 0.0).astype(BF16), jnp.where(low, 0.0, v_hi).astype(BF16))
        for pp in range(2):
            slab = 2 * g + pp
            qp = q_ref[:, slab * LANES:(slab + 1) * LANES]
            acc = None
            for hh in range(2):
                sk = sink_ref[2 * slab + hh]
                s = jnp.where(valid, _dot_nt(qp, kpad[hh]), NEG_INF)
                m = jnp.maximum(jnp.max(s, axis=-1, keepdims=True), sk)
                p = jnp.exp(s - m)
                den = jnp.sum(p, axis=-1, keepdims=True) + jnp.exp(sk - m)
                o = _dot(p.astype(BF16), vpad[hh]) / den
                acc = o if acc is None else acc + o
            o_ref[:, slab * LANES:(slab + 1) * LANES] = acc.astype(BF16)


def _attn_call(sink, q, ka, kb, va, vb, banded):
    bx, tx, _ = q.shape
    if banded:
        rows = 2 * CHUNK
        nt = tx // rows
        grid = (bx, nt)
        q_spec = pl.BlockSpec((None, rows, ATTN_W), lambda b, i: (b, i, 0))
        prev = pl.BlockSpec((None, WINDOW, KV_W), lambda b, i: (b, jnp.maximum(i - 1, 0), 0))
        cur = pl.BlockSpec((None, rows, KV_W), lambda b, i: (b, i, 0))
    else:
        rows = tx
        grid = (bx, 1)
        q_spec = pl.BlockSpec((None, rows, ATTN_W), lambda b, i: (b, 0, 0))
        prev = pl.BlockSpec((None, ka.shape[1], KV_W), lambda b, i: (b, 0, 0))
        cur = pl.BlockSpec((None, rows, KV_W), lambda b, i: (b, 0, 0))
    return pl.pallas_call(
        functools.partial(_attn_kernel, banded=banded),
        grid=grid,
        in_specs=[pl.BlockSpec(memory_space=pltpu.SMEM), q_spec, prev, cur, prev, cur],
        out_specs=q_spec,
        out_shape=jax.ShapeDtypeStruct((bx, tx, ATTN_W), BF16),
        compiler_params=_cparams("parallel", "parallel"),
        name="attn",
    )(sink, q, ka, kb, va, vb)


def _route(logits_t, bias_col, gwt_ref):
    tt = logits_t.shape[1]
    scores = jax.nn.sigmoid(logits_t)
    biased = scores + bias_col
    sub = lax.broadcasted_iota(jnp.int32, (EXPERTS_PER_GROUP, tt), 0).astype(F32)
    ninf = float('-inf')
    blocks = [biased[EXPERTS_PER_GROUP * g:EXPERTS_PER_GROUP * (g + 1), :] for g in range(N_EXPERT_GROUPS)]

    gscore = jnp.zeros((N_EXPERT_GROUPS, tt), F32)
    for g in range(N_EXPERT_GROUPS):
        blk = blocks[g]
        m1 = jnp.max(blk, axis=0, keepdims=True)
        i1 = jnp.min(jnp.where(blk == m1, sub, float(EXPERTS_PER_GROUP)), axis=0, keepdims=True)
        m2 = jnp.max(jnp.where(sub == i1, ninf, blk), axis=0, keepdims=True)
        gscore = jnp.where(sub == float(g), jnp.broadcast_to(m1 + m2, gscore.shape), gscore)

    grank = jnp.zeros((N_EXPERT_GROUPS, tt), F32)
    for j in range(N_EXPERT_GROUPS):
        rj = jnp.broadcast_to(gscore[j:j + 1, :], gscore.shape)
        beats = (rj > gscore) | ((rj == gscore) & (sub > float(j)))
        grank = grank + jnp.where(beats, 1.0, 0.0)
    gsel = jnp.where(grank < float(TOPK_GROUPS), 1.0, 0.0)

    masked = []
    for g in range(N_EXPERT_GROUPS):
        on = jnp.broadcast_to(gsel[g:g + 1, :], blocks[g].shape) > 0.5
        masked.append(jnp.where(on, blocks[g], NEG_INF))

    ranks = [jnp.zeros((EXPERTS_PER_GROUP, tt), F32) for _ in range(N_EXPERT_GROUPS)]
    for jb in range(N_EXPERT_GROUPS):
        for jj in range(EXPERTS_PER_GROUP):
            rj = jnp.broadcast_to(masked[jb][jj:jj + 1, :], (EXPERTS_PER_GROUP, tt))
            for ib in range(N_EXPERT_GROUPS):
                mi = masked[ib]
                if ib < jb:
                    beats = rj > mi
                elif ib > jb:
                    beats = rj >= mi
                else:
                    beats = (rj > mi) | ((rj == mi) & (sub > float(jj)))
                ranks[ib] = ranks[ib] + jnp.where(beats, 1.0, 0.0)

    wsel = []
    tot = jnp.zeros((EXPERTS_PER_GROUP, tt), F32)
    for g in range(N_EXPERT_GROUPS):
        sc = scores[EXPERTS_PER_GROUP * g:EXPERTS_PER_GROUP * (g + 1), :]
        w = jnp.where(ranks[g] < float(TOP_K), sc, 0.0)
        wsel.append(w)
        tot = tot + w
    den = jnp.sum(tot, axis=0, keepdims=True)
    for g in range(N_EXPERT_GROUPS):
        gwt_ref[EXPERTS_PER_GROUP * g:EXPERTS_PER_GROUP * (g + 1), :] = wsel[g] / den * ROUTED_SCALE


def _post_kernel(y_ref, u_ref, a_ref, gs_ref, ga_ref, x_ref, g1_ref, sh2_ref, sc2_ref, g2_ref,
                 d_ref, wglu_ref, bglu_ref, wbs_ref, wba_ref, wout_ref, nf_ref, rwt_ref, rb_ref,
                 swg_ref, swu_ref, swd_ref, xp_ref, h2_ref, gwt_ref):
    ys = y_ref[...] + d_ref[...] * u_ref[...]
    z = jax.nn.gelu(ys)
    y2 = z * jax.nn.sigmoid(_dot(z.astype(BF16), wglu_ref[...]) + bglu_ref[...])
    merged = (jax.nn.sigmoid(gs_ref[...]) * _dot(y2.astype(BF16), wbs_ref[...])
              + jax.nn.sigmoid(ga_ref[...]) * _dot(a_ref[...], wba_ref[...]))
    x1 = x_ref[...] + g1_ref[...] * _dot(merged.astype(BF16), wout_ref[...])

    h2 = _rmsnorm(x1, nf_ref[...]) * (1.0 + sc2_ref[...]) + sh2_ref[...]
    hh, hl = _split(h2)
    h2_ref[...] = hh
    rh, rl = _split(rwt_ref[...])
    logits_t = _dot_nt(rh, hh) + (_dot_nt(rh, hl) + _dot_nt(rl, hh))
    _route(logits_t, rb_ref[...], gwt_ref)

    shared = _dot((jax.nn.silu(_dot(hh, swg_ref[...])) * _dot(hh, swu_ref[...])).astype(BF16), swd_ref[...])
    xp_ref[...] = x1 + g2_ref[...] * shared


def _post_call(y, u, attn, gs, ga, x, mods, lw, tt):
    bx, tx, _ = x.shape
    nt = tx // tt
    per_tok = mods[0].shape[1] != 1
    mod_spec = (pl.BlockSpec((None, tt, D_MODEL), lambda b, i: (b, i, 0)) if per_tok
                else pl.BlockSpec((None, 1, D_MODEL), lambda b, i: (b, 0, 0)))

    def tok_spec(w):
        return pl.BlockSpec((None, tt, w), lambda b, i: (b, i, 0))

    def full(a):
        return pl.BlockSpec(a.shape, lambda b, i: (0,) * a.ndim)

    weights = [lw['ssm_d'], lw['w_glu'], lw['b_glu'], lw['w_bs'], lw['w_ba'], lw['w_out'], lw['norm_ffn_g'],
               lw['router_wt'], lw['router_bias'], lw['sh_wg'], lw['sh_wu'], lw['sh_wd']]
    return pl.pallas_call(
        _post_kernel,
        grid=(bx, nt),
        in_specs=[tok_spec(D_SSM), tok_spec(D_SSM), tok_spec(ATTN_W), tok_spec(D_MODEL), tok_spec(D_MODEL),
                  tok_spec(D_MODEL), mod_spec, mod_spec, mod_spec, mod_spec] + [full(w) for w in weights],
        out_specs=[tok_spec(D_MODEL), tok_spec(D_MODEL),
                   pl.BlockSpec((N_EXPERTS, tt), lambda b, i: (0, b * nt + i))],
        out_shape=[jax.ShapeDtypeStruct((bx, tx, D_MODEL), F32), jax.ShapeDtypeStruct((bx, tx, D_MODEL), BF16),
                   jax.ShapeDtypeStruct((N_EXPERTS, bx * tx), F32)],
        compiler_params=_cparams("parallel", "parallel"),
        name="post",
    )(y, u, attn, gs, ga, x, *mods, *weights)


def _moe_kernel(h_ref, gw_ref, xp_ref, g2_ref, wg_ref, wu_ref, wd_ref, fg_ref, o_ref, *, eb, final):
    e = pl.program_id(1)

    @pl.when(e == 0)
    def _():
        o_ref[...] = jnp.zeros_like(o_ref)

    hb = h_ref[...]
    gw = gw_ref[...]
    lane = lax.broadcasted_iota(jnp.int32, gw.shape, 1)
    out = None
    for j in range(eb):
        col = jnp.sum(jnp.where(lane == e * eb + j, gw, 0.0), axis=1, keepdims=True)
        act = jax.nn.silu(_dot(hb, wg_ref[j])) * _dot(hb, wu_ref[j]) * col
        yj = _dot(act.astype(BF16), wd_ref[j])
        out = yj if out is None else out + yj
    o_ref[...] += out

    @pl.when(e == pl.num_programs(1) - 1)
    def _():
        x2 = xp_ref[...] + g2_ref[...] * o_ref[...]
        if final:
            x2 = _rmsnorm(x2, fg_ref[...])
        o_ref[...] = x2


def _moe_call(h2, gw, xp, g2, wg, wu, wd, final_g, tm, eb, final):
    n = h2.shape[0]
    nt = n // tm
    per_tok = g2.shape[1] != 1
    tiles_per_batch = (n // g2.shape[0]) // tm if not per_tok else 1
    g2_spec = (pl.BlockSpec((None, tm, D_MODEL), lambda i, e: (0, i, 0)) if per_tok
               else pl.BlockSpec((None, 1, D_MODEL), lambda i, e: (i // tiles_per_batch, 0, 0)))
    return pl.pallas_call(
        functools.partial(_moe_kernel, eb=eb, final=final),
        grid=(nt, N_EXPERTS // eb),
        in_specs=[
            pl.BlockSpec((tm, D_MODEL), lambda i, e: (i, 0)),
            pl.BlockSpec((tm, N_EXPERTS), lambda i, e: (i, 0)),
            pl.BlockSpec((tm, D_MODEL), lambda i, e: (i, 0)),
            g2_spec,
            pl.BlockSpec((eb, D_MODEL, D_EXPERT), lambda i, e: (e, 0, 0)),
            pl.BlockSpec((eb, D_MODEL, D_EXPERT), lambda i, e: (e, 0, 0)),
            pl.BlockSpec((eb, D_EXPERT, D_MODEL), lambda i, e: (e, 0, 0)),
            pl.BlockSpec((1, D_MODEL), lambda i, e: (0, 0)),
        ],
        out_specs=pl.BlockSpec((tm, D_MODEL), lambda i, e: (i, 0)),
        out_shape=jax.ShapeDtypeStruct((n, D_MODEL), F32),
        compiler_params=_cparams("parallel", "arbitrary"),
        name="moe",
    )(h2, gw, xp, g2, wg, wu, wd, final_g)


def _rope_tables(pos):
    half = HEAD_DIM // 2
    inv_freq = ROPE_THETA ** (-jnp.arange(half, dtype=F32) / half)
    ang = pos.astype(F32)[:, None] * inv_freq[None, :]
    cos = jnp.cos(ang)
    sin = jnp.sin(ang)
    return jnp.tile(cos, (1, 4)), jnp.tile(jnp.concatenate([-sin, sin], axis=1), (1, 2))


def _trunk(x, mods, pos, past_k, past_v, ssm_h0, layers, final_g, tt, tm, eb, tok_batches):
    bx, tx, _ = x.shape
    b, t = tok_batches
    cos_t, sin_t = _rope_tables(pos)
    new_k, new_v, new_h = [], [], []
    for l, lw in enumerate(layers):
        sh1, sc1, g1, sh2, sc2, g2 = mods[l]
        u, q, k, v, gs, ga = _inproj_call(x, sh1, sc1, lw['norm_attn_g'], lw['w_in'], cos_t, sin_t, tt)
        ks = k.reshape(b, t, KV_W)
        vs = v.reshape(b, t, KV_W)
        qs = q.reshape(b, t, ATTN_W)
        if past_k is None:
            attn = _attn_call(lw['sink'], qs, ks, ks, vs, vs, True)
            new_k.append(ks[:, -WINDOW:].reshape(b, WINDOW, N_KV_HEADS, HEAD_DIM))
            new_v.append(vs[:, -WINDOW:].reshape(b, WINDOW, N_KV_HEADS, HEAD_DIM))
            h0 = jnp.zeros((b, SSM_GROUPS, SSM_STATE, 2), F32)
        else:
            pk = past_k[l].reshape(b, -1, KV_W)
            pv = past_v[l].reshape(b, -1, KV_W)
            attn = _attn_call(lw['sink'], qs, pk, ks, pv, vs, False)
            new_k.append(ks.reshape(b, t, N_KV_HEADS, HEAD_DIM))
            new_v.append(vs.reshape(b, t, N_KV_HEADS, HEAD_DIM))
            h0 = ssm_h0[l]
        y, h_last = _ssm_branch(u.reshape(b, t, D_SSM), lw['ssm_mats'], h0)
        new_h.append(h_last)
        xp, h2, gwt = _post_call(y.reshape(bx, tx, D_SSM), u, attn.reshape(bx, tx, ATTN_W), gs, ga, x,
                                 (g1, sh2, sc2, g2), lw, tt)
        last = l == len(layers) - 1
        x = _moe_call(h2.reshape(bx * tx, D_MODEL), gwt.T, xp.reshape(bx * tx, D_MODEL), g2,
                      lw['ex_wg'], lw['ex_wu'], lw['ex_wd'], final_g, tm, eb, last).reshape(bx, tx, D_MODEL)
    return x, jnp.stack(new_k), jnp.stack(new_v), jnp.stack(new_h)


def kernel(x_prompt, x_sample, cache_k, cache_v, state_ssm, c_prompt, c_sample, ada_w, ada_b, norm_attn_g,
           norm_ffn_g, w_in, ssm_a_re, ssm_a_im, ssm_log_dt, ssm_b_re, ssm_b_im, ssm_c_re, ssm_c_im, ssm_d,
           ssm_w_glu, ssm_b_glu, attn_sink, w_branch_ssm, w_branch_attn, w_out, router_w, router_bias,
           expert_w_gate, expert_w_up, expert_w_down, shared_w_gate, shared_w_up, shared_w_down, final_g):
    depth = ada_w.shape[0]
    bp, tp, _ = x_prompt.shape
    bs, ts, _ = x_sample.shape

    layers = []
    for l in range(depth):
        layers.append({
            'norm_attn_g': norm_attn_g[l][None], 'norm_ffn_g': norm_ffn_g[l][None],
            'w_in': w_in[l].astype(BF16), 'sink': attn_sink[l],
            'ssm_mats': _ssm_mats(ssm_a_re[l], ssm_a_im[l], ssm_log_dt[l], ssm_b_re[l], ssm_b_im[l],
                                  ssm_c_re[l], ssm_c_im[l]),
            'ssm_d': ssm_d[l][None], 'w_glu': ssm_w_glu[l].astype(BF16), 'b_glu': ssm_b_glu[l][None],
            'w_bs': w_branch_ssm[l].astype(BF16), 'w_ba': w_branch_attn[l].astype(BF16),
            'w_out': w_out[l].astype(BF16), 'router_wt': router_w[l].T, 'router_bias': router_bias[l][:, None],
            'sh_wg': shared_w_gate[l].astype(BF16), 'sh_wu': shared_w_up[l].astype(BF16),
            'sh_wd': shared_w_down[l].astype(BF16),
            'ex_wg': expert_w_gate[l].astype(BF16), 'ex_wu': expert_w_up[l].astype(BF16),
            'ex_wd': expert_w_down[l].astype(BF16),
        })
    fg = final_g[None]

    rows = 16
    c_all = jnp.concatenate([c_prompt, c_sample, jnp.zeros((rows - bp - bs, D_MODEL), F32)], axis=0)
    mod = _mod_call(c_all, ada_w, ada_b).reshape(depth, rows, 6, D_MODEL)
    mods_p = [[mod[l, :bp, j][:, None, :] for j in range(6)] for l in range(depth)]
    mods_s = [[jnp.repeat(mod[l, bp:bp + bs, j], ts, axis=0)[None] for j in range(6)] for l in range(depth)]

    pos_p = jnp.arange(tp, dtype=jnp.int32)
    pos_s = jnp.tile(PAST_LEN + jnp.arange(ts, dtype=jnp.int32), bs)

    y_p, k_p, v_p, h_p = _trunk(x_prompt, mods_p, pos_p, None, None, None, layers, fg,
                                tt=512, tm=1024, eb=2, tok_batches=(bp, tp))
    y_s, k_s, v_s, h_s = _trunk(x_sample.reshape(1, bs * ts, D_MODEL), mods_s, pos_s, cache_k, cache_v,
                                state_ssm, layers, fg, tt=bs * ts, tm=bs * ts, eb=2, tok_batches=(bs, ts))
    return (y_p, y_s.reshape(bs, ts, D_MODEL), k_p, v_p, h_p, k_s, v_s, h_s)
```

```python
import functools
import math

import jax
import jax.numpy as jnp
from jax import lax
from jax.experimental import pallas as pl
from jax.experimental.pallas import tpu as pltpu

F32 = jnp.float32
BF16 = jnp.bfloat16

D_MODEL = 1024
DEPTH = 2
PAST_LEN = 4096
CHUNK = 64
N_HEADS = 8
N_KV_HEADS = 2
HEAD_DIM = 64
WINDOW = 128
ROPE_THETA = 10000.0
D_SSM = 512
SSM_GROUP_CH = 16
SSM_GROUPS = D_SSM // SSM_GROUP_CH
SSM_STATE = 64
N_EXPERTS = 64
N_EXPERT_GROUPS = 8
EXPERTS_PER_GROUP = N_EXPERTS // N_EXPERT_GROUPS
TOPK_GROUPS = 4
TOP_K = 8
D_EXPERT = 256
D_SHARED = 256
ROUTED_SCALE = 2.5
RMS_EPS = 1e-6
NEG_INF = -1e30
ATTN_W = N_HEADS * HEAD_DIM
KV_W = N_KV_HEADS * HEAD_DIM
IN_COLS = D_SSM + ATTN_W + 2 * KV_W + 2 * D_MODEL
IN_SPLITS = (0, D_SSM, D_SSM + ATTN_W, D_SSM + ATTN_W + KV_W, D_SSM + ATTN_W + 2 * KV_W,
             D_SSM + ATTN_W + 2 * KV_W + D_MODEL, IN_COLS)

LANES = 128
SSM_L = 16
SSM_OCT_G = LANES // SSM_GROUP_CH
SSM_OCTS = SSM_GROUPS // SSM_OCT_G
SSM_FLAT = SSM_L * LANES
SSM_SW = SSM_OCT_G * SSM_STATE
SSM_POW_ROWS = 16
VMEM_LIMIT = 56 * 1024 * 1024


def _cparams(*sem):
    return pltpu.CompilerParams(dimension_semantics=sem, vmem_limit_bytes=VMEM_LIMIT)


def _dot(a, b):
    return jnp.dot(a, b, preferred_element_type=F32)


def _dot_nt(a, b):
    return lax.dot_general(a, b, (((1,), (1,)), ((), ())), preferred_element_type=F32)


def _split(x):
    hi = x.astype(BF16)
    lo = (x - hi.astype(F32)).astype(BF16)
    return hi, lo


def _rmsnorm(x, g):
    return x * lax.rsqrt(jnp.mean(x * x, axis=-1, keepdims=True) + RMS_EPS) * g


def _mod_kernel(c_ref, w_ref, b_ref, o_ref):
    cond = jax.nn.silu(c_ref[...])
    ch, cl = _split(cond)
    wh, wl = _split(w_ref[...])
    o_ref[...] = _dot(ch, wh) + (_dot(ch, wl) + _dot(cl, wh)) + b_ref[...]


def _mod_call(c_all, ada_w, ada_b):
    depth = ada_w.shape[0]
    rows = c_all.shape[0]
    nj = 6
    return pl.pallas_call(
        _mod_kernel,
        grid=(depth, nj),
        in_specs=[
            pl.BlockSpec((rows, D_MODEL), lambda l, j: (0, 0)),
            pl.BlockSpec((None, D_MODEL, D_MODEL), lambda l, j: (l, 0, j)),
            pl.BlockSpec((None, 1, D_MODEL), lambda l, j: (l, 0, j)),
        ],
        out_specs=pl.BlockSpec((None, rows, D_MODEL), lambda l, j: (l, 0, j)),
        out_shape=jax.ShapeDtypeStruct((depth, rows, nj * D_MODEL), F32),
        compiler_params=_cparams("parallel", "parallel"),
        name="mod",
    )(c_all, ada_w, ada_b.reshape(depth, 1, nj * D_MODEL))


def _rope2(t, cos, sin_signed, first_half):
    swapped = jnp.where(first_half, pltpu.roll(t, LANES - HEAD_DIM // 2, axis=1),
                        pltpu.roll(t, HEAD_DIM // 2, axis=1))
    return t * cos + swapped * sin_signed


def _inproj_kernel(x_ref, sh_ref, sc_ref, g_ref, w_ref, cos_ref, sin_ref,
                   u_ref, q_ref, k_ref, v_ref, gs_ref, ga_ref):
    h = _rmsnorm(x_ref[...], g_ref[...]) * (1.0 + sc_ref[...]) + sh_ref[...]
    hb = h.astype(BF16)

    def proj(i):
        return _dot(hb, w_ref[:, IN_SPLITS[i]:IN_SPLITS[i + 1]])

    u_ref[...] = proj(0)
    cos = cos_ref[...]
    sin = sin_ref[...]
    lane = lax.broadcasted_iota(jnp.int32, cos.shape, 1)
    first_half = (lane % HEAD_DIM) < (HEAD_DIM // 2)
    q = proj(1)
    for j in range(ATTN_W // LANES):
        sl = slice(j * LANES, (j + 1) * LANES)
        q_ref[:, sl] = (_rope2(q[:, sl], cos, sin, first_half) * (HEAD_DIM ** -0.5)).astype(BF16)
    k_ref[...] = _rope2(proj(2), cos, sin, first_half)
    v_ref[...] = proj(3)
    gs_ref[...] = proj(4)
    ga_ref[...] = proj(5)


def _inproj_call(x, sh1, sc1, norm_g, w_in_bf, cos_t, sin_t, tt):
    bx, tx, _ = x.shape
    nt = tx // tt
    per_tok = sh1.shape[1] != 1
    mod_spec = (pl.BlockSpec((None, tt, D_MODEL), lambda b, i: (b, i, 0)) if per_tok
                else pl.BlockSpec((None, 1, D_MODEL), lambda b, i: (b, 0, 0)))

    def tok_spec(w):
        return pl.BlockSpec((None, tt, w), lambda b, i: (b, i, 0))

    def tok_shape(w, dt):
        return jax.ShapeDtypeStruct((bx, tx, w), dt)

    return pl.pallas_call(
        _inproj_kernel,
        grid=(bx, nt),
        in_specs=[
            tok_spec(D_MODEL), mod_spec, mod_spec,
            pl.BlockSpec((1, D_MODEL), lambda b, i: (0, 0)),
            pl.BlockSpec((D_MODEL, IN_COLS), lambda b, i: (0, 0)),
            pl.BlockSpec((tt, LANES), lambda b, i: (i, 0)),
            pl.BlockSpec((tt, LANES), lambda b, i: (i, 0)),
        ],
        out_specs=[tok_spec(D_SSM), tok_spec(ATTN_W), tok_spec(KV_W), tok_spec(KV_W),
                   tok_spec(D_MODEL), tok_spec(D_MODEL)],
        out_shape=[tok_shape(D_SSM, F32), tok_shape(ATTN_W, BF16), tok_shape(KV_W, F32),
                   tok_shape(KV_W, F32), tok_shape(D_MODEL, F32), tok_shape(D_MODEL, F32)],
        compiler_params=_cparams("parallel", "parallel"),
        name="inproj",
    )(x, sh1, sc1, norm_g, w_in_bf, cos_t, sin_t)


def _ssm_kernel(u_ref, t_ref, bl_ref, cl_ref, ap_ref, h0_ref, y_ref, s_ref, *, nb, cb):
    nc = nb * cb
    sw = SSM_SW
    uf = jnp.concatenate([u_ref[pl.ds(s, nc, stride=SSM_L), :].astype(BF16) for s in range(SSM_L)], axis=1)
    y = _dot(uf, t_ref[...])
    v = _dot(uf, bl_ref[...])
    xre = v[:, :sw]
    xim = v[:, sw:]

    row = lax.broadcasted_iota(jnp.int32, (nc, sw), 0)
    cidx = row & (cb - 1)
    bidx = row // cb
    h0 = h0_ref[...]
    h0re = jnp.zeros((nc, sw), F32)
    h0im = jnp.zeros((nc, sw), F32)
    for b in range(nb):
        h0re = jnp.where(bidx == b, h0[b:b + 1, :sw], h0re)
        h0im = jnp.where(bidx == b, h0[b:b + 1, sw:], h0im)
    first = cidx == 0
    are = ap_ref[0:1, :sw]
    aim = ap_ref[0:1, sw:]
    xre = xre + jnp.where(first, are * h0re - aim * h0im, 0.0)
    xim = xim + jnp.where(first, are * h0im + aim * h0re, 0.0)

    d = 1
    k = 0
    while d < cb:
        are = ap_ref[k:k + 1, :sw]
        aim = ap_ref[k:k + 1, sw:]
        keep = cidx >= d
        sre = jnp.where(keep, pltpu.roll(xre, d, axis=0), 0.0)
        sim = jnp.where(keep, pltpu.roll(xim, d, axis=0), 0.0)
        xre, xim = xre + (are * sre - aim * sim), xim + (are * sim + aim * sre)
        d *= 2
        k += 1

    for b in range(nb):
        r = b * cb + cb - 1
        s_ref[b:b + 1, :sw] = xre[r:r + 1, :]
        s_ref[b:b + 1, sw:] = xim[r:r + 1, :]

    pre = jnp.where(first, h0re, pltpu.roll(xre, 1, axis=0))
    pim = jnp.where(first, h0im, pltpu.roll(xim, 1, axis=0))
    y = y + _dot(pre.astype(BF16), cl_ref[:sw, :]) + _dot(pim.astype(BF16), cl_ref[sw:, :])
    for t in range(SSM_L):
        y_ref[pl.ds(t, nc, stride=SSM_L), :] = y[:, t * LANES:(t + 1) * LANES]


def _ssm_call(u, mats, h0o, nb, cb):
    tm, bl, cl, ap = mats
    bx, tx, _ = u.shape

    def const_spec(r, c):
        return pl.BlockSpec((None, r, c), lambda o, b: (o, 0, 0), pipeline_mode=pl.Buffered(1))

    tok_spec = pl.BlockSpec((None, tx, LANES), lambda o, b: (b, 0, o))
    st_spec = pl.BlockSpec((None, None, nb, 2 * SSM_SW), lambda o, b: (o, b, 0, 0))
    return pl.pallas_call(
        functools.partial(_ssm_kernel, nb=nb, cb=cb),
        grid=(SSM_OCTS, bx),
        in_specs=[tok_spec, const_spec(SSM_FLAT, SSM_FLAT), const_spec(SSM_FLAT, 2 * SSM_SW),
                  const_spec(2 * SSM_SW, SSM_FLAT), const_spec(SSM_POW_ROWS, 2 * SSM_SW), st_spec],
        out_specs=[tok_spec, st_spec],
        out_shape=[jax.ShapeDtypeStruct((bx, tx, D_SSM), F32),
                   jax.ShapeDtypeStruct((SSM_OCTS, bx, nb, 2 * SSM_SW), F32)],
        compiler_params=_cparams("arbitrary", "arbitrary"),
        name="ssm",
    )(u, tm, bl, cl, ap, h0o)


def _oct_lanes(a):
    lead = a.shape[:-2]
    a = a.reshape(*lead, SSM_OCTS, SSM_SW)
    return jnp.moveaxis(a, -2, 0)


def _ssm_mats(a_re, a_im, log_dt, b_re, b_im, c_re, c_im):
    hp = lax.Precision.HIGHEST
    dt = jnp.exp(log_dt)[:, None]
    lre = a_re * dt
    lim = a_im * dt
    mag = jnp.exp(lre)
    ab_re = mag * jnp.cos(lim)
    ab_im = mag * jnp.sin(lim)
    den = a_re * a_re + a_im * a_im
    n_re = ab_re - 1.0
    f_re = (n_re * a_re + ab_im * a_im) / den
    f_im = (ab_im * a_re - n_re * a_im) / den
    bb_re = f_re[..., None] * b_re - f_im[..., None] * b_im
    bb_im = f_re[..., None] * b_im + f_im[..., None] * b_re

    def lam_pow(tau):
        tau = tau[:, None, None]
        m = jnp.exp(tau * lre)
        return m * jnp.cos(tau * lim), m * jnp.sin(tau * lim)

    pw_re, pw_im = lam_pow(jnp.arange(SSM_L + 1, dtype=F32))
    cp_re = c_re[None] * pw_re[:, :, None, :] - c_im[None] * pw_im[:, :, None, :]
    cp_im = c_re[None] * pw_im[:, :, None, :] + c_im[None] * pw_re[:, :, None, :]
    kern = (jnp.einsum('tgcn,gnd->tgcd', cp_re, bb_re, precision=hp)
            - jnp.einsum('tgcn,gnd->tgcd', cp_im, bb_im, precision=hp))
    s_i = jnp.arange(SSM_L)[:, None]
    t_i = jnp.arange(SSM_L)[None, :]
    tg = kern[jnp.clip(t_i - s_i, 0, SSM_L)]
    tg = jnp.where((t_i >= s_i)[:, :, None, None, None], tg, 0.0)
    eye = jnp.eye(SSM_OCT_G, dtype=BF16)
    og = (SSM_OCTS, SSM_OCT_G)
    tg = tg.reshape(SSM_L, SSM_L, *og, SSM_GROUP_CH, SSM_GROUP_CH).astype(BF16)
    tm = jnp.einsum('stogcd,gh->osgdthc', tg, eye).reshape(SSM_OCTS, SSM_FLAT, SSM_FLAT)

    rev_re = pw_re[SSM_L - 1 - jnp.arange(SSM_L)]
    rev_im = pw_im[SSM_L - 1 - jnp.arange(SSM_L)]
    bbr = bb_re.transpose(0, 2, 1)[None]
    bbi = bb_im.transpose(0, 2, 1)[None]
    blr = rev_re[:, :, None, :] * bbr - rev_im[:, :, None, :] * bbi
    bli = rev_re[:, :, None, :] * bbi + rev_im[:, :, None, :] * bbr

    def state_cols(m):
        m = m.reshape(SSM_L, *og, SSM_GROUP_CH, SSM_STATE).astype(BF16)
        return jnp.einsum('sogdn,gh->osgdhn', m, eye).reshape(SSM_OCTS, SSM_FLAT, SSM_SW)

    bl = jnp.concatenate([state_cols(blr), state_cols(bli)], axis=2)

    def state_rows(m):
        m = m.reshape(SSM_L, *og, SSM_GROUP_CH, SSM_STATE).astype(BF16)
        return jnp.einsum('togcn,gh->ognthc', m, eye).reshape(SSM_OCTS, SSM_SW, SSM_FLAT)

    cl = jnp.concatenate([state_rows(cp_re[1:]), state_rows(-cp_im[1:])], axis=1)

    steps = float(SSM_L) * (2.0 ** jnp.arange(SSM_POW_ROWS, dtype=F32))
    ap_re, ap_im = lam_pow(steps)
    ap = jnp.concatenate([_oct_lanes(ap_re), _oct_lanes(ap_im)], axis=-1)
    return tm, bl, cl, ap


def _ssm_branch(u, mats, h0, nb, cb):
    bx = u.shape[0]
    h0o = jnp.concatenate([_oct_lanes(h0[..., 0]), _oct_lanes(h0[..., 1])], axis=-1)
    y, s = _ssm_call(u, mats, h0o.reshape(SSM_OCTS, bx, nb, 2 * SSM_SW), nb, cb)
    s = s.reshape(SSM_OCTS, bx * nb, 2, SSM_OCT_G, SSM_STATE).transpose(1, 0, 3, 4, 2)
    return y, s.reshape(bx * nb, SSM_GROUPS, SSM_STATE, 2)


def _attn_kernel(sink_ref, q_ref, ka_ref, kb_ref, va_ref, vb_ref, o_ref, *, banded):
    rows = q_ref.shape[0]
    nk = 2 * LANES
    if banded:
        kfull = jnp.concatenate([ka_ref[...], kb_ref[...]], axis=0)
        vfull = jnp.concatenate([va_ref[...], vb_ref[...]], axis=0)
    else:
        pad = jnp.zeros((nk - ka_ref.shape[0] - kb_ref.shape[0], KV_W), F32)
        kfull = jnp.concatenate([ka_ref[...], kb_ref[...], pad], axis=0)
        vfull = jnp.concatenate([va_ref[...], vb_ref[...], pad], axis=0)

    r_i = lax.broadcasted_iota(jnp.int32, (rows, nk), 0)
    c_i = lax.broadcasted_iota(jnp.int32, (rows, nk), 1)
    if banded:
        lo = (r_i // CHUNK) * CHUNK
        first_frame = jnp.where(pl.program_id(1) > 0, 0, WINDOW)
        valid = (c_i >= lo) & (c_i < lo + WINDOW + CHUNK) & (c_i >= first_frame)
    else:
        valid = c_i < (ka_ref.shape[0] + kb_ref.shape[0])

    lane = lax.broadcasted_iota(jnp.int32, (nk, KV_W), 1)
    low = lane < HEAD_DIM
    kroll = pltpu.roll(kfull, HEAD_DIM, axis=1)
    vroll = pltpu.roll(vfull, HEAD_DIM, axis=1)

    for g in range(N_KV_HEADS):
        k_lo, k_hi = (kfull, kroll) if g == 0 else (kroll, kfull)
        v_lo, v_hi = (vfull, vroll) if g == 0 else (vroll, vfull)
        kpad = (jnp.where(low, k_lo, 0.0).astype(BF16), jnp.where(low, 0.0, k_hi).astype(BF16))
        vpad = (jnp.where(low, v_lo, 0.0).astype(BF16), jnp.where(low, 0.0, v_hi).astype(BF16))
        for pp in range(2):
            slab = 2 * g + pp
            qp = q_ref[:, slab * LANES:(slab + 1) * LANES]
            acc = None
            for hh in range(2):
                sk = sink_ref[2 * slab + hh]
                s = jnp.where(valid, _dot_nt(qp, kpad[hh]), NEG_INF)
                m = jnp.maximum(jnp.max(s, axis=-1, keepdims=True), sk)
                p = jnp.exp(s - m)
                den = jnp.sum(p, axis=-1, keepdims=True) + jnp.exp(sk - m)
                o = _dot(p.astype(BF16), vpad[hh]) / den
                acc = o if acc is None else acc + o
            o_ref[:, slab * LANES:(slab + 1) * LANES] = acc.astype(BF16)


def _attn_call(sink, q, ka, kb, va, vb, banded):
    bx, tx, _ = q.shape
    if banded:
        rows = 2 * CHUNK
        nt = tx // rows
        grid = (bx, nt)
        q_spec = pl.BlockSpec((None, rows, ATTN_W), lambda b, i: (b, i, 0))
        prev = pl.BlockSpec((None, WINDOW, KV_W), lambda b, i: (b, jnp.maximum(i - 1, 0), 0))
        cur = pl.BlockSpec((None, rows, KV_W), lambda b, i: (b, i, 0))
    else:
        rows = tx
        grid = (bx, 1)
        q_spec = pl.BlockSpec((None, rows, ATTN_W), lambda b, i: (b, 0, 0))
        prev = pl.BlockSpec((None, ka.shape[1], KV_W), lambda b, i: (b, 0, 0))
        cur = pl.BlockSpec((None, rows, KV_W), lambda b, i: (b, 0, 0))
    return pl.pallas_call(
        functools.partial(_attn_kernel, banded=banded),
        grid=grid,
        in_specs=[pl.BlockSpec(memory_space=pltpu.SMEM), q_spec, prev, cur, prev, cur],
        out_specs=q_spec,
        out_shape=jax.ShapeDtypeStruct((bx, tx, ATTN_W), BF16),
        compiler_params=_cparams("parallel", "parallel"),
        name="attn",
    )(sink, q, ka, kb, va, vb)


def _route(logits_t, bias_col, gwt_ref):
    tt = logits_t.shape[1]
    scores = jax.nn.sigmoid(logits_t)
    biased = scores + bias_col
    sub = lax.broadcasted_iota(jnp.int32, (EXPERTS_PER_GROUP, tt), 0).astype(F32)
    ninf = float('-inf')
    blocks = [biased[EXPERTS_PER_GROUP * g:EXPERTS_PER_GROUP * (g + 1), :] for g in range(N_EXPERT_GROUPS)]

    gscore = jnp.zeros((N_EXPERT_GROUPS, tt), F32)
    for g in range(N_EXPERT_GROUPS):
        blk = blocks[g]
        m1 = jnp.max(blk, axis=0, keepdims=True)
        i1 = jnp.min(jnp.where(blk == m1, sub, float(EXPERTS_PER_GROUP)), axis=0, keepdims=True)
        m2 = jnp.max(jnp.where(sub == i1, ninf, blk), axis=0, keepdims=True)
        gscore = jnp.where(sub == float(g), jnp.broadcast_to(m1 + m2, gscore.shape), gscore)

    grank = jnp.zeros((N_EXPERT_GROUPS, tt), F32)
    for j in range(N_EXPERT_GROUPS):
        rj = jnp.broadcast_to(gscore[j:j + 1, :], gscore.shape)
        beats = (rj > gscore) | ((rj == gscore) & (sub > float(j)))
        grank = grank + jnp.where(beats, 1.0, 0.0)
    gsel = jnp.where(grank < float(TOPK_GROUPS), 1.0, 0.0)

    masked = []
    for g in range(N_EXPERT_GROUPS):
        on = jnp.broadcast_to(gsel[g:g + 1, :], blocks[g].shape) > 0.5
        masked.append(jnp.where(on, blocks[g], NEG_INF))

    ranks = [jnp.zeros((EXPERTS_PER_GROUP, tt), F32) for _ in range(N_EXPERT_GROUPS)]
    for jb in range(N_EXPERT_GROUPS):
        for jj in range(EXPERTS_PER_GROUP):
            rj = jnp.broadcast_to(masked[jb][jj:jj + 1, :], (EXPERTS_PER_GROUP, tt))
            for ib in range(N_EXPERT_GROUPS):
                mi = masked[ib]
                if ib < jb:
                    beats = rj > mi
                elif ib > jb:
                    beats = rj >= mi
                else:
                    beats = (rj > mi) | ((rj == mi) & (sub > float(jj)))
                ranks[ib] = ranks[ib] + jnp.where(beats, 1.0, 0.0)

    wsel = []
    tot = jnp.zeros((EXPERTS_PER_GROUP, tt), F32)
    for g in range(N_EXPERT_GROUPS):
        sc = scores[EXPERTS_PER_GROUP * g:EXPERTS_PER_GROUP * (g + 1), :]
        w = jnp.where(ranks[g] < float(TOP_K), sc, 0.0)
        wsel.append(w)
        tot = tot + w
    den = jnp.sum(tot, axis=0, keepdims=True)
    for g in range(N_EXPERT_GROUPS):
        gwt_ref[EXPERTS_PER_GROUP * g:EXPERTS_PER_GROUP * (g + 1), :] = wsel[g] / den * ROUTED_SCALE


def _post_kernel(y_ref, u_ref, a_ref, gs_ref, ga_ref, x_ref, g1_ref, sh2_ref, sc2_ref, g2_ref,
                 d_ref, wglu_ref, bglu_ref, wbs_ref, wba_ref, wout_ref, nf_ref, rwt_ref, rb_ref,
                 swg_ref, swu_ref, swd_ref, xp_ref, h2_ref, gwt_ref):
    ys = y_ref[...] + d_ref[...] * u_ref[...]
    z = jax.nn.gelu(ys)
    y2 = z * jax.nn.sigmoid(_dot(z.astype(BF16), wglu_ref[...]) + bglu_ref[...])
    merged = (jax.nn.sigmoid(gs_ref[...]) * _dot(y2.astype(BF16), wbs_ref[...])
              + jax.nn.sigmoid(ga_ref[...]) * _dot(a_ref[...], wba_ref[...]))
    x1 = x_ref[...] + g1_ref[...] * _dot(merged.astype(BF16), wout_ref[...])

    h2 = _rmsnorm(x1, nf_ref[...]) * (1.0 + sc2_ref[...]) + sh2_ref[...]
    hh, hl = _split(h2)
    h2_ref[...] = hh
    rh, rl = _split(rwt_ref[...])
    logits_t = _dot_nt(rh, hh) + (_dot_nt(rh, hl) + _dot_nt(rl, hh))
    _route(logits_t, rb_ref[...], gwt_ref)

    shared = _dot((jax.nn.silu(_dot(hh, swg_ref[...])) * _dot(hh, swu_ref[...])).astype(BF16), swd_ref[...])
    xp_ref[...] = x1 + g2_ref[...] * shared


def _post_call(y, u, attn, gs, ga, x, mods, lw, tt):
    bx, tx, _ = x.shape
    nt = tx // tt
    per_tok = mods[0].shape[1] != 1
    mod_spec = (pl.BlockSpec((None, tt, D_MODEL), lambda b, i: (b, i, 0)) if per_tok
                else pl.BlockSpec((None, 1, D_MODEL), lambda b, i: (b, 0, 0)))

    def tok_spec(w):
        return pl.BlockSpec((None, tt, w), lambda b, i: (b, i, 0))

    def full(a):
        return pl.BlockSpec(a.shape, lambda b, i: (0,) * a.ndim)

    weights = [lw['ssm_d'], lw['w_glu'], lw['b_glu'], lw['w_bs'], lw['w_ba'], lw['w_out'], lw['norm_ffn_g'],
               lw['router_wt'], lw['router_bias'], lw['sh_wg'], lw['sh_wu'], lw['sh_wd']]
    return pl.pallas_call(
        _post_kernel,
        grid=(bx, nt),
        in_specs=[tok_spec(D_SSM), tok_spec(D_SSM), tok_spec(ATTN_W), tok_spec(D_MODEL), tok_spec(D_MODEL),
                  tok_spec(D_MODEL), mod_spec, mod_spec, mod_spec, mod_spec] + [full(w) for w in weights],
        out_specs=[tok_spec(D_MODEL), tok_spec(D_MODEL),
                   pl.BlockSpec((N_EXPERTS, tt), lambda b, i: (0, b * nt + i))],
        out_shape=[jax.ShapeDtypeStruct((bx, tx, D_MODEL), F32), jax.ShapeDtypeStruct((bx, tx, D_MODEL), BF16),
                   jax.ShapeDtypeStruct((N_EXPERTS, bx * tx), F32)],
        compiler_params=_cparams("parallel", "parallel"),
        name="post",
    )(y, u, attn, gs, ga, x, *mods, *weights)


def _moe_kernel(h_ref, gw_ref, xp_ref, g2_ref, wg_ref, wu_ref, wd_ref, fg_ref, o_ref, *, eb, final):
    e = pl.program_id(1)

    @pl.when(e == 0)
    def _():
        o_ref[...] = jnp.zeros_like(o_ref)

    hb = h_ref[...]
    gw = gw_ref[...]
    lane = lax.broadcasted_iota(jnp.int32, gw.shape, 1)
    out = None
    for j in range(eb):
        col = jnp.sum(jnp.where(lane == e * eb + j, gw, 0.0), axis=1, keepdims=True)
        act = jax.nn.silu(_dot(hb, wg_ref[j])) * _dot(hb, wu_ref[j]) * col
        yj = _dot(act.astype(BF16), wd_ref[j])
        out = yj if out is None else out + yj
    o_ref[...] += out

    @pl.when(e == pl.num_programs(1) - 1)
    def _():
        x2 = xp_ref[...] + g2_ref[...] * o_ref[...]
        if final:
            x2 = _rmsnorm(x2, fg_ref[...])
        o_ref[...] = x2


def _moe_call(h2, gw, xp, g2, wg, wu, wd, final_g, tm, eb, final):
    n = h2.shape[0]
    nt = n // tm
    per_tok = g2.shape[1] != 1
    tiles_per_batch = (n // g2.shape[0]) // tm if not per_tok else 1
    g2_spec = (pl.BlockSpec((None, tm, D_MODEL), lambda i, e: (0, i, 0)) if per_tok
               else pl.BlockSpec((None, 1, D_MODEL), lambda i, e: (i // tiles_per_batch, 0, 0)))
    return pl.pallas_call(
        functools.partial(_moe_kernel, eb=eb, final=final),
        grid=(nt, N_EXPERTS // eb),
        in_specs=[
            pl.BlockSpec((tm, D_MODEL), lambda i, e: (i, 0)),
            pl.BlockSpec((tm, N_EXPERTS), lambda i, e: (i, 0)),
            pl.BlockSpec((tm, D_MODEL), lambda i, e: (i, 0)),
            g2_spec,
            pl.BlockSpec((eb, D_MODEL, D_EXPERT), lambda i, e: (e, 0, 0)),
            pl.BlockSpec((eb, D_MODEL, D_EXPERT), lambda i, e: (e, 0, 0)),
            pl.BlockSpec((eb, D_EXPERT, D_MODEL), lambda i, e: (e, 0, 0)),
            pl.BlockSpec((1, D_MODEL), lambda i, e: (0, 0)),
        ],
        out_specs=pl.BlockSpec((tm, D_MODEL), lambda i, e: (i, 0)),
        out_shape=jax.ShapeDtypeStruct((n, D_MODEL), F32),
        compiler_params=_cparams("parallel", "arbitrary"),
        name="moe",
    )(h2, gw, xp, g2, wg, wu, wd, final_g)


def _rope_tables(pos):
    half = HEAD_DIM // 2
    inv_freq = ROPE_THETA ** (-jnp.arange(half, dtype=F32) / half)
    ang = pos.astype(F32)[:, None] * inv_freq[None, :]
    cos = jnp.cos(ang)
    sin = jnp.sin(ang)
    return jnp.tile(cos, (1, 4)), jnp.tile(jnp.concatenate([-sin, sin], axis=1), (1, 2))


def _trunk(x, mods, pos, past_k, past_v, ssm_h0, layers, final_g, tt, tm, eb, tok_batches):
    bx, tx, _ = x.shape
    b, t = tok_batches
    cos_t, sin_t = _rope_tables(pos)
    new_k, new_v, new_h = [], [], []
    for l, lw in enumerate(layers):
        sh1, sc1, g1, sh2, sc2, g2 = mods[l]
        u, q, k, v, gs, ga = _inproj_call(x, sh1, sc1, lw['norm_attn_g'], lw['w_in'], cos_t, sin_t, tt)
        ks = k.reshape(b, t, KV_W)
        vs = v.reshape(b, t, KV_W)
        qs = q.reshape(b, t, ATTN_W)
        if past_k is None:
            attn = _attn_call(lw['sink'], qs, ks, ks, vs, vs, True)
            new_k.append(ks[:, -WINDOW:].reshape(b, WINDOW, N_KV_HEADS, HEAD_DIM))
            new_v.append(vs[:, -WINDOW:].reshape(b, WINDOW, N_KV_HEADS, HEAD_DIM))
            h0 = jnp.zeros((b, SSM_GROUPS, SSM_STATE, 2), F32)
        else:
            pk = past_k[l].reshape(b, -1, KV_W)
            pv = past_v[l].reshape(b, -1, KV_W)
            attn = _attn_call(lw['sink'], qs, pk, ks, pv, vs, False)
            new_k.append(ks.reshape(b, t, N_KV_HEADS, HEAD_DIM))
            new_v.append(vs.reshape(b, t, N_KV_HEADS, HEAD_DIM))
            h0 = ssm_h0[l]
        y, h_last = _ssm_branch(u, lw['ssm_mats'], h0, b // bx, t // SSM_L)
        new_h.append(h_last)
        xp, h2, gwt = _post_call(y, u, attn.reshape(bx, tx, ATTN_W), gs, ga, x,
                                 (g1, sh2, sc2, g2), lw, tt)
        last = l == len(layers) - 1
        x = _moe_call(h2.reshape(bx * tx, D_MODEL), gwt.T, xp.reshape(bx * tx, D_MODEL), g2,
                      lw['ex_wg'], lw['ex_wu'], lw['ex_wd'], final_g, tm, eb, last).reshape(bx, tx, D_MODEL)
    return x, jnp.stack(new_k), jnp.stack(new_v), jnp.stack(new_h)


def kernel(x_prompt, x_sample, cache_k, cache_v, state_ssm, c_prompt, c_sample, ada_w, ada_b, norm_attn_g,
           norm_ffn_g, w_in, ssm_a_re, ssm_a_im, ssm_log_dt, ssm_b_re, ssm_b_im, ssm_c_re, ssm_c_im, ssm_d,
           ssm_w_glu, ssm_b_glu, attn_sink, w_branch_ssm, w_branch_attn, w_out, router_w, router_bias,
           expert_w_gate, expert_w_up, expert_w_down, shared_w_gate, shared_w_up, shared_w_down, final_g):
    depth = ada_w.shape[0]
    bp, tp, _ = x_prompt.shape
    bs, ts, _ = x_sample.shape

    layers = []
    for l in range(depth):
        layers.append({
            'norm_attn_g': norm_attn_g[l][None], 'norm_ffn_g': norm_ffn_g[l][None],
            'w_in': w_in[l].astype(BF16), 'sink': attn_sink[l],
            'ssm_mats': _ssm_mats(ssm_a_re[l], ssm_a_im[l], ssm_log_dt[l], ssm_b_re[l], ssm_b_im[l],
                                  ssm_c_re[l], ssm_c_im[l]),
            'ssm_d': ssm_d[l][None], 'w_glu': ssm_w_glu[l].astype(BF16), 'b_glu': ssm_b_glu[l][None],
            'w_bs': w_branch_ssm[l].astype(BF16), 'w_ba': w_branch_attn[l].astype(BF16),
            'w_out': w_out[l].astype(BF16), 'router_wt': router_w[l].T, 'router_bias': router_bias[l][:, None],
            'sh_wg': shared_w_gate[l].astype(BF16), 'sh_wu': shared_w_up[l].astype(BF16),
            'sh_wd': shared_w_down[l].astype(BF16),
            'ex_wg': expert_w_gate[l].astype(BF16), 'ex_wu': expert_w_up[l].astype(BF16),
            'ex_wd': expert_w_down[l].astype(BF16),
        })
    fg = final_g[None]

    rows = 16
    c_all = jnp.concatenate([c_prompt, c_sample, jnp.zeros((rows - bp - bs, D_MODEL), F32)], axis=0)
    mod = _mod_call(c_all, ada_w, ada_b).reshape(depth, rows, 6, D_MODEL)
    mods_p = [[mod[l, :bp, j][:, None, :] for j in range(6)] for l in range(depth)]
    mods_s = [[jnp.repeat(mod[l, bp:bp + bs, j], ts, axis=0)[None] for j in range(6)] for l in range(depth)]

    pos_p = jnp.arange(tp, dtype=jnp.int32)
    pos_s = jnp.tile(PAST_LEN + jnp.arange(ts, dtype=jnp.int32), bs)

    y_p, k_p, v_p, h_p = _trunk(x_prompt, mods_p, pos_p, None, None, None, layers, fg,
                                tt=512, tm=1024, eb=2, tok_batches=(bp, tp))
    y_s, k_s, v_s, h_s = _trunk(x_sample.reshape(1, bs * ts, D_MODEL), mods_s, pos_s, cache_k, cache_v,
                                state_ssm, layers, fg, tt=bs * ts, tm=bs * ts, eb=2, tok_batches=(bs, ts))
    return (y_p, y_s.reshape(bs, ts, D_MODEL), k_p, v_p, h_p, k_s, v_s, h_s)
```

```python
import functools
import math

import jax
import jax.numpy as jnp
from jax import lax
from jax.experimental import pallas as pl
from jax.experimental.pallas import tpu as pltpu

F32 = jnp.float32
BF16 = jnp.bfloat16

D_MODEL = 1024
DEPTH = 2
PAST_LEN = 4096
CHUNK = 64
N_HEADS = 8
N_KV_HEADS = 2
HEAD_DIM = 64
WINDOW = 128
ROPE_THETA = 10000.0
D_SSM = 512
SSM_GROUP_CH = 16
SSM_GROUPS = D_SSM // SSM_GROUP_CH
SSM_STATE = 64
N_EXPERTS = 64
N_EXPERT_GROUPS = 8
EXPERTS_PER_GROUP = N_EXPERTS // N_EXPERT_GROUPS
TOPK_GROUPS = 4
TOP_K = 8
D_EXPERT = 256
D_SHARED = 256
ROUTED_SCALE = 2.5
RMS_EPS = 1e-6
NEG_INF = -1e30
ATTN_W = N_HEADS * HEAD_DIM
KV_W = N_KV_HEADS * HEAD_DIM
IN_COLS = D_SSM + ATTN_W + 2 * KV_W + 2 * D_MODEL
IN_SPLITS = (0, D_SSM, D_SSM + ATTN_W, D_SSM + ATTN_W + KV_W, D_SSM + ATTN_W + 2 * KV_W,
             D_SSM + ATTN_W + 2 * KV_W + D_MODEL, IN_COLS)

LANES = 128
SSM_L = 16
SSM_OCT_G = LANES // SSM_GROUP_CH
SSM_OCTS = SSM_GROUPS // SSM_OCT_G
SSM_FLAT = SSM_L * LANES
SSM_SW = SSM_OCT_G * SSM_STATE
SSM_POW_ROWS = 16
VMEM_LIMIT = 56 * 1024 * 1024
SUBLANES = 8
SLAB_ROWS = D_MODEL // LANES
MOE_BS = 128
MOE_STRIDE = MOE_BS + SUBLANES
MOE_RMW_UNROLL = 4


def _cparams(*sem):
    return pltpu.CompilerParams(dimension_semantics=sem, vmem_limit_bytes=VMEM_LIMIT)


def _dot(a, b):
    return jnp.dot(a, b, preferred_element_type=F32)


def _dot_nt(a, b):
    return lax.dot_general(a, b, (((1,), (1,)), ((), ())), preferred_element_type=F32)


def _split(x):
    hi = x.astype(BF16)
    lo = (x - hi.astype(F32)).astype(BF16)
    return hi, lo


def _rmsnorm(x, g):
    return x * lax.rsqrt(jnp.mean(x * x, axis=-1, keepdims=True) + RMS_EPS) * g


def _mod_kernel(c_ref, w_ref, b_ref, o_ref):
    cond = jax.nn.silu(c_ref[...])
    ch, cl = _split(cond)
    wh, wl = _split(w_ref[...])
    o_ref[...] = _dot(ch, wh) + (_dot(ch, wl) + _dot(cl, wh)) + b_ref[...]


def _mod_call(c_all, ada_w, ada_b):
    depth = ada_w.shape[0]
    rows = c_all.shape[0]
    nj = 6
    return pl.pallas_call(
        _mod_kernel,
        grid=(depth, nj),
        in_specs=[
            pl.BlockSpec((rows, D_MODEL), lambda l, j: (0, 0)),
            pl.BlockSpec((None, D_MODEL, D_MODEL), lambda l, j: (l, 0, j)),
            pl.BlockSpec((None, 1, D_MODEL), lambda l, j: (l, 0, j)),
        ],
        out_specs=pl.BlockSpec((None, rows, D_MODEL), lambda l, j: (l, 0, j)),
        out_shape=jax.ShapeDtypeStruct((depth, rows, nj * D_MODEL), F32),
        compiler_params=_cparams("parallel", "parallel"),
        name="mod",
    )(c_all, ada_w, ada_b.reshape(depth, 1, nj * D_MODEL))


def _rope2(t, cos, sin_signed, first_half):
    swapped = jnp.where(first_half, pltpu.roll(t, LANES - HEAD_DIM // 2, axis=1),
                        pltpu.roll(t, HEAD_DIM // 2, axis=1))
    return t * cos + swapped * sin_signed


def _inproj_kernel(x_ref, sh_ref, sc_ref, g_ref, w_ref, cos_ref, sin_ref,
                   u_ref, q_ref, k_ref, v_ref, gs_ref, ga_ref):
    h = _rmsnorm(x_ref[...], g_ref[...]) * (1.0 + sc_ref[...]) + sh_ref[...]
    hb = h.astype(BF16)

    def proj(i):
        return _dot(hb, w_ref[:, IN_SPLITS[i]:IN_SPLITS[i + 1]])

    u_ref[...] = proj(0)
    cos = cos_ref[...]
    sin = sin_ref[...]
    lane = lax.broadcasted_iota(jnp.int32, cos.shape, 1)
    first_half = (lane % HEAD_DIM) < (HEAD_DIM // 2)
    q = proj(1)
    for j in range(ATTN_W // LANES):
        sl = slice(j * LANES, (j + 1) * LANES)
        q_ref[:, sl] = (_rope2(q[:, sl], cos, sin, first_half) * (HEAD_DIM ** -0.5)).astype(BF16)
    k_ref[...] = _rope2(proj(2), cos, sin, first_half)
    v_ref[...] = proj(3)
    gs_ref[...] = proj(4)
    ga_ref[...] = proj(5)


def _inproj_call(x, sh1, sc1, norm_g, w_in_bf, cos_t, sin_t, tt):
    bx, tx, _ = x.shape
    nt = tx // tt
    per_tok = sh1.shape[1] != 1
    mod_spec = (pl.BlockSpec((None, tt, D_MODEL), lambda b, i: (b, i, 0)) if per_tok
                else pl.BlockSpec((None, 1, D_MODEL), lambda b, i: (b, 0, 0)))

    def tok_spec(w):
        return pl.BlockSpec((None, tt, w), lambda b, i: (b, i, 0))

    def tok_shape(w, dt):
        return jax.ShapeDtypeStruct((bx, tx, w), dt)

    return pl.pallas_call(
        _inproj_kernel,
        grid=(bx, nt),
        in_specs=[
            tok_spec(D_MODEL), mod_spec, mod_spec,
            pl.BlockSpec((1, D_MODEL), lambda b, i: (0, 0)),
            pl.BlockSpec((D_MODEL, IN_COLS), lambda b, i: (0, 0)),
            pl.BlockSpec((tt, LANES), lambda b, i: (i, 0)),
            pl.BlockSpec((tt, LANES), lambda b, i: (i, 0)),
        ],
        out_specs=[tok_spec(D_SSM), tok_spec(ATTN_W), tok_spec(KV_W), tok_spec(KV_W),
                   tok_spec(D_MODEL), tok_spec(D_MODEL)],
        out_shape=[tok_shape(D_SSM, F32), tok_shape(ATTN_W, BF16), tok_shape(KV_W, F32),
                   tok_shape(KV_W, F32), tok_shape(D_MODEL, F32), tok_shape(D_MODEL, F32)],
        compiler_params=_cparams("parallel", "parallel"),
        name="inproj",
    )(x, sh1, sc1, norm_g, w_in_bf, cos_t, sin_t)


def _ssm_kernel(u_ref, t_ref, bl_ref, cl_ref, ap_ref, h0_ref, y_ref, s_ref, *, nb, cb):
    nc = nb * cb
    sw = SSM_SW
    uf = jnp.concatenate([u_ref[pl.ds(s, nc, stride=SSM_L), :].astype(BF16) for s in range(SSM_L)], axis=1)
    y = _dot(uf, t_ref[...])
    v = _dot(uf, bl_ref[...])
    xre = v[:, :sw]
    xim = v[:, sw:]

    row = lax.broadcasted_iota(jnp.int32, (nc, sw), 0)
    cidx = row & (cb - 1)
    bidx = row // cb
    h0 = h0_ref[...]
    h0re = jnp.zeros((nc, sw), F32)
    h0im = jnp.zeros((nc, sw), F32)
    for b in range(nb):
        h0re = jnp.where(bidx == b, h0[b:b + 1, :sw], h0re)
        h0im = jnp.where(bidx == b, h0[b:b + 1, sw:], h0im)
    first = cidx == 0
    are = ap_ref[0:1, :sw]
    aim = ap_ref[0:1, sw:]
    xre = xre + jnp.where(first, are * h0re - aim * h0im, 0.0)
    xim = xim + jnp.where(first, are * h0im + aim * h0re, 0.0)

    d = 1
    k = 0
    while d < cb:
        are = ap_ref[k:k + 1, :sw]
        aim = ap_ref[k:k + 1, sw:]
        keep = cidx >= d
        sre = jnp.where(keep, pltpu.roll(xre, d, axis=0), 0.0)
        sim = jnp.where(keep, pltpu.roll(xim, d, axis=0), 0.0)
        xre, xim = xre + (are * sre - aim * sim), xim + (are * sim + aim * sre)
        d *= 2
        k += 1

    for b in range(nb):
        r = b * cb + cb - 1
        s_ref[b:b + 1, :sw] = xre[r:r + 1, :]
        s_ref[b:b + 1, sw:] = xim[r:r + 1, :]

    pre = jnp.where(first, h0re, pltpu.roll(xre, 1, axis=0))
    pim = jnp.where(first, h0im, pltpu.roll(xim, 1, axis=0))
    y = y + _dot(pre.astype(BF16), cl_ref[:sw, :]) + _dot(pim.astype(BF16), cl_ref[sw:, :])
    for t in range(SSM_L):
        y_ref[pl.ds(t, nc, stride=SSM_L), :] = y[:, t * LANES:(t + 1) * LANES]


def _ssm_call(u, mats, h0o, nb, cb):
    tm, bl, cl, ap = mats
    bx, tx, _ = u.shape

    def const_spec(r, c):
        return pl.BlockSpec((None, r, c), lambda o, b: (o, 0, 0), pipeline_mode=pl.Buffered(1))

    tok_spec = pl.BlockSpec((None, tx, LANES), lambda o, b: (b, 0, o))
    st_spec = pl.BlockSpec((None, None, nb, 2 * SSM_SW), lambda o, b: (o, b, 0, 0))
    return pl.pallas_call(
        functools.partial(_ssm_kernel, nb=nb, cb=cb),
        grid=(SSM_OCTS, bx),
        in_specs=[tok_spec, const_spec(SSM_FLAT, SSM_FLAT), const_spec(SSM_FLAT, 2 * SSM_SW),
                  const_spec(2 * SSM_SW, SSM_FLAT), const_spec(SSM_POW_ROWS, 2 * SSM_SW), st_spec],
        out_specs=[tok_spec, st_spec],
        out_shape=[jax.ShapeDtypeStruct((bx, tx, D_SSM), F32),
                   jax.ShapeDtypeStruct((SSM_OCTS, bx, nb, 2 * SSM_SW), F32)],
        compiler_params=_cparams("arbitrary", "arbitrary"),
        name="ssm",
    )(u, tm, bl, cl, ap, h0o)


def _oct_lanes(a):
    lead = a.shape[:-2]
    a = a.reshape(*lead, SSM_OCTS, SSM_SW)
    return jnp.moveaxis(a, -2, 0)


def _ssm_mats(a_re, a_im, log_dt, b_re, b_im, c_re, c_im):
    hp = lax.Precision.HIGHEST
    dt = jnp.exp(log_dt)[:, None]
    lre = a_re * dt
    lim = a_im * dt
    mag = jnp.exp(lre)
    ab_re = mag * jnp.cos(lim)
    ab_im = mag * jnp.sin(lim)
    den = a_re * a_re + a_im * a_im
    n_re = ab_re - 1.0
    f_re = (n_re * a_re + ab_im * a_im) / den
    f_im = (ab_im * a_re - n_re * a_im) / den
    bb_re = f_re[..., None] * b_re - f_im[..., None] * b_im
    bb_im = f_re[..., None] * b_im + f_im[..., None] * b_re

    def lam_pow(tau):
        tau = tau[:, None, None]
        m = jnp.exp(tau * lre)
        return m * jnp.cos(tau * lim), m * jnp.sin(tau * lim)

    pw_re, pw_im = lam_pow(jnp.arange(SSM_L + 1, dtype=F32))
    cp_re = c_re[None] * pw_re[:, :, None, :] - c_im[None] * pw_im[:, :, None, :]
    cp_im = c_re[None] * pw_im[:, :, None, :] + c_im[None] * pw_re[:, :, None, :]
    kern = (jnp.einsum('tgcn,gnd->tgcd', cp_re, bb_re, precision=hp)
            - jnp.einsum('tgcn,gnd->tgcd', cp_im, bb_im, precision=hp))
    eye = jnp.eye(SSM_OCT_G, dtype=F32)
    og = (SSM_OCTS, SSM_OCT_G)

    def slab_diag(m):
        lead = m.shape[:-3]
        a, b = m.shape[-2:]
        m = m.reshape(*lead, *og, a, b)
        m = m[..., :, None, :] * eye[:, None, :, None]
        return m.reshape(*lead, SSM_OCTS, SSM_OCT_G * a, SSM_OCT_G * b)

    kt = slab_diag(kern.transpose(0, 1, 3, 2)).astype(BF16)
    s_i = jnp.arange(SSM_L)[:, None]
    t_i = jnp.arange(SSM_L)[None, :]
    tm = kt[jnp.clip(t_i - s_i, 0, SSM_L)]
    tm = jnp.where((t_i >= s_i)[:, :, None, None, None], tm, jnp.zeros((), BF16))
    tm = tm.transpose(2, 0, 3, 1, 4).reshape(SSM_OCTS, SSM_FLAT, SSM_FLAT)

    bd_re = slab_diag(bb_re.transpose(0, 2, 1))[:, None]
    bd_im = slab_diag(bb_im.transpose(0, 2, 1))[:, None]
    rev = SSM_L - 1 - jnp.arange(SSM_L)
    rv_re = _oct_lanes(pw_re[rev])[:, :, None, :]
    rv_im = _oct_lanes(pw_im[rev])[:, :, None, :]
    bl = jnp.concatenate([(bd_re * rv_re - bd_im * rv_im).astype(BF16),
                          (bd_re * rv_im + bd_im * rv_re).astype(BF16)], axis=-1)
    bl = bl.reshape(SSM_OCTS, SSM_FLAT, 2 * SSM_SW)

    cd_re = slab_diag(c_re.transpose(0, 2, 1))[:, :, None, :]
    cd_im = slab_diag(c_im.transpose(0, 2, 1))[:, :, None, :]
    fw_re = _oct_lanes(pw_re[1:]).transpose(0, 2, 1)[..., None]
    fw_im = _oct_lanes(pw_im[1:]).transpose(0, 2, 1)[..., None]
    cl = jnp.concatenate([(cd_re * fw_re - cd_im * fw_im).astype(BF16),
                          (-(cd_re * fw_im + cd_im * fw_re)).astype(BF16)], axis=1)
    cl = cl.reshape(SSM_OCTS, 2 * SSM_SW, SSM_FLAT)

    steps = float(SSM_L) * (2.0 ** jnp.arange(SSM_POW_ROWS, dtype=F32))
    ap_re, ap_im = lam_pow(steps)
    ap = jnp.concatenate([_oct_lanes(ap_re), _oct_lanes(ap_im)], axis=-1)
    return tm, bl, cl, ap


def _ssm_branch(u, mats, h0, nb, cb):
    bx = u.shape[0]
    h0o = jnp.concatenate([_oct_lanes(h0[..., 0]), _oct_lanes(h0[..., 1])], axis=-1)
    y, s = _ssm_call(u, mats, h0o.reshape(SSM_OCTS, bx, nb, 2 * SSM_SW), nb, cb)
    s = s.reshape(SSM_OCTS, bx * nb, 2, SSM_OCT_G, SSM_STATE).transpose(1, 0, 3, 4, 2)
    return y, s.reshape(bx * nb, SSM_GROUPS, SSM_STATE, 2)


def _attn_kernel(sink_ref, q_ref, ka_ref, kb_ref, va_ref, vb_ref, o_ref, *, banded):
    rows = q_ref.shape[0]
    nk = 2 * LANES
    if banded:
        kfull = jnp.concatenate([ka_ref[...], kb_ref[...]], axis=0)
        vfull = jnp.concatenate([va_ref[...], vb_ref[...]], axis=0)
    else:
        pad = jnp.zeros((nk - ka_ref.shape[0] - kb_ref.shape[0], KV_W), F32)
        kfull = jnp.concatenate([ka_ref[...], kb_ref[...], pad], axis=0)
        vfull = jnp.concatenate([va_ref[...], vb_ref[...], pad], axis=0)

    r_i = lax.broadcasted_iota(jnp.int32, (rows, nk), 0)
    c_i = lax.broadcasted_iota(jnp.int32, (rows, nk), 1)
    if banded:
        lo = (r_i // CHUNK) * CHUNK
        first_frame = jnp.where(pl.program_id(1) > 0, 0, WINDOW)
        valid = (c_i >= lo) & (c_i < lo + WINDOW + CHUNK) & (c_i >= first_frame)
    else:
        valid = c_i < (ka_ref.shape[0] + kb_ref.shape[0])

    lane = lax.broadcasted_iota(jnp.int32, (nk, KV_W), 1)
    low = lane < HEAD_DIM
    kroll = pltpu.roll(kfull, HEAD_DIM, axis=1)
    vroll = pltpu.roll(vfull, HEAD_DIM, axis=1)

    for g in range(N_KV_HEADS):
        k_lo, k_hi = (kfull, kroll) if g == 0 else (kroll, kfull)
        v_lo, v_hi = (vfull, vroll) if g == 0 else (vroll, vfull)
        kpad = (jnp.where(low, k_lo, 0.0).astype(BF16), jnp.where(low, 0.0, k_hi).astype(BF16))
        vpad = (jnp.where(low, v_lo, 0.0).astype(BF16), jnp.where(low, 0.0, v_hi).astype(BF16))
        for pp in range(2):
            slab = 2 * g + pp
            qp = q_ref[:, slab * LANES:(slab + 1) * LANES]
            acc = None
            for hh in range(2):
                sk = sink_ref[2 * slab + hh]
                s = jnp.where(valid, _dot_nt(qp, kpad[hh]), NEG_INF)
                m = jnp.maximum(jnp.max(s, axis=-1, keepdims=True), sk)
                p = jnp.exp(s - m)
                den = jnp.sum(p, axis=-1, keepdims=True) + jnp.exp(sk - m)
                o = _dot(p.astype(BF16), vpad[hh]) / den
                acc = o if acc is None else acc + o
            o_ref[:, slab * LANES:(slab + 1) * LANES] = acc.astype(BF16)


def _attn_call(sink, q, ka, kb, va, vb, banded):
    bx, tx, _ = q.shape
    if banded:
        rows = 2 * CHUNK
        nt = tx // rows
        grid = (bx, nt)
        q_spec = pl.BlockSpec((None, rows, ATTN_W), lambda b, i: (b, i, 0))
        prev = pl.BlockSpec((None, WINDOW, KV_W), lambda b, i: (b, jnp.maximum(i - 1, 0), 0))
        cur = pl.BlockSpec((None, rows, KV_W), lambda b, i: (b, i, 0))
    else:
        rows = tx
        grid = (bx, 1)
        q_spec = pl.BlockSpec((None, rows, ATTN_W), lambda b, i: (b, 0, 0))
        prev = pl.BlockSpec((None, ka.shape[1], KV_W), lambda b, i: (b, 0, 0))
        cur = pl.BlockSpec((None, rows, KV_W), lambda b, i: (b, 0, 0))
    return pl.pallas_call(
        functools.partial(_attn_kernel, banded=banded),
        grid=grid,
        in_specs=[pl.BlockSpec(memory_space=pltpu.SMEM), q_spec, prev, cur, prev, cur],
        out_specs=q_spec,
        out_shape=jax.ShapeDtypeStruct((bx, tx, ATTN_W), BF16),
        compiler_params=_cparams("parallel", "parallel"),
        name="attn",
    )(sink, q, ka, kb, va, vb)


def _route(logits_t, bias_col, topi_ref, topw_ref):
    tt = logits_t.shape[1]
    scores = jax.nn.sigmoid(logits_t)
    biased = scores + bias_col
    sub = lax.broadcasted_iota(jnp.int32, (EXPERTS_PER_GROUP, tt), 0).astype(F32)
    ninf = float('-inf')
    blocks = [biased[EXPERTS_PER_GROUP * g:EXPERTS_PER_GROUP * (g + 1), :] for g in range(N_EXPERT_GROUPS)]

    gscore = jnp.zeros((N_EXPERT_GROUPS, tt), F32)
    for g in range(N_EXPERT_GROUPS):
        blk = blocks[g]
        m1 = jnp.max(blk, axis=0, keepdims=True)
        i1 = jnp.min(jnp.where(blk == m1, sub, float(EXPERTS_PER_GROUP)), axis=0, keepdims=True)
        m2 = jnp.max(jnp.where(sub == i1, ninf, blk), axis=0, keepdims=True)
        gscore = jnp.where(sub == float(g), jnp.broadcast_to(m1 + m2, gscore.shape), gscore)

    grank = jnp.zeros((N_EXPERT_GROUPS, tt), F32)
    for j in range(N_EXPERT_GROUPS):
        rj = jnp.broadcast_to(gscore[j:j + 1, :], gscore.shape)
        beats = (rj > gscore) | ((rj == gscore) & (sub > float(j)))
        grank = grank + jnp.where(beats, 1.0, 0.0)
    gsel = jnp.where(grank < float(TOPK_GROUPS), 1.0, 0.0)

    masked = []
    for g in range(N_EXPERT_GROUPS):
        on = jnp.broadcast_to(gsel[g:g + 1, :], blocks[g].shape) > 0.5
        masked.append(jnp.where(on, blocks[g], NEG_INF))

    ranks = [jnp.zeros((EXPERTS_PER_GROUP, tt), F32) for _ in range(N_EXPERT_GROUPS)]
    for jb in range(N_EXPERT_GROUPS):
        for jj in range(EXPERTS_PER_GROUP):
            rj = jnp.broadcast_to(masked[jb][jj:jj + 1, :], (EXPERTS_PER_GROUP, tt))
            for ib in range(N_EXPERT_GROUPS):
                mi = masked[ib]
                if ib < jb:
                    beats = rj > mi
                elif ib > jb:
                    beats = rj >= mi
                else:
                    beats = (rj > mi) | ((rj == mi) & (sub > float(jj)))
                ranks[ib] = ranks[ib] + jnp.where(beats, 1.0, 0.0)

    topi = jnp.zeros((TOP_K, tt), F32)
    topw = jnp.zeros((TOP_K, tt), F32)
    for r in range(TOP_K):
        ai = jnp.zeros((EXPERTS_PER_GROUP, tt), F32)
        aw = jnp.zeros((EXPERTS_PER_GROUP, tt), F32)
        for g in range(N_EXPERT_GROUPS):
            hit = ranks[g] == float(r)
            ai = ai + jnp.where(hit, sub + float(EXPERTS_PER_GROUP * g), 0.0)
            aw = aw + jnp.where(hit, scores[EXPERTS_PER_GROUP * g:EXPERTS_PER_GROUP * (g + 1), :], 0.0)
        ir = jnp.sum(ai, axis=0, keepdims=True)
        wr = jnp.sum(aw, axis=0, keepdims=True)
        topi = jnp.where(sub == float(r), jnp.broadcast_to(ir, topi.shape), topi)
        topw = jnp.where(sub == float(r), jnp.broadcast_to(wr, topw.shape), topw)
    den = jnp.sum(topw, axis=0, keepdims=True)
    topi_ref[...] = topi.astype(jnp.int32)
    topw_ref[...] = topw / den * ROUTED_SCALE


def _post_kernel(y_ref, u_ref, a_ref, gs_ref, ga_ref, x_ref, g1_ref, sh2_ref, sc2_ref, g2_ref,
                 d_ref, wglu_ref, bglu_ref, wbs_ref, wba_ref, wout_ref, nf_ref, rwt_ref, rb_ref,
                 swg_ref, swu_ref, swd_ref, xp_ref, h2s_ref, topi_ref, topw_ref):
    ys = y_ref[...] + d_ref[...] * u_ref[...]
    z = jax.nn.gelu(ys)
    y2 = z * jax.nn.sigmoid(_dot(z.astype(BF16), wglu_ref[...]) + bglu_ref[...])
    merged = (jax.nn.sigmoid(gs_ref[...]) * _dot(y2.astype(BF16), wbs_ref[...])
              + jax.nn.sigmoid(ga_ref[...]) * _dot(a_ref[...], wba_ref[...]))
    x1 = x_ref[...] + g1_ref[...] * _dot(merged.astype(BF16), wout_ref[...])

    h2 = _rmsnorm(x1, nf_ref[...]) * (1.0 + sc2_ref[...]) + sh2_ref[...]
    tt = h2.shape[0]
    for j in range(SLAB_ROWS):
        h2s_ref[pl.ds(j, tt, stride=SLAB_ROWS), :] = h2[:, j * LANES:(j + 1) * LANES]
    hh, hl = _split(h2)
    rh, rl = _split(rwt_ref[...])
    logits_t = _dot_nt(rh, hh) + (_dot_nt(rh, hl) + _dot_nt(rl, hh))
    _route(logits_t, rb_ref[...], topi_ref, topw_ref)

    shared = _dot((jax.nn.silu(_dot(hh, swg_ref[...])) * _dot(hh, swu_ref[...])).astype(BF16), swd_ref[...])
    xp_ref[...] = x1 + g2_ref[...] * shared


def _post_call(y, u, attn, gs, ga, x, mods, lw, tt):
    bx, tx, _ = x.shape
    nt = tx // tt
    per_tok = mods[0].shape[1] != 1
    mod_spec = (pl.BlockSpec((None, tt, D_MODEL), lambda b, i: (b, i, 0)) if per_tok
                else pl.BlockSpec((None, 1, D_MODEL), lambda b, i: (b, 0, 0)))

    def tok_spec(w):
        return pl.BlockSpec((None, tt, w), lambda b, i: (b, i, 0))

    def full(a):
        return pl.BlockSpec(a.shape, lambda b, i: (0,) * a.ndim)

    weights = [lw['ssm_d'], lw['w_glu'], lw['b_glu'], lw['w_bs'], lw['w_ba'], lw['w_out'], lw['norm_ffn_g'],
               lw['router_wt'], lw['router_bias'], lw['sh_wg'], lw['sh_wu'], lw['sh_wd']]
    return pl.pallas_call(
        _post_kernel,
        grid=(bx, nt),
        in_specs=[tok_spec(D_SSM), tok_spec(D_SSM), tok_spec(ATTN_W), tok_spec(D_MODEL), tok_spec(D_MODEL),
                  tok_spec(D_MODEL), mod_spec, mod_spec, mod_spec, mod_spec] + [full(w) for w in weights],
        out_specs=[tok_spec(D_MODEL),
                   pl.BlockSpec((tt * SLAB_ROWS, LANES), lambda b, i: (b * nt + i, 0)),
                   pl.BlockSpec((TOP_K, tt), lambda b, i: (0, b * nt + i)),
                   pl.BlockSpec((TOP_K, tt), lambda b, i: (0, b * nt + i))],
        out_shape=[jax.ShapeDtypeStruct((bx, tx, D_MODEL), F32),
                   jax.ShapeDtypeStruct((bx * tx * SLAB_ROWS, LANES), F32),
                   jax.ShapeDtypeStruct((TOP_K, bx * tx), jnp.int32),
                   jax.ShapeDtypeStruct((TOP_K, bx * tx), F32)],
        compiler_params=_cparams("parallel", "parallel"),
        name="post",
    )(y, u, attn, gs, ga, x, *mods, *weights)


def _moe_kernel(first_ref, nblk_ref, hs_ref, row_hbm, gate_hbm, wg_ref, wu_ref, wd_ref, o_ref,
                row_s, gate_s, xt_ref, yt_ref, acc_ref, sem, *, tm):
    i = pl.program_id(0)
    e = pl.program_id(1)
    nslot = row_s.shape[0]
    last_row = (tm - 1) * SLAB_ROWS

    def list_copies():
        src = pl.ds(pl.multiple_of(i * nslot, LANES), nslot)
        return (pltpu.make_async_copy(row_hbm.at[src], row_s, sem.at[0]),
                pltpu.make_async_copy(gate_hbm.at[src], gate_s, sem.at[1]))

    @pl.when(e == 0)
    def _():
        for c in list_copies():
            c.start()
        acc_ref[...] = jnp.zeros_like(acc_ref)
        for c in list_copies():
            c.wait()

    seg = i * N_EXPERTS + e
    s0 = first_ref[seg] - i * nslot

    def block(b, carry):
        base = s0 + b * MOE_BS
        for r in range(MOE_BS):
            row = pl.multiple_of(jnp.minimum(row_s[base + r], last_row), SLAB_ROWS)
            xt_ref[pl.ds(r, SLAB_ROWS, stride=MOE_STRIDE), :] = hs_ref[pl.ds(row, SLAB_ROWS), :]
        x = jnp.concatenate([xt_ref[j * MOE_STRIDE:j * MOE_STRIDE + MOE_BS, :] for j in range(SLAB_ROWS)],
                            axis=1).astype(BF16)
        act = jax.nn.silu(_dot(x, wg_ref[...])) * _dot(x, wu_ref[...])
        y = _dot(act.astype(BF16), wd_ref[...])
        for j in range(SLAB_ROWS):
            yt_ref[j * MOE_STRIDE:j * MOE_STRIDE + MOE_BS, :] = y[:, j * LANES:(j + 1) * LANES]
        for r0 in range(0, MOE_BS, MOE_RMW_UNROLL):
            upd = []
            for r in range(r0, r0 + MOE_RMW_UNROLL):
                row = pl.multiple_of(row_s[base + r], SLAB_ROWS)
                upd.append((row, acc_ref[pl.ds(row, SLAB_ROWS), :]
                            + gate_s[base + r] * yt_ref[pl.ds(r, SLAB_ROWS, stride=MOE_STRIDE), :]))
            for row, val in upd:
                acc_ref[pl.ds(row, SLAB_ROWS), :] = val
        return carry

    lax.fori_loop(0, nblk_ref[seg], block, 0)

    @pl.when(e == pl.num_programs(1) - 1)
    def _():
        o_ref[...] = acc_ref[0:tm * SLAB_ROWS, :]


def _moe_call(h2s, topi, topw, wg, wu, wd, tm):
    n = h2s.shape[0] // SLAB_ROWS
    nt = n // tm
    nseg = nt * N_EXPERTS
    npad = MOE_BS
    nslot = tm * TOP_K + N_EXPERTS * npad
    slot = jnp.arange(n * TOP_K, dtype=jnp.int32)
    key = 2 * ((slot // (tm * TOP_K)) * N_EXPERTS + topi.T.reshape(n * TOP_K))
    row = ((slot // TOP_K) % tm) * SLAB_ROWS
    pad_key = jnp.repeat(2 * jnp.arange(nseg, dtype=jnp.int32) + 1, npad)
    key_s, row_s, gate_s = lax.sort(
        (jnp.concatenate([key, pad_key]),
         jnp.concatenate([row, jnp.full((nseg * npad,), tm * SLAB_ROWS, jnp.int32)]),
         jnp.concatenate([topw.T.reshape(n * TOP_K), jnp.zeros((nseg * npad,), F32)])),
        num_keys=1, is_stable=False)
    bounds = jnp.searchsorted(key_s, jnp.arange(2 * nseg, dtype=jnp.int32), side='left').astype(jnp.int32)
    first = bounds[0::2]
    nblk = (bounds[1::2] - first + MOE_BS - 1) // MOE_BS

    grid_spec = pltpu.PrefetchScalarGridSpec(
        num_scalar_prefetch=2,
        grid=(nt, N_EXPERTS),
        in_specs=[
            pl.BlockSpec((tm * SLAB_ROWS, LANES), lambda i, e, f, nb: (i, 0)),
            pl.BlockSpec(memory_space=pl.ANY),
            pl.BlockSpec(memory_space=pl.ANY),
            pl.BlockSpec((None, D_MODEL, D_EXPERT), lambda i, e, f, nb: (e, 0, 0)),
            pl.BlockSpec((None, D_MODEL, D_EXPERT), lambda i, e, f, nb: (e, 0, 0)),
            pl.BlockSpec((None, D_EXPERT, D_MODEL), lambda i, e, f, nb: (e, 0, 0)),
        ],
        out_specs=pl.BlockSpec((tm * SLAB_ROWS, LANES), lambda i, e, f, nb: (i, 0)),
        scratch_shapes=[
            pltpu.SMEM((nslot,), jnp.int32),
            pltpu.SMEM((nslot,), F32),
            pltpu.VMEM((SLAB_ROWS * MOE_STRIDE, LANES), F32),
            pltpu.VMEM((SLAB_ROWS * MOE_STRIDE, LANES), F32),
            pltpu.VMEM(((tm + 1) * SLAB_ROWS, LANES), F32),
            pltpu.SemaphoreType.DMA((2,)),
        ],
    )
    return pl.pallas_call(
        functools.partial(_moe_kernel, tm=tm),
        grid_spec=grid_spec,
        out_shape=jax.ShapeDtypeStruct((n * SLAB_ROWS, LANES), F32),
        compiler_params=_cparams("arbitrary", "arbitrary"),
        name="moe",
    )(first, nblk, h2s, row_s, gate_s, wg, wu, wd)


def _combine_kernel(r_ref, xp_ref, g2_ref, fg_ref, o_ref, *, final):
    tt = xp_ref.shape[0]
    routed = jnp.concatenate([r_ref[pl.ds(j, tt, stride=SLAB_ROWS), :] for j in range(SLAB_ROWS)], axis=1)
    x2 = xp_ref[...] + g2_ref[...] * routed
    if final:
        x2 = _rmsnorm(x2, fg_ref[...])
    o_ref[...] = x2


def _combine_call(routed_s, xp, g2, final_g, tt, final):
    bx, tx, _ = xp.shape
    nt = tx // tt
    per_tok = g2.shape[1] != 1
    g2_spec = (pl.BlockSpec((None, tt, D_MODEL), lambda b, i: (b, i, 0)) if per_tok
               else pl.BlockSpec((None, 1, D_MODEL), lambda b, i: (b, 0, 0)))
    tok_spec = pl.BlockSpec((None, tt, D_MODEL), lambda b, i: (b, i, 0))
    return pl.pallas_call(
        functools.partial(_combine_kernel, final=final),
        grid=(bx, nt),
        in_specs=[pl.BlockSpec((tt * SLAB_ROWS, LANES), lambda b, i: (b * nt + i, 0)), tok_spec, g2_spec,
                  pl.BlockSpec((1, D_MODEL), lambda b, i: (0, 0))],
        out_specs=tok_spec,
        out_shape=jax.ShapeDtypeStruct((bx, tx, D_MODEL), F32),
        compiler_params=_cparams("parallel", "parallel"),
        name="combine",
    )(routed_s, xp, g2, final_g)


def _rope_tables(pos):
    half = HEAD_DIM // 2
    inv_freq = ROPE_THETA ** (-jnp.arange(half, dtype=F32) / half)
    ang = pos.astype(F32)[:, None] * inv_freq[None, :]
    cos = jnp.cos(ang)
    sin = jnp.sin(ang)
    return jnp.tile(cos, (1, 4)), jnp.tile(jnp.concatenate([-sin, sin], axis=1), (1, 2))


def _trunk(x, mods, pos, past_k, past_v, ssm_h0, layers, final_g, tt, tm, tok_batches):
    bx, tx, _ = x.shape
    b, t = tok_batches
    cos_t, sin_t = _rope_tables(pos)
    new_k, new_v, new_h = [], [], []
    for l, lw in enumerate(layers):
        sh1, sc1, g1, sh2, sc2, g2 = mods[l]
        u, q, k, v, gs, ga = _inproj_call(x, sh1, sc1, lw['norm_attn_g'], lw['w_in'], cos_t, sin_t, tt)
        ks = k.reshape(b, t, KV_W)
        vs = v.reshape(b, t, KV_W)
        qs = q.reshape(b, t, ATTN_W)
        if past_k is None:
            attn = _attn_call(lw['sink'], qs, ks, ks, vs, vs, True)
            new_k.append(ks[:, -WINDOW:].reshape(b, WINDOW, N_KV_HEADS, HEAD_DIM))
            new_v.append(vs[:, -WINDOW:].reshape(b, WINDOW, N_KV_HEADS, HEAD_DIM))
            h0 = jnp.zeros((b, SSM_GROUPS, SSM_STATE, 2), F32)
        else:
            pk = past_k[l].reshape(b, -1, KV_W)
            pv = past_v[l].reshape(b, -1, KV_W)
            attn = _attn_call(lw['sink'], qs, pk, ks, pv, vs, False)
            new_k.append(ks.reshape(b, t, N_KV_HEADS, HEAD_DIM))
            new_v.append(vs.reshape(b, t, N_KV_HEADS, HEAD_DIM))
            h0 = ssm_h0[l]
        y, h_last = _ssm_branch(u, lw['ssm_mats'], h0, b // bx, t // SSM_L)
        new_h.append(h_last)
        xp, h2s, topi, topw = _post_call(y, u, attn.reshape(bx, tx, ATTN_W), gs, ga, x,
                                         (g1, sh2, sc2, g2), lw, tt)
        routed_s = _moe_call(h2s, topi, topw, lw['ex_wg'], lw['ex_wu'], lw['ex_wd'], tm)
        x = _combine_call(routed_s, xp, g2, final_g, tt, l == len(layers) - 1)
    return x, jnp.stack(new_k), jnp.stack(new_v), jnp.stack(new_h)


def kernel(x_prompt, x_sample, cache_k, cache_v, state_ssm, c_prompt, c_sample, ada_w, ada_b, norm_attn_g,
           norm_ffn_g, w_in, ssm_a_re, ssm_a_im, ssm_log_dt, ssm_b_re, ssm_b_im, ssm_c_re, ssm_c_im, ssm_d,
           ssm_w_glu, ssm_b_glu, attn_sink, w_branch_ssm, w_branch_attn, w_out, router_w, router_bias,
           expert_w_gate, expert_w_up, expert_w_down, shared_w_gate, shared_w_up, shared_w_down, final_g):
    depth = ada_w.shape[0]
    bp, tp, _ = x_prompt.shape
    bs, ts, _ = x_sample.shape

    layers = []
    for l in range(depth):
        layers.append({
            'norm_attn_g': norm_attn_g[l][None], 'norm_ffn_g': norm_ffn_g[l][None],
            'w_in': w_in[l].astype(BF16), 'sink': attn_sink[l],
            'ssm_mats': _ssm_mats(ssm_a_re[l], ssm_a_im[l], ssm_log_dt[l], ssm_b_re[l], ssm_b_im[l],
                                  ssm_c_re[l], ssm_c_im[l]),
            'ssm_d': ssm_d[l][None], 'w_glu': ssm_w_glu[l].astype(BF16), 'b_glu': ssm_b_glu[l][None],
            'w_bs': w_branch_ssm[l].astype(BF16), 'w_ba': w_branch_attn[l].astype(BF16),
            'w_out': w_out[l].astype(BF16), 'router_wt': router_w[l].T, 'router_bias': router_bias[l][:, None],
            'sh_wg': shared_w_gate[l].astype(BF16), 'sh_wu': shared_w_up[l].astype(BF16),
            'sh_wd': shared_w_down[l].astype(BF16),
            'ex_wg': expert_w_gate[l].astype(BF16), 'ex_wu': expert_w_up[l].astype(BF16),
            'ex_wd': expert_w_down[l].astype(BF16),
        })
    fg = final_g[None]

    rows = 16
    c_all = jnp.concatenate([c_prompt, c_sample, jnp.zeros((rows - bp - bs, D_MODEL), F32)], axis=0)
    mod = _mod_call(c_all, ada_w, ada_b).reshape(depth, rows, 6, D_MODEL)
    mods_p = [[mod[l, :bp, j][:, None, :] for j in range(6)] for l in range(depth)]
    mods_s = [[jnp.repeat(mod[l, bp:bp + bs, j], ts, axis=0)[None] for j in range(6)] for l in range(depth)]

    pos_p = jnp.arange(tp, dtype=jnp.int32)
    pos_s = jnp.tile(PAST_LEN + jnp.arange(ts, dtype=jnp.int32), bs)

    y_p, k_p, v_p, h_p = _trunk(x_prompt, mods_p, pos_p, None, None, None, layers, fg,
                                tt=512, tm=min(2048, bp * tp), tok_batches=(bp, tp))
    y_s, k_s, v_s, h_s = _trunk(x_sample.reshape(1, bs * ts, D_MODEL), mods_s, pos_s, cache_k, cache_v,
                                state_ssm, layers, fg, tt=bs * ts, tm=bs * ts, tok_batches=(bs, ts))
    return (y_p, y_s.reshape(bs, ts, D_MODEL), k_p, v_p, h_p, k_s, v_s, h_s)
```

```python
import functools
import math

import jax
import jax.numpy as jnp
from jax import lax
from jax.experimental import pallas as pl
from jax.experimental.pallas import tpu as pltpu

F32 = jnp.float32
BF16 = jnp.bfloat16

D_MODEL = 1024
DEPTH = 2
PAST_LEN = 4096
CHUNK = 64
N_HEADS = 8
N_KV_HEADS = 2
HEAD_DIM = 64
WINDOW = 128
ROPE_THETA = 10000.0
D_SSM = 512
SSM_GROUP_CH = 16
SSM_GROUPS = D_SSM // SSM_GROUP_CH
SSM_STATE = 64
N_EXPERTS = 64
N_EXPERT_GROUPS = 8
EXPERTS_PER_GROUP = N_EXPERTS // N_EXPERT_GROUPS
TOPK_GROUPS = 4
TOP_K = 8
D_EXPERT = 256
D_SHARED = 256
ROUTED_SCALE = 2.5
RMS_EPS = 1e-6
NEG_INF = -1e30
ATTN_W = N_HEADS * HEAD_DIM
KV_W = N_KV_HEADS * HEAD_DIM
IN_COLS = D_SSM + ATTN_W + 2 * KV_W + 2 * D_MODEL
IN_SPLITS = (0, D_SSM, D_SSM + ATTN_W, D_SSM + ATTN_W + KV_W, D_SSM + ATTN_W + 2 * KV_W,
             D_SSM + ATTN_W + 2 * KV_W + D_MODEL, IN_COLS)

LANES = 128
SSM_L = 16
SSM_OCT_G = LANES // SSM_GROUP_CH
SSM_OCTS = SSM_GROUPS // SSM_OCT_G
SSM_FLAT = SSM_L * LANES
SSM_SW = SSM_OCT_G * SSM_STATE
SSM_POW_ROWS = 16
VMEM_LIMIT = 56 * 1024 * 1024
SUBLANES = 8
SLAB_ROWS = D_MODEL // LANES
MOE_BS = 128
MOE_STRIDE = MOE_BS + SUBLANES
MOE_RMW_UNROLL = 4


def _cparams(*sem):
    return pltpu.CompilerParams(dimension_semantics=sem, vmem_limit_bytes=VMEM_LIMIT)


def _dot(a, b):
    return jnp.dot(a, b, preferred_element_type=F32)


def _dot_nt(a, b):
    return lax.dot_general(a, b, (((1,), (1,)), ((), ())), preferred_element_type=F32)


def _split(x):
    hi = x.astype(BF16)
    lo = (x - hi.astype(F32)).astype(BF16)
    return hi, lo


def _rmsnorm(x, g):
    return x * lax.rsqrt(jnp.mean(x * x, axis=-1, keepdims=True) + RMS_EPS) * g


def _mod_kernel(c_ref, w_ref, b_ref, o_ref):
    cond = jax.nn.silu(c_ref[...])
    ch, cl = _split(cond)
    wh, wl = _split(w_ref[...])
    o_ref[...] = _dot(ch, wh) + (_dot(ch, wl) + _dot(cl, wh)) + b_ref[...]


def _mod_call(c_all, ada_w, ada_b):
    depth = ada_w.shape[0]
    rows = c_all.shape[0]
    nj = 6
    return pl.pallas_call(
        _mod_kernel,
        grid=(depth, nj),
        in_specs=[
            pl.BlockSpec((rows, D_MODEL), lambda l, j: (0, 0)),
            pl.BlockSpec((None, D_MODEL, D_MODEL), lambda l, j: (l, 0, j)),
            pl.BlockSpec((None, 1, D_MODEL), lambda l, j: (l, 0, j)),
        ],
        out_specs=pl.BlockSpec((None, rows, D_MODEL), lambda l, j: (l, 0, j)),
        out_shape=jax.ShapeDtypeStruct((depth, rows, nj * D_MODEL), F32),
        compiler_params=_cparams("parallel", "parallel"),
        name="mod",
    )(c_all, ada_w, ada_b.reshape(depth, 1, nj * D_MODEL))


def _rope2(t, cos, sin_signed, first_half):
    swapped = jnp.where(first_half, pltpu.roll(t, LANES - HEAD_DIM // 2, axis=1),
                        pltpu.roll(t, HEAD_DIM // 2, axis=1))
    return t * cos + swapped * sin_signed


def _inproj_kernel(x_ref, sh_ref, sc_ref, g_ref, w_ref, cos_ref, sin_ref,
                   u_ref, q_ref, k_ref, v_ref, gs_ref, ga_ref):
    h = _rmsnorm(x_ref[...], g_ref[...]) * (1.0 + sc_ref[...]) + sh_ref[...]
    hb = h.astype(BF16)

    def proj(i):
        return _dot(hb, w_ref[:, IN_SPLITS[i]:IN_SPLITS[i + 1]])

    u_ref[...] = proj(0)
    cos = cos_ref[...]
    sin = sin_ref[...]
    lane = lax.broadcasted_iota(jnp.int32, cos.shape, 1)
    first_half = (lane % HEAD_DIM) < (HEAD_DIM // 2)
    q = proj(1)
    for j in range(ATTN_W // LANES):
        sl = slice(j * LANES, (j + 1) * LANES)
        q_ref[:, sl] = (_rope2(q[:, sl], cos, sin, first_half) * (HEAD_DIM ** -0.5)).astype(BF16)
    k_ref[...] = _rope2(proj(2), cos, sin, first_half)
    v_ref[...] = proj(3)
    gs_ref[...] = proj(4)
    ga_ref[...] = proj(5)


def _inproj_call(x, sh1, sc1, norm_g, w_in_bf, cos_t, sin_t, tt):
    bx, tx, _ = x.shape
    nt = tx // tt
    per_tok = sh1.shape[1] != 1
    mod_spec = (pl.BlockSpec((None, tt, D_MODEL), lambda b, i: (b, i, 0)) if per_tok
                else pl.BlockSpec((None, 1, D_MODEL), lambda b, i: (b, 0, 0)))

    def tok_spec(w):
        return pl.BlockSpec((None, tt, w), lambda b, i: (b, i, 0))

    def tok_shape(w, dt):
        return jax.ShapeDtypeStruct((bx, tx, w), dt)

    return pl.pallas_call(
        _inproj_kernel,
        grid=(bx, nt),
        in_specs=[
            tok_spec(D_MODEL), mod_spec, mod_spec,
            pl.BlockSpec((1, D_MODEL), lambda b, i: (0, 0)),
            pl.BlockSpec((D_MODEL, IN_COLS), lambda b, i: (0, 0)),
            pl.BlockSpec((tt, LANES), lambda b, i: (i, 0)),
            pl.BlockSpec((tt, LANES), lambda b, i: (i, 0)),
        ],
        out_specs=[tok_spec(D_SSM), tok_spec(ATTN_W), tok_spec(KV_W), tok_spec(KV_W),
                   tok_spec(D_MODEL), tok_spec(D_MODEL)],
        out_shape=[tok_shape(D_SSM, F32), tok_shape(ATTN_W, BF16), tok_shape(KV_W, F32),
                   tok_shape(KV_W, F32), tok_shape(D_MODEL, F32), tok_shape(D_MODEL, F32)],
        compiler_params=_cparams("parallel", "parallel"),
        name="inproj",
    )(x, sh1, sc1, norm_g, w_in_bf, cos_t, sin_t)


def _ssm_kernel(u_ref, t_ref, bl_ref, cl_ref, ap_ref, h0_ref, y_ref, s_ref, *, nb, cb):
    nc = nb * cb
    sw = SSM_SW
    uf = jnp.concatenate([u_ref[pl.ds(s, nc, stride=SSM_L), :].astype(BF16) for s in range(SSM_L)], axis=1)
    y = _dot(uf, t_ref[...])
    v = _dot(uf, bl_ref[...])
    xre = v[:, :sw]
    xim = v[:, sw:]

    row = lax.broadcasted_iota(jnp.int32, (nc, sw), 0)
    cidx = row & (cb - 1)
    bidx = row // cb
    h0 = h0_ref[...]
    h0re = jnp.zeros((nc, sw), F32)
    h0im = jnp.zeros((nc, sw), F32)
    for b in range(nb):
        h0re = jnp.where(bidx == b, h0[b:b + 1, :sw], h0re)
        h0im = jnp.where(bidx == b, h0[b:b + 1, sw:], h0im)
    first = cidx == 0
    are = ap_ref[0:1, :sw]
    aim = ap_ref[0:1, sw:]
    xre = xre + jnp.where(first, are * h0re - aim * h0im, 0.0)
    xim = xim + jnp.where(first, are * h0im + aim * h0re, 0.0)

    d = 1
    k = 0
    while d < cb:
        are = ap_ref[k:k + 1, :sw]
        aim = ap_ref[k:k + 1, sw:]
        keep = cidx >= d
        sre = jnp.where(keep, pltpu.roll(xre, d, axis=0), 0.0)
        sim = jnp.where(keep, pltpu.roll(xim, d, axis=0), 0.0)
        xre, xim = xre + (are * sre - aim * sim), xim + (are * sim + aim * sre)
        d *= 2
        k += 1

    for b in range(nb):
        r = b * cb + cb - 1
        s_ref[b:b + 1, :sw] = xre[r:r + 1, :]
        s_ref[b:b + 1, sw:] = xim[r:r + 1, :]

    pre = jnp.where(first, h0re, pltpu.roll(xre, 1, axis=0))
    pim = jnp.where(first, h0im, pltpu.roll(xim, 1, axis=0))
    y = y + _dot(pre.astype(BF16), cl_ref[:sw, :]) + _dot(pim.astype(BF16), cl_ref[sw:, :])
    for t in range(SSM_L):
        y_ref[pl.ds(t, nc, stride=SSM_L), :] = y[:, t * LANES:(t + 1) * LANES]


def _ssm_call(u, mats, h0o, nb, cb):
    tm, bl, cl, ap = mats
    bx, tx, _ = u.shape

    def const_spec(r, c):
        return pl.BlockSpec((None, r, c), lambda o, b: (o, 0, 0), pipeline_mode=pl.Buffered(1))

    tok_spec = pl.BlockSpec((None, tx, LANES), lambda o, b: (b, 0, o))
    st_spec = pl.BlockSpec((None, None, nb, 2 * SSM_SW), lambda o, b: (o, b, 0, 0))
    return pl.pallas_call(
        functools.partial(_ssm_kernel, nb=nb, cb=cb),
        grid=(SSM_OCTS, bx),
        in_specs=[tok_spec, const_spec(SSM_FLAT, SSM_FLAT), const_spec(SSM_FLAT, 2 * SSM_SW),
                  const_spec(2 * SSM_SW, SSM_FLAT), const_spec(SSM_POW_ROWS, 2 * SSM_SW), st_spec],
        out_specs=[tok_spec, st_spec],
        out_shape=[jax.ShapeDtypeStruct((bx, tx, D_SSM), F32),
                   jax.ShapeDtypeStruct((SSM_OCTS, bx, nb, 2 * SSM_SW), F32)],
        compiler_params=_cparams("arbitrary", "arbitrary"),
        name="ssm",
    )(u, tm, bl, cl, ap, h0o)


def _oct_lanes(a):
    lead = a.shape[:-2]
    a = a.reshape(*lead, SSM_OCTS, SSM_SW)
    return jnp.moveaxis(a, -2, 0)


def _ssm_mats(a_re, a_im, log_dt, b_re, b_im, c_re, c_im):
    hp = lax.Precision.HIGHEST
    dt = jnp.exp(log_dt)[:, None]
    lre = a_re * dt
    lim = a_im * dt
    mag = jnp.exp(lre)
    ab_re = mag * jnp.cos(lim)
    ab_im = mag * jnp.sin(lim)
    den = a_re * a_re + a_im * a_im
    n_re = ab_re - 1.0
    f_re = (n_re * a_re + ab_im * a_im) / den
    f_im = (ab_im * a_re - n_re * a_im) / den
    bb_re = f_re[..., None] * b_re - f_im[..., None] * b_im
    bb_im = f_re[..., None] * b_im + f_im[..., None] * b_re

    def lam_pow(tau):
        tau = tau[:, None, None]
        m = jnp.exp(tau * lre)
        return m * jnp.cos(tau * lim), m * jnp.sin(tau * lim)

    pw_re, pw_im = lam_pow(jnp.arange(SSM_L + 1, dtype=F32))
    cp_re = c_re[None] * pw_re[:, :, None, :] - c_im[None] * pw_im[:, :, None, :]
    cp_im = c_re[None] * pw_im[:, :, None, :] + c_im[None] * pw_re[:, :, None, :]
    kern = (jnp.einsum('tgcn,gnd->tgcd', cp_re, bb_re, precision=hp)
            - jnp.einsum('tgcn,gnd->tgcd', cp_im, bb_im, precision=hp))
    eye = jnp.eye(SSM_OCT_G, dtype=F32)
    og = (SSM_OCTS, SSM_OCT_G)

    def slab_diag(m):
        lead = m.shape[:-3]
        a, b = m.shape[-2:]
        m = m.reshape(*lead, *og, a, b)
        m = m[..., :, None, :] * eye[:, None, :, None]
        return m.reshape(*lead, SSM_OCTS, SSM_OCT_G * a, SSM_OCT_G * b)

    kt = slab_diag(kern.transpose(0, 1, 3, 2)).astype(BF16)
    s_i = jnp.arange(SSM_L)[:, None]
    t_i = jnp.arange(SSM_L)[None, :]
    tm = kt[jnp.clip(t_i - s_i, 0, SSM_L)]
    tm = jnp.where((t_i >= s_i)[:, :, None, None, None], tm, jnp.zeros((), BF16))
    tm = tm.transpose(2, 0, 3, 1, 4).reshape(SSM_OCTS, SSM_FLAT, SSM_FLAT)

    bd_re = slab_diag(bb_re.transpose(0, 2, 1))[:, None]
    bd_im = slab_diag(bb_im.transpose(0, 2, 1))[:, None]
    rev = SSM_L - 1 - jnp.arange(SSM_L)
    rv_re = _oct_lanes(pw_re[rev])[:, :, None, :]
    rv_im = _oct_lanes(pw_im[rev])[:, :, None, :]
    bl = jnp.concatenate([(bd_re * rv_re - bd_im * rv_im).astype(BF16),
                          (bd_re * rv_im + bd_im * rv_re).astype(BF16)], axis=-1)
    bl = bl.reshape(SSM_OCTS, SSM_FLAT, 2 * SSM_SW)

    cd_re = slab_diag(c_re.transpose(0, 2, 1))[:, :, None, :]
    cd_im = slab_diag(c_im.transpose(0, 2, 1))[:, :, None, :]
    fw_re = _oct_lanes(pw_re[1:]).transpose(0, 2, 1)[..., None]
    fw_im = _oct_lanes(pw_im[1:]).transpose(0, 2, 1)[..., None]
    cl = jnp.concatenate([(cd_re * fw_re - cd_im * fw_im).astype(BF16),
                          (-(cd_re * fw_im + cd_im * fw_re)).astype(BF16)], axis=1)
    cl = cl.reshape(SSM_OCTS, 2 * SSM_SW, SSM_FLAT)

    steps = float(SSM_L) * (2.0 ** jnp.arange(SSM_POW_ROWS, dtype=F32))
    ap_re, ap_im = lam_pow(steps)
    ap = jnp.concatenate([_oct_lanes(ap_re), _oct_lanes(ap_im)], axis=-1)
    return tm, bl, cl, ap


def _ssm_branch(u, mats, h0, nb, cb):
    bx = u.shape[0]
    h0o = jnp.concatenate([_oct_lanes(h0[..., 0]), _oct_lanes(h0[..., 1])], axis=-1)
    y, s = _ssm_call(u, mats, h0o.reshape(SSM_OCTS, bx, nb, 2 * SSM_SW), nb, cb)
    s = s.reshape(SSM_OCTS, bx * nb, 2, SSM_OCT_G, SSM_STATE).transpose(1, 0, 3, 4, 2)
    return y, s.reshape(bx * nb, SSM_GROUPS, SSM_STATE, 2)


def _attn_kernel(sink_ref, q_ref, ka_ref, kb_ref, va_ref, vb_ref, o_ref, *, banded):
    rows = q_ref.shape[0]
    nk = 2 * LANES
    if banded:
        kfull = jnp.concatenate([ka_ref[...], kb_ref[...]], axis=0)
        vfull = jnp.concatenate([va_ref[...], vb_ref[...]], axis=0)
    else:
        pad = jnp.zeros((nk - ka_ref.shape[0] - kb_ref.shape[0], KV_W), F32)
        kfull = jnp.concatenate([ka_ref[...], kb_ref[...], pad], axis=0)
        vfull = jnp.concatenate([va_ref[...], vb_ref[...], pad], axis=0)

    r_i = lax.broadcasted_iota(jnp.int32, (rows, nk), 0)
    c_i = lax.broadcasted_iota(jnp.int32, (rows, nk), 1)
    if banded:
        lo = (r_i // CHUNK) * CHUNK
        first_frame = jnp.where(pl.program_id(1) > 0, 0, WINDOW)
        valid = (c_i >= lo) & (c_i < lo + WINDOW + CHUNK) & (c_i >= first_frame)
    else:
        valid = c_i < (ka_ref.shape[0] + kb_ref.shape[0])

    lane = lax.broadcasted_iota(jnp.int32, (nk, KV_W), 1)
    low = lane < HEAD_DIM
    kroll = pltpu.roll(kfull, HEAD_DIM, axis=1)
    vroll = pltpu.roll(vfull, HEAD_DIM, axis=1)

    for g in range(N_KV_HEADS):
        k_lo, k_hi = (kfull, kroll) if g == 0 else (kroll, kfull)
        v_lo, v_hi = (vfull, vroll) if g == 0 else (vroll, vfull)
        kpad = (jnp.where(low, k_lo, 0.0).astype(BF16), jnp.where(low, 0.0, k_hi).astype(BF16))
        vpad = (jnp.where(low, v_lo, 0.0).astype(BF16), jnp.where(low, 0.0, v_hi).astype(BF16))
        for pp in range(2):
            slab = 2 * g + pp
            qp = q_ref[:, slab * LANES:(slab + 1) * LANES]
            acc = None
            for hh in range(2):
                sk = sink_ref[2 * slab + hh]
                s = jnp.where(valid, _dot_nt(qp, kpad[hh]), NEG_INF)
                m = jnp.maximum(jnp.max(s, axis=-1, keepdims=True), sk)
                p = jnp.exp(s - m)
                den = jnp.sum(p, axis=-1, keepdims=True) + jnp.exp(sk - m)
                o = _dot(p.astype(BF16), vpad[hh]) / den
                acc = o if acc is None else acc + o
            o_ref[:, slab * LANES:(slab + 1) * LANES] = acc.astype(BF16)


def _attn_call(sink, q, ka, kb, va, vb, banded):
    bx, tx, _ = q.shape
    if banded:
        rows = 2 * CHUNK
        nt = tx // rows
        grid = (bx, nt)
        q_spec = pl.BlockSpec((None, rows, ATTN_W), lambda b, i: (b, i, 0))
        prev = pl.BlockSpec((None, WINDOW, KV_W), lambda b, i: (b, jnp.maximum(i - 1, 0), 0))
        cur = pl.BlockSpec((None, rows, KV_W), lambda b, i: (b, i, 0))
    else:
        rows = tx
        grid = (bx, 1)
        q_spec = pl.BlockSpec((None, rows, ATTN_W), lambda b, i: (b, 0, 0))
        prev = pl.BlockSpec((None, ka.shape[1], KV_W), lambda b, i: (b, 0, 0))
        cur = pl.BlockSpec((None, rows, KV_W), lambda b, i: (b, 0, 0))
    return pl.pallas_call(
        functools.partial(_attn_kernel, banded=banded),
        grid=grid,
        in_specs=[pl.BlockSpec(memory_space=pltpu.SMEM), q_spec, prev, cur, prev, cur],
        out_specs=q_spec,
        out_shape=jax.ShapeDtypeStruct((bx, tx, ATTN_W), BF16),
        compiler_params=_cparams("parallel", "parallel"),
        name="attn",
    )(sink, q, ka, kb, va, vb)


def _route(logits_t, bias_col, topi_ref, topw_ref):
    tt = logits_t.shape[1]
    scores = jax.nn.sigmoid(logits_t)
    biased = scores + bias_col
    sub = lax.broadcasted_iota(jnp.int32, (EXPERTS_PER_GROUP, tt), 0).astype(F32)
    ninf = float('-inf')
    blocks = [biased[EXPERTS_PER_GROUP * g:EXPERTS_PER_GROUP * (g + 1), :] for g in range(N_EXPERT_GROUPS)]

    gscore = jnp.zeros((N_EXPERT_GROUPS, tt), F32)
    for g in range(N_EXPERT_GROUPS):
        blk = blocks[g]
        m1 = jnp.max(blk, axis=0, keepdims=True)
        i1 = jnp.min(jnp.where(blk == m1, sub, float(EXPERTS_PER_GROUP)), axis=0, keepdims=True)
        m2 = jnp.max(jnp.where(sub == i1, ninf, blk), axis=0, keepdims=True)
        gscore = jnp.where(sub == float(g), jnp.broadcast_to(m1 + m2, gscore.shape), gscore)

    grank = jnp.zeros((N_EXPERT_GROUPS, tt), F32)
    for j in range(N_EXPERT_GROUPS):
        rj = jnp.broadcast_to(gscore[j:j + 1, :], gscore.shape)
        beats = (rj > gscore) | ((rj == gscore) & (sub > float(j)))
        grank = grank + jnp.where(beats, 1.0, 0.0)
    gsel = jnp.where(grank < float(TOPK_GROUPS), 1.0, 0.0)

    masked = []
    for g in range(N_EXPERT_GROUPS):
        on = jnp.broadcast_to(gsel[g:g + 1, :], blocks[g].shape) > 0.5
        masked.append(jnp.where(on, blocks[g], NEG_INF))

    ranks = [jnp.zeros((EXPERTS_PER_GROUP, tt), F32) for _ in range(N_EXPERT_GROUPS)]
    for jb in range(N_EXPERT_GROUPS):
        for jj in range(EXPERTS_PER_GROUP):
            rj = jnp.broadcast_to(masked[jb][jj:jj + 1, :], (EXPERTS_PER_GROUP, tt))
            for ib in range(N_EXPERT_GROUPS):
                mi = masked[ib]
                if ib < jb:
                    beats = rj > mi
                elif ib > jb:
                    beats = rj >= mi
                else:
                    beats = (rj > mi) | ((rj == mi) & (sub > float(jj)))
                ranks[ib] = ranks[ib] + jnp.where(beats, 1.0, 0.0)

    topi = jnp.zeros((TOP_K, tt), F32)
    topw = jnp.zeros((TOP_K, tt), F32)
    for r in range(TOP_K):
        ai = jnp.zeros((EXPERTS_PER_GROUP, tt), F32)
        aw = jnp.zeros((EXPERTS_PER_GROUP, tt), F32)
        for g in range(N_EXPERT_GROUPS):
            hit = ranks[g] == float(r)
            ai = ai + jnp.where(hit, sub + float(EXPERTS_PER_GROUP * g), 0.0)
            aw = aw + jnp.where(hit, scores[EXPERTS_PER_GROUP * g:EXPERTS_PER_GROUP * (g + 1), :], 0.0)
        ir = jnp.sum(ai, axis=0, keepdims=True)
        wr = jnp.sum(aw, axis=0, keepdims=True)
        topi = jnp.where(sub == float(r), jnp.broadcast_to(ir, topi.shape), topi)
        topw = jnp.where(sub == float(r), jnp.broadcast_to(wr, topw.shape), topw)
    den = jnp.sum(topw, axis=0, keepdims=True)
    topi_ref[...] = topi.astype(jnp.int32)
    topw_ref[...] = topw / den * ROUTED_SCALE


def _post_kernel(y_ref, u_ref, a_ref, gs_ref, ga_ref, x_ref, g1_ref, sh2_ref, sc2_ref, g2_ref,
                 d_ref, wglu_ref, bglu_ref, wbs_ref, wba_ref, wout_ref, nf_ref, rwt_ref, rb_ref,
                 swg_ref, swu_ref, swd_ref, xp_ref, h2s_ref, topi_ref, topw_ref):
    ys = y_ref[...] + d_ref[...] * u_ref[...]
    z = jax.nn.gelu(ys)
    y2 = z * jax.nn.sigmoid(_dot(z.astype(BF16), wglu_ref[...]) + bglu_ref[...])
    merged = (jax.nn.sigmoid(gs_ref[...]) * _dot(y2.astype(BF16), wbs_ref[...])
              + jax.nn.sigmoid(ga_ref[...]) * _dot(a_ref[...], wba_ref[...]))
    x1 = x_ref[...] + g1_ref[...] * _dot(merged.astype(BF16), wout_ref[...])

    h2 = _rmsnorm(x1, nf_ref[...]) * (1.0 + sc2_ref[...]) + sh2_ref[...]
    tt = h2.shape[0]
    for j in range(SLAB_ROWS):
        h2s_ref[pl.ds(j, tt, stride=SLAB_ROWS), :] = h2[:, j * LANES:(j + 1) * LANES]
    hh, hl = _split(h2)
    rh, rl = _split(rwt_ref[...])
    logits_t = _dot_nt(rh, hh) + (_dot_nt(rh, hl) + _dot_nt(rl, hh))
    _route(logits_t, rb_ref[...], topi_ref, topw_ref)

    shared = _dot((jax.nn.silu(_dot(hh, swg_ref[...])) * _dot(hh, swu_ref[...])).astype(BF16), swd_ref[...])
    xp_ref[...] = x1 + g2_ref[...] * shared


def _post_call(y, u, attn, gs, ga, x, mods, lw, tt):
    bx, tx, _ = x.shape
    nt = tx // tt
    per_tok = mods[0].shape[1] != 1
    mod_spec = (pl.BlockSpec((None, tt, D_MODEL), lambda b, i: (b, i, 0)) if per_tok
                else pl.BlockSpec((None, 1, D_MODEL), lambda b, i: (b, 0, 0)))

    def tok_spec(w):
        return pl.BlockSpec((None, tt, w), lambda b, i: (b, i, 0))

    def full(a):
        return pl.BlockSpec(a.shape, lambda b, i: (0,) * a.ndim)

    weights = [lw['ssm_d'], lw['w_glu'], lw['b_glu'], lw['w_bs'], lw['w_ba'], lw['w_out'], lw['norm_ffn_g'],
               lw['router_wt'], lw['router_bias'], lw['sh_wg'], lw['sh_wu'], lw['sh_wd']]
    return pl.pallas_call(
        _post_kernel,
        grid=(bx, nt),
        in_specs=[tok_spec(D_SSM), tok_spec(D_SSM), tok_spec(ATTN_W), tok_spec(D_MODEL), tok_spec(D_MODEL),
                  tok_spec(D_MODEL), mod_spec, mod_spec, mod_spec, mod_spec] + [full(w) for w in weights],
        out_specs=[tok_spec(D_MODEL),
                   pl.BlockSpec((tt * SLAB_ROWS, LANES), lambda b, i: (b * nt + i, 0)),
                   pl.BlockSpec((TOP_K, tt), lambda b, i: (0, b * nt + i)),
                   pl.BlockSpec((TOP_K, tt), lambda b, i: (0, b * nt + i))],
        out_shape=[jax.ShapeDtypeStruct((bx, tx, D_MODEL), F32),
                   jax.ShapeDtypeStruct((bx * tx * SLAB_ROWS, LANES), F32),
                   jax.ShapeDtypeStruct((TOP_K, bx * tx), jnp.int32),
                   jax.ShapeDtypeStruct((TOP_K, bx * tx), F32)],
        compiler_params=_cparams("parallel", "parallel"),
        name="post",
    )(y, u, attn, gs, ga, x, *mods, *weights)


def _moe_kernel(ptab_ref, pstart_ref, npair_ref, hs_ref, row_hbm, gate_hbm, wg_ref, wu_ref, wd_ref, o_ref,
                row_s, gate_s, xt0, xt1, yt0, yt1, sem, *, tm, ptab_w):
    i = pl.program_id(0)
    e = pl.program_id(1)
    nslot = row_s.shape[0]
    acc_ref = o_ref
    row_mask = tm * SLAB_ROWS - 1
    tab0 = i * ptab_w + 1

    def list_copies():
        src = pl.ds(pl.multiple_of(i * nslot, LANES), nslot)
        return (pltpu.make_async_copy(row_hbm.at[src], row_s, sem.at[0]),
                pltpu.make_async_copy(gate_hbm.at[src], gate_s, sem.at[1]))

    def gather(base, xt_ref):
        for r in range(MOE_BS):
            row = pl.multiple_of(row_s[base + r] & row_mask, SLAB_ROWS)
            xt_ref[pl.ds(r, SLAB_ROWS, stride=MOE_STRIDE), :] = hs_ref[pl.ds(row, SLAB_ROWS), :]

    def expert(xt_ref, yt_ref):
        x = jnp.concatenate([xt_ref[j * MOE_STRIDE:j * MOE_STRIDE + MOE_BS, :] for j in range(SLAB_ROWS)],
                            axis=1).astype(BF16)
        act = jax.nn.silu(_dot(x, wg_ref[...])) * _dot(x, wu_ref[...])
        y = _dot(act.astype(BF16), wd_ref[...])
        for j in range(SLAB_ROWS):
            yt_ref[j * MOE_STRIDE:j * MOE_STRIDE + MOE_BS, :] = y[:, j * LANES:(j + 1) * LANES]

    def scatter(base, yt_ref):
        for r0 in range(0, MOE_BS, MOE_RMW_UNROLL):
            upd = []
            for r in range(r0, r0 + MOE_RMW_UNROLL):
                row = pl.multiple_of(row_s[base + r], SLAB_ROWS)
                upd.append((row, acc_ref[pl.ds(row, SLAB_ROWS), :]
                            + gate_s[base + r] * yt_ref[pl.ds(r, SLAB_ROWS, stride=MOE_STRIDE), :]))
            for row, val in upd:
                acc_ref[pl.ds(row, SLAB_ROWS), :] = val

    @pl.when(e == 0)
    def _():
        for c in list_copies():
            c.start()
        acc_ref[...] = jnp.zeros_like(acc_ref)
        yt1[...] = jnp.zeros_like(yt1)
        for c in list_copies():
            c.wait()
        gather(ptab_ref[tab0], xt0)

    seg = i * N_EXPERTS + e
    p0 = pstart_ref[seg]

    def pair(j, carry):
        t = tab0 + p0 + 2 * j
        gather(ptab_ref[t + 1], xt1)
        expert(xt0, yt0)
        scatter(ptab_ref[t - 1], yt1)
        gather(ptab_ref[t + 2], xt0)
        expert(xt1, yt1)
        scatter(ptab_ref[t], yt0)
        return carry

    lax.fori_loop(0, npair_ref[seg], pair, 0)

    @pl.when(e == pl.num_programs(1) - 1)
    def _():
        scatter(ptab_ref[tab0 + p0 + 2 * npair_ref[seg] - 1], yt1)


def _moe_call(h2s, topi, topw, wg, wu, wd, tm):
    n = h2s.shape[0] // SLAB_ROWS
    nt = n // tm
    nseg = nt * N_EXPERTS
    nreal = tm * TOP_K
    nslot = nreal + (N_EXPERTS + 1) * MOE_BS
    null_base = nslot - MOE_BS
    ptab_w = nreal // MOE_BS + 2 * N_EXPERTS + 4

    i32 = jnp.int32
    experts = jnp.arange(N_EXPERTS, dtype=i32)
    counts = jnp.sum((topi.reshape(TOP_K, nt, tm)[..., None] == experts).astype(i32), axis=(0, 2))
    nblk = (counts + MOE_BS - 1) // MOE_BS
    npair = (nblk + 1) // 2
    first = jnp.cumsum(counts + MOE_BS, axis=1) - (counts + MOE_BS)
    pend = jnp.cumsum(2 * npair, axis=1)
    pstart = pend - 2 * npair
    pos = jnp.arange(ptab_w, dtype=i32) - 1
    e_of = jnp.sum((pend[:, None, :] <= pos[None, :, None]).astype(i32), axis=-1)
    e_cl = jnp.minimum(e_of, N_EXPERTS - 1)
    q = pos[None, :] - jnp.take_along_axis(pstart, e_cl, axis=1)
    real_blk = (pos[None, :] >= 0) & (e_of < N_EXPERTS) & (q < jnp.take_along_axis(nblk, e_cl, axis=1))
    ptab = jnp.where(real_blk, jnp.take_along_axis(first, e_cl, axis=1) + q * MOE_BS, null_base).astype(i32)

    slot = jnp.arange(n * TOP_K, dtype=i32)
    key = 2 * ((slot // nreal) * N_EXPERTS + topi.T.reshape(n * TOP_K))
    row = ((slot // TOP_K) % tm) * SLAB_ROWS
    pad_key = jnp.concatenate([jnp.repeat(2 * jnp.arange(nseg, dtype=i32) + 1, MOE_BS),
                               jnp.repeat(2 * (jnp.arange(nt, dtype=i32) * N_EXPERTS + N_EXPERTS - 1) + 1, MOE_BS)])
    npad = pad_key.shape[0]
    _, row_s, gate_s = lax.sort(
        (jnp.concatenate([key, pad_key]),
         jnp.concatenate([row, jnp.full((npad,), tm * SLAB_ROWS, i32)]),
         jnp.concatenate([topw.T.reshape(n * TOP_K), jnp.zeros((npad,), F32)])),
        num_keys=1, is_stable=False)

    def wspec(r, c):
        return pl.BlockSpec((None, r, c), lambda i, e, *_: (e, 0, 0))

    stage = pltpu.VMEM((SLAB_ROWS * MOE_STRIDE, LANES), F32)
    grid_spec = pltpu.PrefetchScalarGridSpec(
        num_scalar_prefetch=3,
        grid=(nt, N_EXPERTS),
        in_specs=[
            pl.BlockSpec((tm * SLAB_ROWS, LANES), lambda i, e, *_: (i, 0), pipeline_mode=pl.Buffered(1)),
            pl.BlockSpec(memory_space=pl.ANY),
            pl.BlockSpec(memory_space=pl.ANY),
            wspec(D_MODEL, D_EXPERT), wspec(D_MODEL, D_EXPERT), wspec(D_EXPERT, D_MODEL),
        ],
        out_specs=pl.BlockSpec((None, (tm + 1) * SLAB_ROWS, LANES), lambda i, e, *_: (i, 0, 0),
                               pipeline_mode=pl.Buffered(1)),
        scratch_shapes=[
            pltpu.SMEM((nslot,), i32),
            pltpu.SMEM((nslot,), F32),
            stage, stage, stage, stage,
            pltpu.SemaphoreType.DMA((2,)),
        ],
    )
    return pl.pallas_call(
        functools.partial(_moe_kernel, tm=tm, ptab_w=ptab_w),
        grid_spec=grid_spec,
        out_shape=jax.ShapeDtypeStruct((nt, (tm + 1) * SLAB_ROWS, LANES), F32),
        compiler_params=_cparams("arbitrary", "arbitrary"),
        name="moe",
    )(ptab.reshape(nt * ptab_w), pstart.reshape(nseg).astype(i32), npair.reshape(nseg).astype(i32),
      h2s, row_s, gate_s, wg, wu, wd)


def _combine_kernel(r_ref, xp_ref, g2_ref, fg_ref, o_ref, *, final):
    tt = xp_ref.shape[0]
    routed = jnp.concatenate([r_ref[pl.ds(j, tt, stride=SLAB_ROWS), :] for j in range(SLAB_ROWS)], axis=1)
    x2 = xp_ref[...] + g2_ref[...] * routed
    if final:
        x2 = _rmsnorm(x2, fg_ref[...])
    o_ref[...] = x2


def _combine_call(routed_s, xp, g2, final_g, tt, final):
    bx, tx, _ = xp.shape
    nt = tx // tt
    per_tok = g2.shape[1] != 1
    g2_spec = (pl.BlockSpec((None, tt, D_MODEL), lambda b, i: (b, i, 0)) if per_tok
               else pl.BlockSpec((None, 1, D_MODEL), lambda b, i: (b, 0, 0)))
    tok_spec = pl.BlockSpec((None, tt, D_MODEL), lambda b, i: (b, i, 0))
    per_tile = (routed_s.shape[1] // SLAB_ROWS - 1) // tt
    routed_spec = pl.BlockSpec((None, tt * SLAB_ROWS, LANES),
                               lambda b, i: ((b * nt + i) // per_tile, (b * nt + i) % per_tile, 0))
    return pl.pallas_call(
        functools.partial(_combine_kernel, final=final),
        grid=(bx, nt),
        in_specs=[routed_spec, tok_spec, g2_spec,
                  pl.BlockSpec((1, D_MODEL), lambda b, i: (0, 0))],
        out_specs=tok_spec,
        out_shape=jax.ShapeDtypeStruct((bx, tx, D_MODEL), F32),
        compiler_params=_cparams("parallel", "parallel"),
        name="combine",
    )(routed_s, xp, g2, final_g)


def _rope_tables(pos):
    half = HEAD_DIM // 2
    inv_freq = ROPE_THETA ** (-jnp.arange(half, dtype=F32) / half)
    ang = pos.astype(F32)[:, None] * inv_freq[None, :]
    cos = jnp.cos(ang)
    sin = jnp.sin(ang)
    return jnp.tile(cos, (1, 4)), jnp.tile(jnp.concatenate([-sin, sin], axis=1), (1, 2))


def _trunk(x, mods, pos, past_k, past_v, ssm_h0, layers, final_g, tt, tm, tok_batches):
    bx, tx, _ = x.shape
    b, t = tok_batches
    cos_t, sin_t = _rope_tables(pos)
    new_k, new_v, new_h = [], [], []
    for l, lw in enumerate(layers):
        sh1, sc1, g1, sh2, sc2, g2 = mods[l]
        u, q, k, v, gs, ga = _inproj_call(x, sh1, sc1, lw['norm_attn_g'], lw['w_in'], cos_t, sin_t, tt)
        ks = k.reshape(b, t, KV_W)
        vs = v.reshape(b, t, KV_W)
        qs = q.reshape(b, t, ATTN_W)
        if past_k is None:
            attn = _attn_call(lw['sink'], qs, ks, ks, vs, vs, True)
            new_k.append(ks[:, -WINDOW:].reshape(b, WINDOW, N_KV_HEADS, HEAD_DIM))
            new_v.append(vs[:, -WINDOW:].reshape(b, WINDOW, N_KV_HEADS, HEAD_DIM))
            h0 = jnp.zeros((b, SSM_GROUPS, SSM_STATE, 2), F32)
        else:
            pk = past_k[l].reshape(b, -1, KV_W)
            pv = past_v[l].reshape(b, -1, KV_W)
            attn = _attn_call(lw['sink'], qs, pk, ks, pv, vs, False)
            new_k.append(ks.reshape(b, t, N_KV_HEADS, HEAD_DIM))
            new_v.append(vs.reshape(b, t, N_KV_HEADS, HEAD_DIM))
            h0 = ssm_h0[l]
        y, h_last = _ssm_branch(u, lw['ssm_mats'], h0, b // bx, t // SSM_L)
        new_h.append(h_last)
        xp, h2s, topi, topw = _post_call(y, u, attn.reshape(bx, tx, ATTN_W), gs, ga, x,
                                         (g1, sh2, sc2, g2), lw, tt)
        routed_s = _moe_call(h2s, topi, topw, lw['ex_wg'], lw['ex_wu'], lw['ex_wd'], tm)
        x = _combine_call(routed_s, xp, g2, final_g, tt, l == len(layers) - 1)
    return x, jnp.stack(new_k), jnp.stack(new_v), jnp.stack(new_h)


def kernel(x_prompt, x_sample, cache_k, cache_v, state_ssm, c_prompt, c_sample, ada_w, ada_b, norm_attn_g,
           norm_ffn_g, w_in, ssm_a_re, ssm_a_im, ssm_log_dt, ssm_b_re, ssm_b_im, ssm_c_re, ssm_c_im, ssm_d,
           ssm_w_glu, ssm_b_glu, attn_sink, w_branch_ssm, w_branch_attn, w_out, router_w, router_bias,
           expert_w_gate, expert_w_up, expert_w_down, shared_w_gate, shared_w_up, shared_w_down, final_g):
    depth = ada_w.shape[0]
    bp, tp, _ = x_prompt.shape
    bs, ts, _ = x_sample.shape

    layers = []
    for l in range(depth):
        layers.append({
            'norm_attn_g': norm_attn_g[l][None], 'norm_ffn_g': norm_ffn_g[l][None],
            'w_in': w_in[l].astype(BF16), 'sink': attn_sink[l],
            'ssm_mats': _ssm_mats(ssm_a_re[l], ssm_a_im[l], ssm_log_dt[l], ssm_b_re[l], ssm_b_im[l],
                                  ssm_c_re[l], ssm_c_im[l]),
            'ssm_d': ssm_d[l][None], 'w_glu': ssm_w_glu[l].astype(BF16), 'b_glu': ssm_b_glu[l][None],
            'w_bs': w_branch_ssm[l].astype(BF16), 'w_ba': w_branch_attn[l].astype(BF16),
            'w_out': w_out[l].astype(BF16), 'router_wt': router_w[l].T, 'router_bias': router_bias[l][:, None],
            'sh_wg': shared_w_gate[l].astype(BF16), 'sh_wu': shared_w_up[l].astype(BF16),
            'sh_wd': shared_w_down[l].astype(BF16),
            'ex_wg': expert_w_gate[l].astype(BF16), 'ex_wu': expert_w_up[l].astype(BF16),
            'ex_wd': expert_w_down[l].astype(BF16),
        })
    fg = final_g[None]

    rows = 16
    c_all = jnp.concatenate([c_prompt, c_sample, jnp.zeros((rows - bp - bs, D_MODEL), F32)], axis=0)
    mod = _mod_call(c_all, ada_w, ada_b).reshape(depth, rows, 6, D_MODEL)
    mods_p = [[mod[l, :bp, j][:, None, :] for j in range(6)] for l in range(depth)]
    mods_s = [[jnp.repeat(mod[l, bp:bp + bs, j], ts, axis=0)[None] for j in range(6)] for l in range(depth)]

    pos_p = jnp.arange(tp, dtype=jnp.int32)
    pos_s = jnp.tile(PAST_LEN + jnp.arange(ts, dtype=jnp.int32), bs)

    y_p, k_p, v_p, h_p = _trunk(x_prompt, mods_p, pos_p, None, None, None, layers, fg,
                                tt=512, tm=min(4096, bp * tp), tok_batches=(bp, tp))
    y_s, k_s, v_s, h_s = _trunk(x_sample.reshape(1, bs * ts, D_MODEL), mods_s, pos_s, cache_k, cache_v,
                                state_ssm, layers, fg, tt=bs * ts, tm=bs * ts, tok_batches=(bs, ts))
    return (y_p, y_s.reshape(bs, ts, D_MODEL), k_p, v_p, h_p, k_s, v_s, h_s)
```

```python
import functools
import math

import jax
import jax.numpy as jnp
from jax import lax
from jax.experimental import pallas as pl
from jax.experimental.pallas import tpu as pltpu

F32 = jnp.float32
BF16 = jnp.bfloat16

D_MODEL = 1024
DEPTH = 2
PAST_LEN = 4096
CHUNK = 64
N_HEADS = 8
N_KV_HEADS = 2
HEAD_DIM = 64
WINDOW = 128
ROPE_THETA = 10000.0
D_SSM = 512
SSM_GROUP_CH = 16
SSM_GROUPS = D_SSM // SSM_GROUP_CH
SSM_STATE = 64
N_EXPERTS = 64
N_EXPERT_GROUPS = 8
EXPERTS_PER_GROUP = N_EXPERTS // N_EXPERT_GROUPS
TOPK_GROUPS = 4
TOP_K = 8
D_EXPERT = 256
D_SHARED = 256
ROUTED_SCALE = 2.5
RMS_EPS = 1e-6
NEG_INF = -1e30
ATTN_W = N_HEADS * HEAD_DIM
KV_W = N_KV_HEADS * HEAD_DIM
IN_COLS = D_SSM + ATTN_W + 2 * KV_W + 2 * D_MODEL
IN_SPLITS = (0, D_SSM, D_SSM + ATTN_W, D_SSM + ATTN_W + KV_W, D_SSM + ATTN_W + 2 * KV_W,
             D_SSM + ATTN_W + 2 * KV_W + D_MODEL, IN_COLS)

LANES = 128
SSM_L = 16
SSM_OCT_G = LANES // SSM_GROUP_CH
SSM_OCTS = SSM_GROUPS // SSM_OCT_G
SSM_FLAT = SSM_L * LANES
SSM_SW = SSM_OCT_G * SSM_STATE
SSM_POW_ROWS = 16
VMEM_LIMIT = 56 * 1024 * 1024
SUBLANES = 8
SLAB_ROWS = D_MODEL // LANES
MOE_BS = 128
MOE_STRIDE = MOE_BS + SUBLANES
MOE_RMW_UNROLL = 4
MOE_WSLOTS = 4


def _cparams(*sem):
    return pltpu.CompilerParams(dimension_semantics=sem, vmem_limit_bytes=VMEM_LIMIT)


def _dot(a, b):
    return jnp.dot(a, b, preferred_element_type=F32)


def _dot_nt(a, b):
    return lax.dot_general(a, b, (((1,), (1,)), ((), ())), preferred_element_type=F32)


def _split(x):
    hi = x.astype(BF16)
    lo = (x - hi.astype(F32)).astype(BF16)
    return hi, lo


def _rmsnorm(x, g):
    return x * lax.rsqrt(jnp.mean(x * x, axis=-1, keepdims=True) + RMS_EPS) * g


def _mod_kernel(c_ref, w_ref, b_ref, o_ref):
    cond = jax.nn.silu(c_ref[...])
    ch, cl = _split(cond)
    wh, wl = _split(w_ref[...])
    o_ref[...] = _dot(ch, wh) + (_dot(ch, wl) + _dot(cl, wh)) + b_ref[...]


def _mod_call(c_all, ada_w, ada_b):
    depth = ada_w.shape[0]
    rows = c_all.shape[0]
    nj = 6
    return pl.pallas_call(
        _mod_kernel,
        grid=(depth, nj),
        in_specs=[
            pl.BlockSpec((rows, D_MODEL), lambda l, j: (0, 0)),
            pl.BlockSpec((None, D_MODEL, D_MODEL), lambda l, j: (l, 0, j)),
            pl.BlockSpec((None, 1, D_MODEL), lambda l, j: (l, 0, j)),
        ],
        out_specs=pl.BlockSpec((None, rows, D_MODEL), lambda l, j: (l, 0, j)),
        out_shape=jax.ShapeDtypeStruct((depth, rows, nj * D_MODEL), F32),
        compiler_params=_cparams("parallel", "parallel"),
        name="mod",
    )(c_all, ada_w, ada_b.reshape(depth, 1, nj * D_MODEL))


def _rope2(t, cos, sin_signed, first_half):
    swapped = jnp.where(first_half, pltpu.roll(t, LANES - HEAD_DIM // 2, axis=1),
                        pltpu.roll(t, HEAD_DIM // 2, axis=1))
    return t * cos + swapped * sin_signed


def _inproj_kernel(x_ref, sh_ref, sc_ref, g_ref, w_ref, cos_ref, sin_ref,
                   u_ref, q_ref, k_ref, v_ref, gs_ref, ga_ref):
    h = _rmsnorm(x_ref[...], g_ref[...]) * (1.0 + sc_ref[...]) + sh_ref[...]
    hb = h.astype(BF16)

    def proj(i):
        return _dot(hb, w_ref[:, IN_SPLITS[i]:IN_SPLITS[i + 1]])

    u_ref[...] = proj(0)
    cos = cos_ref[...]
    sin = sin_ref[...]
    lane = lax.broadcasted_iota(jnp.int32, cos.shape, 1)
    first_half = (lane % HEAD_DIM) < (HEAD_DIM // 2)
    q = proj(1)
    for j in range(ATTN_W // LANES):
        sl = slice(j * LANES, (j + 1) * LANES)
        q_ref[:, sl] = (_rope2(q[:, sl], cos, sin, first_half) * (HEAD_DIM ** -0.5)).astype(BF16)
    k_ref[...] = _rope2(proj(2), cos, sin, first_half)
    v_ref[...] = proj(3)
    gs_ref[...] = proj(4)
    ga_ref[...] = proj(5)


def _inproj_call(x, sh1, sc1, norm_g, w_in_bf, cos_t, sin_t, tt):
    bx, tx, _ = x.shape
    nt = tx // tt
    per_tok = sh1.shape[1] != 1
    mod_spec = (pl.BlockSpec((None, tt, D_MODEL), lambda b, i: (b, i, 0)) if per_tok
                else pl.BlockSpec((None, 1, D_MODEL), lambda b, i: (b, 0, 0)))

    def tok_spec(w):
        return pl.BlockSpec((None, tt, w), lambda b, i: (b, i, 0))

    def tok_shape(w, dt):
        return jax.ShapeDtypeStruct((bx, tx, w), dt)

    return pl.pallas_call(
        _inproj_kernel,
        grid=(bx, nt),
        in_specs=[
            tok_spec(D_MODEL), mod_spec, mod_spec,
            pl.BlockSpec((1, D_MODEL), lambda b, i: (0, 0)),
            pl.BlockSpec((D_MODEL, IN_COLS), lambda b, i: (0, 0)),
            pl.BlockSpec((tt, LANES), lambda b, i: (i, 0)),
            pl.BlockSpec((tt, LANES), lambda b, i: (i, 0)),
        ],
        out_specs=[tok_spec(D_SSM), tok_spec(ATTN_W), tok_spec(KV_W), tok_spec(KV_W),
                   tok_spec(D_MODEL), tok_spec(D_MODEL)],
        out_shape=[tok_shape(D_SSM, F32), tok_shape(ATTN_W, BF16), tok_shape(KV_W, F32),
                   tok_shape(KV_W, F32), tok_shape(D_MODEL, F32), tok_shape(D_MODEL, F32)],
        compiler_params=_cparams("parallel", "parallel"),
        name="inproj",
    )(x, sh1, sc1, norm_g, w_in_bf, cos_t, sin_t)


def _ssm_kernel(u_ref, t_ref, bl_ref, cl_ref, ap_ref, h0_ref, y_ref, s_ref, *, nb, cb):
    nc = nb * cb
    sw = SSM_SW
    uf = jnp.concatenate([u_ref[pl.ds(s, nc, stride=SSM_L), :].astype(BF16) for s in range(SSM_L)], axis=1)
    y = _dot(uf, t_ref[...])
    v = _dot(uf, bl_ref[...])
    xre = v[:, :sw]
    xim = v[:, sw:]

    row = lax.broadcasted_iota(jnp.int32, (nc, sw), 0)
    cidx = row & (cb - 1)
    bidx = row // cb
    h0 = h0_ref[...]
    h0re = jnp.zeros((nc, sw), F32)
    h0im = jnp.zeros((nc, sw), F32)
    for b in range(nb):
        h0re = jnp.where(bidx == b, h0[b:b + 1, :sw], h0re)
        h0im = jnp.where(bidx == b, h0[b:b + 1, sw:], h0im)
    first = cidx == 0
    are = ap_ref[0:1, :sw]
    aim = ap_ref[0:1, sw:]
    xre = xre + jnp.where(first, are * h0re - aim * h0im, 0.0)
    xim = xim + jnp.where(first, are * h0im + aim * h0re, 0.0)

    d = 1
    k = 0
    while d < cb:
        are = ap_ref[k:k + 1, :sw]
        aim = ap_ref[k:k + 1, sw:]
        keep = cidx >= d
        sre = jnp.where(keep, pltpu.roll(xre, d, axis=0), 0.0)
        sim = jnp.where(keep, pltpu.roll(xim, d, axis=0), 0.0)
        xre, xim = xre + (are * sre - aim * sim), xim + (are * sim + aim * sre)
        d *= 2
        k += 1

    for b in range(nb):
        r = b * cb + cb - 1
        s_ref[b:b + 1, :sw] = xre[r:r + 1, :]
        s_ref[b:b + 1, sw:] = xim[r:r + 1, :]

    pre = jnp.where(first, h0re, pltpu.roll(xre, 1, axis=0))
    pim = jnp.where(first, h0im, pltpu.roll(xim, 1, axis=0))
    y = y + _dot(pre.astype(BF16), cl_ref[:sw, :]) + _dot(pim.astype(BF16), cl_ref[sw:, :])
    for t in range(SSM_L):
        y_ref[pl.ds(t, nc, stride=SSM_L), :] = y[:, t * LANES:(t + 1) * LANES]


def _ssm_call(u, mats, h0o, nb, cb):
    tm, bl, cl, ap = mats
    bx, tx, _ = u.shape

    def const_spec(r, c):
        return pl.BlockSpec((None, r, c), lambda o, b: (o, 0, 0), pipeline_mode=pl.Buffered(1))

    tok_spec = pl.BlockSpec((None, tx, LANES), lambda o, b: (b, 0, o))
    st_spec = pl.BlockSpec((None, None, nb, 2 * SSM_SW), lambda o, b: (o, b, 0, 0))
    return pl.pallas_call(
        functools.partial(_ssm_kernel, nb=nb, cb=cb),
        grid=(SSM_OCTS, bx),
        in_specs=[tok_spec, const_spec(SSM_FLAT, SSM_FLAT), const_spec(SSM_FLAT, 2 * SSM_SW),
                  const_spec(2 * SSM_SW, SSM_FLAT), const_spec(SSM_POW_ROWS, 2 * SSM_SW), st_spec],
        out_specs=[tok_spec, st_spec],
        out_shape=[jax.ShapeDtypeStruct((bx, tx, D_SSM), F32),
                   jax.ShapeDtypeStruct((SSM_OCTS, bx, nb, 2 * SSM_SW), F32)],
        compiler_params=_cparams("arbitrary", "arbitrary"),
        name="ssm",
    )(u, tm, bl, cl, ap, h0o)


def _oct_lanes(a):
    lead = a.shape[:-2]
    a = a.reshape(*lead, SSM_OCTS, SSM_SW)
    return jnp.moveaxis(a, -2, 0)


def _ssm_mats(a_re, a_im, log_dt, b_re, b_im, c_re, c_im):
    hp = lax.Precision.HIGHEST
    dt = jnp.exp(log_dt)[:, None]
    lre = a_re * dt
    lim = a_im * dt
    mag = jnp.exp(lre)
    ab_re = mag * jnp.cos(lim)
    ab_im = mag * jnp.sin(lim)
    den = a_re * a_re + a_im * a_im
    n_re = ab_re - 1.0
    f_re = (n_re * a_re + ab_im * a_im) / den
    f_im = (ab_im * a_re - n_re * a_im) / den
    bb_re = f_re[..., None] * b_re - f_im[..., None] * b_im
    bb_im = f_re[..., None] * b_im + f_im[..., None] * b_re

    def lam_pow(tau):
        tau = tau[:, None, None]
        m = jnp.exp(tau * lre)
        return m * jnp.cos(tau * lim), m * jnp.sin(tau * lim)

    pw_re, pw_im = lam_pow(jnp.arange(SSM_L + 1, dtype=F32))
    cp_re = c_re[None] * pw_re[:, :, None, :] - c_im[None] * pw_im[:, :, None, :]
    cp_im = c_re[None] * pw_im[:, :, None, :] + c_im[None] * pw_re[:, :, None, :]
    kern = (jnp.einsum('tgcn,gnd->tgcd', cp_re, bb_re, precision=hp)
            - jnp.einsum('tgcn,gnd->tgcd', cp_im, bb_im, precision=hp))
    eye = jnp.eye(SSM_OCT_G, dtype=F32)
    og = (SSM_OCTS, SSM_OCT_G)

    def slab_diag(m):
        lead = m.shape[:-3]
        a, b = m.shape[-2:]
        m = m.reshape(*lead, *og, a, b)
        m = m[..., :, None, :] * eye[:, None, :, None]
        return m.reshape(*lead, SSM_OCTS, SSM_OCT_G * a, SSM_OCT_G * b)

    kt = slab_diag(kern.transpose(0, 1, 3, 2)).astype(BF16)
    s_i = jnp.arange(SSM_L)[:, None]
    t_i = jnp.arange(SSM_L)[None, :]
    tm = kt[jnp.clip(t_i - s_i, 0, SSM_L)]
    tm = jnp.where((t_i >= s_i)[:, :, None, None, None], tm, jnp.zeros((), BF16))
    tm = tm.transpose(2, 0, 3, 1, 4).reshape(SSM_OCTS, SSM_FLAT, SSM_FLAT)

    bd_re = slab_diag(bb_re.transpose(0, 2, 1))[:, None]
    bd_im = slab_diag(bb_im.transpose(0, 2, 1))[:, None]
    rev = SSM_L - 1 - jnp.arange(SSM_L)
    rv_re = _oct_lanes(pw_re[rev])[:, :, None, :]
    rv_im = _oct_lanes(pw_im[rev])[:, :, None, :]
    bl = jnp.concatenate([(bd_re * rv_re - bd_im * rv_im).astype(BF16),
                          (bd_re * rv_im + bd_im * rv_re).astype(BF16)], axis=-1)
    bl = bl.reshape(SSM_OCTS, SSM_FLAT, 2 * SSM_SW)

    cd_re = slab_diag(c_re.transpose(0, 2, 1))[:, :, None, :]
    cd_im = slab_diag(c_im.transpose(0, 2, 1))[:, :, None, :]
    fw_re = _oct_lanes(pw_re[1:]).transpose(0, 2, 1)[..., None]
    fw_im = _oct_lanes(pw_im[1:]).transpose(0, 2, 1)[..., None]
    cl = jnp.concatenate([(cd_re * fw_re - cd_im * fw_im).astype(BF16),
                          (-(cd_re * fw_im + cd_im * fw_re)).astype(BF16)], axis=1)
    cl = cl.reshape(SSM_OCTS, 2 * SSM_SW, SSM_FLAT)

    steps = float(SSM_L) * (2.0 ** jnp.arange(SSM_POW_ROWS, dtype=F32))
    ap_re, ap_im = lam_pow(steps)
    ap = jnp.concatenate([_oct_lanes(ap_re), _oct_lanes(ap_im)], axis=-1)
    return tm, bl, cl, ap


def _ssm_branch(u, mats, h0, nb, cb):
    bx = u.shape[0]
    h0o = jnp.concatenate([_oct_lanes(h0[..., 0]), _oct_lanes(h0[..., 1])], axis=-1)
    y, s = _ssm_call(u, mats, h0o.reshape(SSM_OCTS, bx, nb, 2 * SSM_SW), nb, cb)
    s = s.reshape(SSM_OCTS, bx * nb, 2, SSM_OCT_G, SSM_STATE).transpose(1, 0, 3, 4, 2)
    return y, s.reshape(bx * nb, SSM_GROUPS, SSM_STATE, 2)


def _attn_kernel(sink_ref, q_ref, ka_ref, kb_ref, va_ref, vb_ref, o_ref, *, banded):
    rows = q_ref.shape[0]
    nk = 2 * LANES
    if banded:
        kfull = jnp.concatenate([ka_ref[...], kb_ref[...]], axis=0)
        vfull = jnp.concatenate([va_ref[...], vb_ref[...]], axis=0)
    else:
        pad = jnp.zeros((nk - ka_ref.shape[0] - kb_ref.shape[0], KV_W), F32)
        kfull = jnp.concatenate([ka_ref[...], kb_ref[...], pad], axis=0)
        vfull = jnp.concatenate([va_ref[...], vb_ref[...], pad], axis=0)

    r_i = lax.broadcasted_iota(jnp.int32, (rows, nk), 0)
    c_i = lax.broadcasted_iota(jnp.int32, (rows, nk), 1)
    if banded:
        lo = (r_i // CHUNK) * CHUNK
        first_frame = jnp.where(pl.program_id(1) > 0, 0, WINDOW)
        valid = (c_i >= lo) & (c_i < lo + WINDOW + CHUNK) & (c_i >= first_frame)
    else:
        valid = c_i < (ka_ref.shape[0] + kb_ref.shape[0])

    lane = lax.broadcasted_iota(jnp.int32, (nk, KV_W), 1)
    low = lane < HEAD_DIM
    kroll = pltpu.roll(kfull, HEAD_DIM, axis=1)
    vroll = pltpu.roll(vfull, HEAD_DIM, axis=1)

    for g in range(N_KV_HEADS):
        k_lo, k_hi = (kfull, kroll) if g == 0 else (kroll, kfull)
        v_lo, v_hi = (vfull, vroll) if g == 0 else (vroll, vfull)
        kpad = (jnp.where(low, k_lo, 0.0).astype(BF16), jnp.where(low, 0.0, k_hi).astype(BF16))
        vpad = (jnp.where(low, v_lo, 0.0).astype(BF16), jnp.where(low, 0.0, v_hi).astype(BF16))
        for pp in range(2):
            slab = 2 * g + pp
            qp = q_ref[:, slab * LANES:(slab + 1) * LANES]
            acc = None
            for hh in range(2):
                sk = sink_ref[2 * slab + hh]
                s = jnp.where(valid, _dot_nt(qp, kpad[hh]), NEG_INF)
                m = jnp.maximum(jnp.max(s, axis=-1, keepdims=True), sk)
                p = jnp.exp(s - m)
                den = jnp.sum(p, axis=-1, keepdims=True) + jnp.exp(sk - m)
                o = _dot(p.astype(BF16), vpad[hh]) / den
                acc = o if acc is None else acc + o
            o_ref[:, slab * LANES:(slab + 1) * LANES] = acc.astype(BF16)


def _attn_call(sink, q, ka, kb, va, vb, banded):
    bx, tx, _ = q.shape
    if banded:
        rows = 2 * CHUNK
        nt = tx // rows
        grid = (bx, nt)
        q_spec = pl.BlockSpec((None, rows, ATTN_W), lambda b, i: (b, i, 0))
        prev = pl.BlockSpec((None, WINDOW, KV_W), lambda b, i: (b, jnp.maximum(i - 1, 0), 0))
        cur = pl.BlockSpec((None, rows, KV_W), lambda b, i: (b, i, 0))
    else:
        rows = tx
        grid = (bx, 1)
        q_spec = pl.BlockSpec((None, rows, ATTN_W), lambda b, i: (b, 0, 0))
        prev = pl.BlockSpec((None, ka.shape[1], KV_W), lambda b, i: (b, 0, 0))
        cur = pl.BlockSpec((None, rows, KV_W), lambda b, i: (b, 0, 0))
    return pl.pallas_call(
        functools.partial(_attn_kernel, banded=banded),
        grid=grid,
        in_specs=[pl.BlockSpec(memory_space=pltpu.SMEM), q_spec, prev, cur, prev, cur],
        out_specs=q_spec,
        out_shape=jax.ShapeDtypeStruct((bx, tx, ATTN_W), BF16),
        compiler_params=_cparams("parallel", "parallel"),
        name="attn",
    )(sink, q, ka, kb, va, vb)


def _route(logits_t, bias_col, topi_ref, topw_ref):
    tt = logits_t.shape[1]
    scores = jax.nn.sigmoid(logits_t)
    biased = scores + bias_col
    sub = lax.broadcasted_iota(jnp.int32, (EXPERTS_PER_GROUP, tt), 0).astype(F32)
    ninf = float('-inf')
    blocks = [biased[EXPERTS_PER_GROUP * g:EXPERTS_PER_GROUP * (g + 1), :] for g in range(N_EXPERT_GROUPS)]

    gscore = jnp.zeros((N_EXPERT_GROUPS, tt), F32)
    for g in range(N_EXPERT_GROUPS):
        blk = blocks[g]
        m1 = jnp.max(blk, axis=0, keepdims=True)
        i1 = jnp.min(jnp.where(blk == m1, sub, float(EXPERTS_PER_GROUP)), axis=0, keepdims=True)
        m2 = jnp.max(jnp.where(sub == i1, ninf, blk), axis=0, keepdims=True)
        gscore = jnp.where(sub == float(g), jnp.broadcast_to(m1 + m2, gscore.shape), gscore)

    grank = jnp.zeros((N_EXPERT_GROUPS, tt), F32)
    for j in range(N_EXPERT_GROUPS):
        rj = jnp.broadcast_to(gscore[j:j + 1, :], gscore.shape)
        beats = (rj > gscore) | ((rj == gscore) & (sub > float(j)))
        grank = grank + jnp.where(beats, 1.0, 0.0)
    gsel = jnp.where(grank < float(TOPK_GROUPS), 1.0, 0.0)

    masked = []
    for g in range(N_EXPERT_GROUPS):
        on = jnp.broadcast_to(gsel[g:g + 1, :], blocks[g].shape) > 0.5
        masked.append(jnp.where(on, blocks[g], NEG_INF))

    ranks = [jnp.zeros((EXPERTS_PER_GROUP, tt), F32) for _ in range(N_EXPERT_GROUPS)]
    for jb in range(N_EXPERT_GROUPS):
        for jj in range(EXPERTS_PER_GROUP):
            rj = jnp.broadcast_to(masked[jb][jj:jj + 1, :], (EXPERTS_PER_GROUP, tt))
            for ib in range(N_EXPERT_GROUPS):
                mi = masked[ib]
                if ib < jb:
                    beats = rj > mi
                elif ib > jb:
                    beats = rj >= mi
                else:
                    beats = (rj > mi) | ((rj == mi) & (sub > float(jj)))
                ranks[ib] = ranks[ib] + jnp.where(beats, 1.0, 0.0)

    topi = jnp.zeros((TOP_K, tt), F32)
    topw = jnp.zeros((TOP_K, tt), F32)
    for r in range(TOP_K):
        ai = jnp.zeros((EXPERTS_PER_GROUP, tt), F32)
        aw = jnp.zeros((EXPERTS_PER_GROUP, tt), F32)
        for g in range(N_EXPERT_GROUPS):
            hit = ranks[g] == float(r)
            ai = ai + jnp.where(hit, sub + float(EXPERTS_PER_GROUP * g), 0.0)
            aw = aw + jnp.where(hit, scores[EXPERTS_PER_GROUP * g:EXPERTS_PER_GROUP * (g + 1), :], 0.0)
        ir = jnp.sum(ai, axis=0, keepdims=True)
        wr = jnp.sum(aw, axis=0, keepdims=True)
        topi = jnp.where(sub == float(r), jnp.broadcast_to(ir, topi.shape), topi)
        topw = jnp.where(sub == float(r), jnp.broadcast_to(wr, topw.shape), topw)
    den = jnp.sum(topw, axis=0, keepdims=True)
    topi_ref[...] = topi.astype(jnp.int32)
    topw_ref[...] = topw / den * ROUTED_SCALE


def _post_kernel(y_ref, u_ref, a_ref, gs_ref, ga_ref, x_ref, g1_ref, sh2_ref, sc2_ref, g2_ref,
                 d_ref, wglu_ref, bglu_ref, wbs_ref, wba_ref, wout_ref, nf_ref, rwt_ref, rb_ref,
                 swg_ref, swu_ref, swd_ref, xp_ref, h2s_ref, topi_ref, topw_ref):
    ys = y_ref[...] + d_ref[...] * u_ref[...]
    z = jax.nn.gelu(ys)
    y2 = z * jax.nn.sigmoid(_dot(z.astype(BF16), wglu_ref[...]) + bglu_ref[...])
    merged = (jax.nn.sigmoid(gs_ref[...]) * _dot(y2.astype(BF16), wbs_ref[...])
              + jax.nn.sigmoid(ga_ref[...]) * _dot(a_ref[...], wba_ref[...]))
    x1 = x_ref[...] + g1_ref[...] * _dot(merged.astype(BF16), wout_ref[...])

    h2 = _rmsnorm(x1, nf_ref[...]) * (1.0 + sc2_ref[...]) + sh2_ref[...]
    tt = h2.shape[0]
    for j in range(SLAB_ROWS):
        h2s_ref[pl.ds(j, tt, stride=SLAB_ROWS), :] = h2[:, j * LANES:(j + 1) * LANES]
    hh, hl = _split(h2)
    rh, rl = _split(rwt_ref[...])
    logits_t = _dot_nt(rh, hh) + (_dot_nt(rh, hl) + _dot_nt(rl, hh))
    _route(logits_t, rb_ref[...], topi_ref, topw_ref)

    shared = _dot((jax.nn.silu(_dot(hh, swg_ref[...])) * _dot(hh, swu_ref[...])).astype(BF16), swd_ref[...])
    xp_ref[...] = x1 + g2_ref[...] * shared


def _post_call(y, u, attn, gs, ga, x, mods, lw, tt):
    bx, tx, _ = x.shape
    nt = tx // tt
    per_tok = mods[0].shape[1] != 1
    mod_spec = (pl.BlockSpec((None, tt, D_MODEL), lambda b, i: (b, i, 0)) if per_tok
                else pl.BlockSpec((None, 1, D_MODEL), lambda b, i: (b, 0, 0)))

    def tok_spec(w):
        return pl.BlockSpec((None, tt, w), lambda b, i: (b, i, 0))

    def full(a):
        return pl.BlockSpec(a.shape, lambda b, i: (0,) * a.ndim)

    weights = [lw['ssm_d'], lw['w_glu'], lw['b_glu'], lw['w_bs'], lw['w_ba'], lw['w_out'], lw['norm_ffn_g'],
               lw['router_wt'], lw['router_bias'], lw['sh_wg'], lw['sh_wu'], lw['sh_wd']]
    return pl.pallas_call(
        _post_kernel,
        grid=(bx, nt),
        in_specs=[tok_spec(D_SSM), tok_spec(D_SSM), tok_spec(ATTN_W), tok_spec(D_MODEL), tok_spec(D_MODEL),
                  tok_spec(D_MODEL), mod_spec, mod_spec, mod_spec, mod_spec] + [full(w) for w in weights],
        out_specs=[tok_spec(D_MODEL),
                   pl.BlockSpec((tt * SLAB_ROWS, LANES), lambda b, i: (b * nt + i, 0)),
                   pl.BlockSpec((TOP_K, tt), lambda b, i: (0, b * nt + i)),
                   pl.BlockSpec((TOP_K, tt), lambda b, i: (0, b * nt + i))],
        out_shape=[jax.ShapeDtypeStruct((bx, tx, D_MODEL), F32),
                   jax.ShapeDtypeStruct((bx * tx * SLAB_ROWS, LANES), F32),
                   jax.ShapeDtypeStruct((TOP_K, bx * tx), jnp.int32),
                   jax.ShapeDtypeStruct((TOP_K, bx * tx), F32)],
        compiler_params=_cparams("parallel", "parallel"),
        name="post",
    )(y, u, attn, gs, ga, x, *mods, *weights)


def _moe_kernel(ptab_ref, ktab_ref, elist_ref, meta_ref, hs_ref, row_hbm, gate_hbm, wg_hbm, wu_hbm, wd_hbm, o_ref,
                row_s, gate_s, xt0, xt1, yt0, yt1, wg_buf, wu_buf, wd_buf, lsem, wsem, *, tm, ptab_w):
    i = pl.program_id(0)
    nslot = row_s.shape[0]
    acc_ref = o_ref
    row_mask = tm * SLAB_ROWS - 1
    tab0 = i * ptab_w + 1
    npairs = meta_ref[2 * i]
    nrank = meta_ref[2 * i + 1]

    def list_copies():
        src = pl.ds(pl.multiple_of(i * nslot, LANES), nslot)
        return (pltpu.make_async_copy(row_hbm.at[src], row_s, lsem.at[0]),
                pltpu.make_async_copy(gate_hbm.at[src], gate_s, lsem.at[1]))

    def weight_copies(k):
        ex = elist_ref[i * N_EXPERTS + k]
        slot = k % MOE_WSLOTS
        return (pltpu.make_async_copy(wg_hbm.at[ex], wg_buf.at[slot], wsem.at[slot, 0]),
                pltpu.make_async_copy(wu_hbm.at[ex], wu_buf.at[slot], wsem.at[slot, 1]),
                pltpu.make_async_copy(wd_hbm.at[ex], wd_buf.at[slot], wsem.at[slot, 2]))

    def gather(base, xt_ref):
        for r in range(MOE_BS):
            row = pl.multiple_of(row_s[base + r] & row_mask, SLAB_ROWS)
            xt_ref[pl.ds(r, SLAB_ROWS, stride=MOE_STRIDE), :] = hs_ref[pl.ds(row, SLAB_ROWS), :]

    def expert(xt_ref, yt_ref, k):
        slot = k % MOE_WSLOTS
        x = jnp.concatenate([xt_ref[j * MOE_STRIDE:j * MOE_STRIDE + MOE_BS, :] for j in range(SLAB_ROWS)],
                            axis=1).astype(BF16)
        act = jax.nn.silu(_dot(x, wg_buf[slot])) * _dot(x, wu_buf[slot])
        y = _dot(act.astype(BF16), wd_buf[slot])
        for j in range(SLAB_ROWS):
            yt_ref[j * MOE_STRIDE:j * MOE_STRIDE + MOE_BS, :] = y[:, j * LANES:(j + 1) * LANES]

    def scatter(base, yt_ref):
        for r0 in range(0, MOE_BS, MOE_RMW_UNROLL):
            upd = []
            for r in range(r0, r0 + MOE_RMW_UNROLL):
                row = pl.multiple_of(row_s[base + r], SLAB_ROWS)
                upd.append((row, acc_ref[pl.ds(row, SLAB_ROWS), :]
                            + gate_s[base + r] * yt_ref[pl.ds(r, SLAB_ROWS, stride=MOE_STRIDE), :]))
            for row, val in upd:
                acc_ref[pl.ds(row, SLAB_ROWS), :] = val

    for c in list_copies():
        c.start()
    ahead = MOE_WSLOTS - 2
    for r in range(ahead + 1):
        @pl.when(r < nrank)
        def _():
            for c in weight_copies(r):
                c.start()

    acc_ref[...] = jnp.zeros_like(acc_ref)
    yt1[...] = jnp.zeros_like(yt1)
    for c in list_copies():
        c.wait()
    gather(ptab_ref[tab0], xt0)

    def pair(j, carry):
        waited, started = carry
        t = tab0 + 2 * j
        k0 = ktab_ref[t]
        k1 = ktab_ref[t + 1]

        target = jnp.minimum(k1 + ahead, nrank - 1)
        for d in (1, 2):
            @pl.when(started + d <= target)
            def _():
                for c in weight_copies(started + d):
                    c.start()

        for d in (1, 2):
            @pl.when(waited + d <= k1)
            def _():
                for c in weight_copies(waited + d):
                    c.wait()

        gather(ptab_ref[t + 1], xt1)
        expert(xt0, yt0, k0)
        scatter(ptab_ref[t - 1], yt1)
        gather(ptab_ref[t + 2], xt0)
        expert(xt1, yt1, k1)
        scatter(ptab_ref[t], yt0)
        return jnp.maximum(waited, k1), jnp.maximum(started, target)

    lax.fori_loop(0, npairs, pair, (jnp.int32(-1), jnp.minimum(jnp.int32(ahead), nrank - 1)))
    scatter(ptab_ref[tab0 + 2 * npairs - 1], yt1)


def _moe_call(h2s, topi, topw, wg, wu, wd, tm):
    n = h2s.shape[0] // SLAB_ROWS
    nt = n // tm
    nseg = nt * N_EXPERTS
    nreal = tm * TOP_K
    nslot = nreal + (N_EXPERTS + 1) * MOE_BS
    null_base = nslot - MOE_BS
    ptab_w = nreal // MOE_BS + N_EXPERTS + 4

    i32 = jnp.int32
    experts = jnp.arange(N_EXPERTS, dtype=i32)
    counts = jnp.sum((topi.reshape(TOP_K, nt, tm)[..., None] == experts).astype(i32), axis=(0, 2))
    nblk = (counts + MOE_BS - 1) // MOE_BS
    first = jnp.cumsum(counts + MOE_BS, axis=1) - (counts + MOE_BS)
    pend = jnp.cumsum(nblk, axis=1)
    pstart = pend - nblk
    nonempty = (nblk > 0).astype(i32)
    krank = jnp.cumsum(nonempty, axis=1) - 1
    nrank = jnp.sum(nonempty, axis=1)
    elist = jnp.sum(jnp.where((nonempty[:, None, :] == 1) & (krank[:, None, :] == experts[None, :, None]),
                              experts[None, None, :], 0), axis=-1).astype(i32)
    pos = jnp.arange(ptab_w, dtype=i32) - 1
    e_of = jnp.sum((pend[:, None, :] <= pos[None, :, None]).astype(i32), axis=-1)
    e_cl = jnp.minimum(e_of, N_EXPERTS - 1)
    q = pos[None, :] - jnp.take_along_axis(pstart, e_cl, axis=1)
    real_blk = (pos[None, :] >= 0) & (e_of < N_EXPERTS)
    ptab = jnp.where(real_blk, jnp.take_along_axis(first, e_cl, axis=1) + q * MOE_BS, null_base).astype(i32)
    ktab = jnp.where(real_blk, jnp.take_along_axis(krank, e_cl, axis=1), nrank[:, None] - 1).astype(i32)
    meta = jnp.stack([(pend[:, -1] + 1) // 2, nrank], axis=1).astype(i32)

    slot = jnp.arange(n * TOP_K, dtype=i32)
    key = 2 * ((slot // nreal) * N_EXPERTS + topi.T.reshape(n * TOP_K))
    row = ((slot // TOP_K) % tm) * SLAB_ROWS
    pad_key = jnp.concatenate([jnp.repeat(2 * jnp.arange(nseg, dtype=i32) + 1, MOE_BS),
                               jnp.repeat(2 * (jnp.arange(nt, dtype=i32) * N_EXPERTS + N_EXPERTS - 1) + 1, MOE_BS)])
    npad = pad_key.shape[0]
    _, row_s, gate_s = lax.sort(
        (jnp.concatenate([key, pad_key]),
         jnp.concatenate([row, jnp.full((npad,), tm * SLAB_ROWS, i32)]),
         jnp.concatenate([topw.T.reshape(n * TOP_K), jnp.zeros((npad,), F32)])),
        num_keys=1, is_stable=False)

    stage = pltpu.VMEM((SLAB_ROWS * MOE_STRIDE, LANES), F32)
    hbm = pl.BlockSpec(memory_space=pl.ANY)
    grid_spec = pltpu.PrefetchScalarGridSpec(
        num_scalar_prefetch=4,
        grid=(nt,),
        in_specs=[
            pl.BlockSpec((tm * SLAB_ROWS, LANES), lambda i, *_: (i, 0), pipeline_mode=pl.Buffered(1)),
            hbm, hbm, hbm, hbm, hbm,
        ],
        out_specs=pl.BlockSpec((None, (tm + 1) * SLAB_ROWS, LANES), lambda i, *_: (i, 0, 0),
                               pipeline_mode=pl.Buffered(1)),
        scratch_shapes=[
            pltpu.SMEM((nslot,), i32),
            pltpu.SMEM((nslot,), F32),
            stage, stage, stage, stage,
            pltpu.VMEM((MOE_WSLOTS, D_MODEL, D_EXPERT), BF16),
            pltpu.VMEM((MOE_WSLOTS, D_MODEL, D_EXPERT), BF16),
            pltpu.VMEM((MOE_WSLOTS, D_EXPERT, D_MODEL), BF16),
            pltpu.SemaphoreType.DMA((2,)),
            pltpu.SemaphoreType.DMA((MOE_WSLOTS, 3)),
        ],
    )
    return pl.pallas_call(
        functools.partial(_moe_kernel, tm=tm, ptab_w=ptab_w),
        grid_spec=grid_spec,
        out_shape=jax.ShapeDtypeStruct((nt, (tm + 1) * SLAB_ROWS, LANES), F32),
        compiler_params=_cparams("arbitrary"),
        name="moe",
    )(ptab.reshape(nt * ptab_w), ktab.reshape(nt * ptab_w), elist.reshape(nseg), meta.reshape(nt * 2),
      h2s, row_s, gate_s, wg, wu, wd)


def _combine_kernel(r_ref, xp_ref, g2_ref, fg_ref, o_ref, *, final):
    tt = xp_ref.shape[0]
    routed = jnp.concatenate([r_ref[pl.ds(j, tt, stride=SLAB_ROWS), :] for j in range(SLAB_ROWS)], axis=1)
    x2 = xp_ref[...] + g2_ref[...] * routed
    if final:
        x2 = _rmsnorm(x2, fg_ref[...])
    o_ref[...] = x2


def _combine_call(routed_s, xp, g2, final_g, tt, final):
    bx, tx, _ = xp.shape
    nt = tx // tt
    per_tok = g2.shape[1] != 1
    g2_spec = (pl.BlockSpec((None, tt, D_MODEL), lambda b, i: (b, i, 0)) if per_tok
               else pl.BlockSpec((None, 1, D_MODEL), lambda b, i: (b, 0, 0)))
    tok_spec = pl.BlockSpec((None, tt, D_MODEL), lambda b, i: (b, i, 0))
    per_tile = (routed_s.shape[1] // SLAB_ROWS - 1) // tt
    routed_spec = pl.BlockSpec((None, tt * SLAB_ROWS, LANES),
                               lambda b, i: ((b * nt + i) // per_tile, (b * nt + i) % per_tile, 0))
    return pl.pallas_call(
        functools.partial(_combine_kernel, final=final),
        grid=(bx, nt),
        in_specs=[routed_spec, tok_spec, g2_spec,
                  pl.BlockSpec((1, D_MODEL), lambda b, i: (0, 0))],
        out_specs=tok_spec,
        out_shape=jax.ShapeDtypeStruct((bx, tx, D_MODEL), F32),
        compiler_params=_cparams("parallel", "parallel"),
        name="combine",
    )(routed_s, xp, g2, final_g)


def _rope_tables(pos):
    half = HEAD_DIM // 2
    inv_freq = ROPE_THETA ** (-jnp.arange(half, dtype=F32) / half)
    ang = pos.astype(F32)[:, None] * inv_freq[None, :]
    cos = jnp.cos(ang)
    sin = jnp.sin(ang)
    return jnp.tile(cos, (1, 4)), jnp.tile(jnp.concatenate([-sin, sin], axis=1), (1, 2))


def _trunk(x, mods, pos, past_k, past_v, ssm_h0, layers, final_g, tt, tm, tok_batches):
    bx, tx, _ = x.shape
    b, t = tok_batches
    cos_t, sin_t = _rope_tables(pos)
    new_k, new_v, new_h = [], [], []
    for l, lw in enumerate(layers):
        sh1, sc1, g1, sh2, sc2, g2 = mods[l]
        u, q, k, v, gs, ga = _inproj_call(x, sh1, sc1, lw['norm_attn_g'], lw['w_in'], cos_t, sin_t, tt)
        ks = k.reshape(b, t, KV_W)
        vs = v.reshape(b, t, KV_W)
        qs = q.reshape(b, t, ATTN_W)
        if past_k is None:
            attn = _attn_call(lw['sink'], qs, ks, ks, vs, vs, True)
            new_k.append(ks[:, -WINDOW:].reshape(b, WINDOW, N_KV_HEADS, HEAD_DIM))
            new_v.append(vs[:, -WINDOW:].reshape(b, WINDOW, N_KV_HEADS, HEAD_DIM))
            h0 = jnp.zeros((b, SSM_GROUPS, SSM_STATE, 2), F32)
        else:
            pk = past_k[l].reshape(b, -1, KV_W)
            pv = past_v[l].reshape(b, -1, KV_W)
            attn = _attn_call(lw['sink'], qs, pk, ks, pv, vs, False)
            new_k.append(ks.reshape(b, t, N_KV_HEADS, HEAD_DIM))
            new_v.append(vs.reshape(b, t, N_KV_HEADS, HEAD_DIM))
            h0 = ssm_h0[l]
        y, h_last = _ssm_branch(u, lw['ssm_mats'], h0, b // bx, t // SSM_L)
        new_h.append(h_last)
        xp, h2s, topi, topw = _post_call(y, u, attn.reshape(bx, tx, ATTN_W), gs, ga, x,
                                         (g1, sh2, sc2, g2), lw, tt)
        routed_s = _moe_call(h2s, topi, topw, lw['ex_wg'], lw['ex_wu'], lw['ex_wd'], tm)
        x = _combine_call(routed_s, xp, g2, final_g, tt, l == len(layers) - 1)
    return x, jnp.stack(new_k), jnp.stack(new_v), jnp.stack(new_h)


def kernel(x_prompt, x_sample, cache_k, cache_v, state_ssm, c_prompt, c_sample, ada_w, ada_b, norm_attn_g,
           norm_ffn_g, w_in, ssm_a_re, ssm_a_im, ssm_log_dt, ssm_b_re, ssm_b_im, ssm_c_re, ssm_c_im, ssm_d,
           ssm_w_glu, ssm_b_glu, attn_sink, w_branch_ssm, w_branch_attn, w_out, router_w, router_bias,
           expert_w_gate, expert_w_up, expert_w_down, shared_w_gate, shared_w_up, shared_w_down, final_g):
    depth = ada_w.shape[0]
    bp, tp, _ = x_prompt.shape
    bs, ts, _ = x_sample.shape

    layers = []
    for l in range(depth):
        layers.append({
            'norm_attn_g': norm_attn_g[l][None], 'norm_ffn_g': norm_ffn_g[l][None],
            'w_in': w_in[l].astype(BF16), 'sink': attn_sink[l],
            'ssm_mats': _ssm_mats(ssm_a_re[l], ssm_a_im[l], ssm_log_dt[l], ssm_b_re[l], ssm_b_im[l],
                                  ssm_c_re[l], ssm_c_im[l]),
            'ssm_d': ssm_d[l][None], 'w_glu': ssm_w_glu[l].astype(BF16), 'b_glu': ssm_b_glu[l][None],
            'w_bs': w_branch_ssm[l].astype(BF16), 'w_ba': w_branch_attn[l].astype(BF16),
            'w_out': w_out[l].astype(BF16), 'router_wt': router_w[l].T, 'router_bias': router_bias[l][:, None],
            'sh_wg': shared_w_gate[l].astype(BF16), 'sh_wu': shared_w_up[l].astype(BF16),
            'sh_wd': shared_w_down[l].astype(BF16),
            'ex_wg': expert_w_gate[l].astype(BF16), 'ex_wu': expert_w_up[l].astype(BF16),
            'ex_wd': expert_w_down[l].astype(BF16),
        })
    fg = final_g[None]

    rows = 16
    c_all = jnp.concatenate([c_prompt, c_sample, jnp.zeros((rows - bp - bs, D_MODEL), F32)], axis=0)
    mod = _mod_call(c_all, ada_w, ada_b).reshape(depth, rows, 6, D_MODEL)
    mods_p = [[mod[l, :bp, j][:, None, :] for j in range(6)] for l in range(depth)]
    mods_s = [[jnp.repeat(mod[l, bp:bp + bs, j], ts, axis=0)[None] for j in range(6)] for l in range(depth)]

    pos_p = jnp.arange(tp, dtype=jnp.int32)
    pos_s = jnp.tile(PAST_LEN + jnp.arange(ts, dtype=jnp.int32), bs)

    y_p, k_p, v_p, h_p = _trunk(x_prompt, mods_p, pos_p, None, None, None, layers, fg,
                                tt=512, tm=min(4096, bp * tp), tok_batches=(bp, tp))
    y_s, k_s, v_s, h_s = _trunk(x_sample.reshape(1, bs * ts, D_MODEL), mods_s, pos_s, cache_k, cache_v,
                                state_ssm, layers, fg, tt=bs * ts, tm=bs * ts, tok_batches=(bs, ts))
    return (y_p, y_s.reshape(bs, ts, D_MODEL), k_p, v_p, h_p, k_s, v_s, h_s)
```

```python
import functools
import math

import jax
import jax.numpy as jnp
from jax import lax
from jax.experimental import pallas as pl
from jax.experimental.pallas import tpu as pltpu

F32 = jnp.float32
BF16 = jnp.bfloat16

D_MODEL = 1024
DEPTH = 2
PAST_LEN = 4096
CHUNK = 64
N_HEADS = 8
N_KV_HEADS = 2
HEAD_DIM = 64
WINDOW = 128
ROPE_THETA = 10000.0
D_SSM = 512
SSM_GROUP_CH = 16
SSM_GROUPS = D_SSM // SSM_GROUP_CH
SSM_STATE = 64
N_EXPERTS = 64
N_EXPERT_GROUPS = 8
EXPERTS_PER_GROUP = N_EXPERTS // N_EXPERT_GROUPS
TOPK_GROUPS = 4
TOP_K = 8
D_EXPERT = 256
D_SHARED = 256
ROUTED_SCALE = 2.5
RMS_EPS = 1e-6
NEG_INF = -1e30
ATTN_W = N_HEADS * HEAD_DIM
KV_W = N_KV_HEADS * HEAD_DIM
IN_COLS = D_SSM + ATTN_W + 2 * KV_W + 2 * D_MODEL
IN_SPLITS = (0, D_SSM, D_SSM + ATTN_W, D_SSM + ATTN_W + KV_W, D_SSM + ATTN_W + 2 * KV_W,
             D_SSM + ATTN_W + 2 * KV_W + D_MODEL, IN_COLS)

LANES = 128
SSM_L = 16
SSM_OCT_G = LANES // SSM_GROUP_CH
SSM_OCTS = SSM_GROUPS // SSM_OCT_G
SSM_FLAT = SSM_L * LANES
SSM_SW = SSM_OCT_G * SSM_STATE
SSM_POW_ROWS = 16
VMEM_LIMIT = 56 * 1024 * 1024
SUBLANES = 8
SLAB_ROWS = D_MODEL // LANES
MOE_BS = 128
MOE_STRIDE = MOE_BS + SUBLANES
MOE_RMW_UNROLL = 4
MOE_WSLOTS = 4
MOE_WORD_BITS = 16


def _cparams(*sem):
    return pltpu.CompilerParams(dimension_semantics=sem, vmem_limit_bytes=VMEM_LIMIT)


def _dot(a, b):
    return jnp.dot(a, b, preferred_element_type=F32)


def _dot_nt(a, b):
    return lax.dot_general(a, b, (((1,), (1,)), ((), ())), preferred_element_type=F32)


def _split(x):
    hi = x.astype(BF16)
    lo = (x - hi.astype(F32)).astype(BF16)
    return hi, lo


def _rmsnorm(x, g):
    return x * lax.rsqrt(jnp.mean(x * x, axis=-1, keepdims=True) + RMS_EPS) * g


def _mod_kernel(c_ref, w_ref, b_ref, o_ref):
    cond = jax.nn.silu(c_ref[...])
    ch, cl = _split(cond)
    wh, wl = _split(w_ref[...])
    o_ref[...] = _dot(ch, wh) + (_dot(ch, wl) + _dot(cl, wh)) + b_ref[...]


def _mod_call(c_all, ada_w, ada_b):
    depth = ada_w.shape[0]
    rows = c_all.shape[0]
    nj = 6
    return pl.pallas_call(
        _mod_kernel,
        grid=(depth, nj),
        in_specs=[
            pl.BlockSpec((rows, D_MODEL), lambda l, j: (0, 0)),
            pl.BlockSpec((None, D_MODEL, D_MODEL), lambda l, j: (l, 0, j)),
            pl.BlockSpec((None, 1, D_MODEL), lambda l, j: (l, 0, j)),
        ],
        out_specs=pl.BlockSpec((None, rows, D_MODEL), lambda l, j: (l, 0, j)),
        out_shape=jax.ShapeDtypeStruct((depth, rows, nj * D_MODEL), F32),
        compiler_params=_cparams("parallel", "parallel"),
        name="mod",
    )(c_all, ada_w, ada_b.reshape(depth, 1, nj * D_MODEL))


def _rope2(t, cos, sin_signed, first_half):
    swapped = jnp.where(first_half, pltpu.roll(t, LANES - HEAD_DIM // 2, axis=1),
                        pltpu.roll(t, HEAD_DIM // 2, axis=1))
    return t * cos + swapped * sin_signed


def _inproj_kernel(x_ref, sh_ref, sc_ref, g_ref, w_ref, cos_ref, sin_ref,
                   u_ref, q_ref, k_ref, v_ref, gs_ref, ga_ref):
    h = _rmsnorm(x_ref[...], g_ref[...]) * (1.0 + sc_ref[...]) + sh_ref[...]
    hb = h.astype(BF16)

    def proj(i):
        return _dot(hb, w_ref[:, IN_SPLITS[i]:IN_SPLITS[i + 1]])

    u_ref[...] = proj(0)
    cos = cos_ref[...]
    sin = sin_ref[...]
    lane = lax.broadcasted_iota(jnp.int32, cos.shape, 1)
    first_half = (lane % HEAD_DIM) < (HEAD_DIM // 2)
    q = proj(1)
    for j in range(ATTN_W // LANES):
        sl = slice(j * LANES, (j + 1) * LANES)
        q_ref[:, sl] = (_rope2(q[:, sl], cos, sin, first_half) * (HEAD_DIM ** -0.5)).astype(BF16)
    k_ref[...] = _rope2(proj(2), cos, sin, first_half)
    v_ref[...] = proj(3)
    gs_ref[...] = proj(4)
    ga_ref[...] = proj(5)


def _inproj_call(x, sh1, sc1, norm_g, w_in_bf, cos_t, sin_t, tt):
    bx, tx, _ = x.shape
    nt = tx // tt
    per_tok = sh1.shape[1] != 1
    mod_spec = (pl.BlockSpec((None, tt, D_MODEL), lambda b, i: (b, i, 0)) if per_tok
                else pl.BlockSpec((None, 1, D_MODEL), lambda b, i: (b, 0, 0)))

    def tok_spec(w):
        return pl.BlockSpec((None, tt, w), lambda b, i: (b, i, 0))

    def tok_shape(w, dt):
        return jax.ShapeDtypeStruct((bx, tx, w), dt)

    return pl.pallas_call(
        _inproj_kernel,
        grid=(bx, nt),
        in_specs=[
            tok_spec(D_MODEL), mod_spec, mod_spec,
            pl.BlockSpec((1, D_MODEL), lambda b, i: (0, 0)),
            pl.BlockSpec((D_MODEL, IN_COLS), lambda b, i: (0, 0)),
            pl.BlockSpec((tt, LANES), lambda b, i: (i, 0)),
            pl.BlockSpec((tt, LANES), lambda b, i: (i, 0)),
        ],
        out_specs=[tok_spec(D_SSM), tok_spec(ATTN_W), tok_spec(KV_W), tok_spec(KV_W),
                   tok_spec(D_MODEL), tok_spec(D_MODEL)],
        out_shape=[tok_shape(D_SSM, F32), tok_shape(ATTN_W, BF16), tok_shape(KV_W, F32),
                   tok_shape(KV_W, F32), tok_shape(D_MODEL, F32), tok_shape(D_MODEL, F32)],
        compiler_params=_cparams("parallel", "parallel"),
        name="inproj",
    )(x, sh1, sc1, norm_g, w_in_bf, cos_t, sin_t)


def _ssm_kernel(u_ref, kt_ref, bd_ref, rv_ref, cd_ref, fw_ref, ap_ref, h0_ref, y_ref, s_ref,
                t_ref, bl_ref, cl_ref, *, nb, cb):
    nc = nb * cb
    sw = SSM_SW

    @pl.when(pl.program_id(1) == 0)
    def _():
        zero = jnp.zeros((LANES, LANES), BF16)
        for s in range(SSM_L):
            for t in range(SSM_L):
                t_ref[s * LANES:(s + 1) * LANES, t * LANES:(t + 1) * LANES] = kt_ref[t - s] if t >= s else zero
        bre = bd_ref[:, :sw]
        bim = bd_ref[:, sw:]
        for s in range(SSM_L):
            rre = rv_ref[s:s + 1, :sw]
            rim = rv_ref[s:s + 1, sw:]
            bl_ref[s * LANES:(s + 1) * LANES, :sw] = (bre * rre - bim * rim).astype(BF16)
            bl_ref[s * LANES:(s + 1) * LANES, sw:] = (bre * rim + bim * rre).astype(BF16)
        cre = cd_ref[:sw, :]
        cim = cd_ref[sw:, :]
        for t in range(SSM_L):
            fre = fw_ref[:sw, t:t + 1]
            fim = fw_ref[sw:, t:t + 1]
            cl_ref[:sw, t * LANES:(t + 1) * LANES] = (cre * fre - cim * fim).astype(BF16)
            cl_ref[sw:, t * LANES:(t + 1) * LANES] = (-(cre * fim + cim * fre)).astype(BF16)

    uf = jnp.concatenate([u_ref[pl.ds(s, nc, stride=SSM_L), :].astype(BF16) for s in range(SSM_L)], axis=1)
    y = _dot(uf, t_ref[...])
    v = _dot(uf, bl_ref[...])
    xre = v[:, :sw]
    xim = v[:, sw:]

    row = lax.broadcasted_iota(jnp.int32, (nc, sw), 0)
    cidx = row & (cb - 1)
    bidx = row // cb
    h0 = h0_ref[...]
    h0re = jnp.zeros((nc, sw), F32)
    h0im = jnp.zeros((nc, sw), F32)
    for b in range(nb):
        h0re = jnp.where(bidx == b, h0[b:b + 1, :sw], h0re)
        h0im = jnp.where(bidx == b, h0[b:b + 1, sw:], h0im)
    first = cidx == 0
    are = ap_ref[0:1, :sw]
    aim = ap_ref[0:1, sw:]
    xre = xre + jnp.where(first, are * h0re - aim * h0im, 0.0)
    xim = xim + jnp.where(first, are * h0im + aim * h0re, 0.0)

    d = 1
    k = 0
    while d < cb:
        are = ap_ref[k:k + 1, :sw]
        aim = ap_ref[k:k + 1, sw:]
        keep = cidx >= d
        sre = jnp.where(keep, pltpu.roll(xre, d, axis=0), 0.0)
        sim = jnp.where(keep, pltpu.roll(xim, d, axis=0), 0.0)
        xre, xim = xre + (are * sre - aim * sim), xim + (are * sim + aim * sre)
        d *= 2
        k += 1

    for b in range(nb):
        r = b * cb + cb - 1
        s_ref[b:b + 1, :sw] = xre[r:r + 1, :]
        s_ref[b:b + 1, sw:] = xim[r:r + 1, :]

    pre = jnp.where(first, h0re, pltpu.roll(xre, 1, axis=0))
    pim = jnp.where(first, h0im, pltpu.roll(xim, 1, axis=0))
    y = y + _dot(pre.astype(BF16), cl_ref[:sw, :]) + _dot(pim.astype(BF16), cl_ref[sw:, :])
    for t in range(SSM_L):
        y_ref[pl.ds(t, nc, stride=SSM_L), :] = y[:, t * LANES:(t + 1) * LANES]


def _ssm_call(u, mats, h0o, nb, cb):
    kt, bd, rv, cd, fw, ap = mats
    bx, tx, _ = u.shape

    def const_spec(a):
        return pl.BlockSpec((None,) + a.shape[1:], lambda o, b: (o,) + (0,) * (a.ndim - 1))

    tok_spec = pl.BlockSpec((None, tx, LANES), lambda o, b: (b, 0, o))
    st_spec = pl.BlockSpec((None, None, nb, 2 * SSM_SW), lambda o, b: (o, b, 0, 0))
    return pl.pallas_call(
        functools.partial(_ssm_kernel, nb=nb, cb=cb),
        grid=(SSM_OCTS, bx),
        in_specs=[tok_spec] + [const_spec(a) for a in (kt, bd, rv, cd, fw, ap)] + [st_spec],
        out_specs=[tok_spec, st_spec],
        out_shape=[jax.ShapeDtypeStruct((bx, tx, D_SSM), F32),
                   jax.ShapeDtypeStruct((SSM_OCTS, bx, nb, 2 * SSM_SW), F32)],
        scratch_shapes=[pltpu.VMEM((SSM_FLAT, SSM_FLAT), BF16), pltpu.VMEM((SSM_FLAT, 2 * SSM_SW), BF16),
                        pltpu.VMEM((2 * SSM_SW, SSM_FLAT), BF16)],
        compiler_params=_cparams("arbitrary", "arbitrary"),
        name="ssm",
    )(u, kt, bd, rv, cd, fw, ap, h0o)


def _oct_lanes(a):
    lead = a.shape[:-2]
    a = a.reshape(*lead, SSM_OCTS, SSM_SW)
    return jnp.moveaxis(a, -2, 0)


def _ssm_mats(a_re, a_im, log_dt, b_re, b_im, c_re, c_im):
    hp = lax.Precision.HIGHEST
    dt = jnp.exp(log_dt)[:, None]
    lre = a_re * dt
    lim = a_im * dt
    mag = jnp.exp(lre)
    ab_re = mag * jnp.cos(lim)
    ab_im = mag * jnp.sin(lim)
    den = a_re * a_re + a_im * a_im
    n_re = ab_re - 1.0
    f_re = (n_re * a_re + ab_im * a_im) / den
    f_im = (ab_im * a_re - n_re * a_im) / den
    bb_re = f_re[..., None] * b_re - f_im[..., None] * b_im
    bb_im = f_re[..., None] * b_im + f_im[..., None] * b_re

    def lam_pow(tau):
        tau = tau[:, None, None]
        m = jnp.exp(tau * lre)
        return m * jnp.cos(tau * lim), m * jnp.sin(tau * lim)

    pw_re, pw_im = lam_pow(jnp.arange(SSM_L + 1, dtype=F32))
    cp_re = c_re[None] * pw_re[:, :, None, :] - c_im[None] * pw_im[:, :, None, :]
    cp_im = c_re[None] * pw_im[:, :, None, :] + c_im[None] * pw_re[:, :, None, :]
    kern = (jnp.einsum('tgcn,gnd->tgcd', cp_re, bb_re, precision=hp)
            - jnp.einsum('tgcn,gnd->tgcd', cp_im, bb_im, precision=hp))
    eye = jnp.eye(SSM_OCT_G, dtype=F32)
    og = (SSM_OCTS, SSM_OCT_G)

    def slab_diag(m):
        lead = m.shape[:-3]
        a, b = m.shape[-2:]
        m = m.reshape(*lead, *og, a, b)
        m = m[..., :, None, :] * eye[:, None, :, None]
        return m.reshape(*lead, SSM_OCTS, SSM_OCT_G * a, SSM_OCT_G * b)

    kt = slab_diag(kern.transpose(0, 1, 3, 2)).astype(BF16).transpose(1, 0, 2, 3)
    bd = jnp.concatenate([slab_diag(bb_re.transpose(0, 2, 1)), slab_diag(bb_im.transpose(0, 2, 1))], axis=-1)
    rev = SSM_L - 1 - jnp.arange(SSM_L)
    rv = jnp.concatenate([_oct_lanes(pw_re[rev]), _oct_lanes(pw_im[rev])], axis=-1)
    cd = jnp.concatenate([slab_diag(c_re.transpose(0, 2, 1)), slab_diag(c_im.transpose(0, 2, 1))], axis=1)
    fw = jnp.concatenate([_oct_lanes(pw_re[1:]), _oct_lanes(pw_im[1:])], axis=-1).transpose(0, 2, 1)
    fw = jnp.pad(fw, ((0, 0), (0, 0), (0, LANES - SSM_L)))

    steps = float(SSM_L) * (2.0 ** jnp.arange(SSM_POW_ROWS, dtype=F32))
    ap_re, ap_im = lam_pow(steps)
    ap = jnp.concatenate([_oct_lanes(ap_re), _oct_lanes(ap_im)], axis=-1)
    return kt, bd, rv, cd, fw, ap


def _ssm_branch(u, mats, h0, nb, cb):
    bx = u.shape[0]
    h0o = jnp.concatenate([_oct_lanes(h0[..., 0]), _oct_lanes(h0[..., 1])], axis=-1)
    y, s = _ssm_call(u, mats, h0o.reshape(SSM_OCTS, bx, nb, 2 * SSM_SW), nb, cb)
    s = s.reshape(SSM_OCTS, bx * nb, 2, SSM_OCT_G, SSM_STATE).transpose(1, 0, 3, 4, 2)
    return y, s.reshape(bx * nb, SSM_GROUPS, SSM_STATE, 2)


def _attn_kernel(sink_ref, q_ref, ka_ref, kb_ref, va_ref, vb_ref, o_ref, *, banded):
    rows = q_ref.shape[0]
    nk = 2 * LANES
    if banded:
        kfull = jnp.concatenate([ka_ref[...], kb_ref[...]], axis=0)
        vfull = jnp.concatenate([va_ref[...], vb_ref[...]], axis=0)
    else:
        pad = jnp.zeros((nk - ka_ref.shape[0] - kb_ref.shape[0], KV_W), F32)
        kfull = jnp.concatenate([ka_ref[...], kb_ref[...], pad], axis=0)
        vfull = jnp.concatenate([va_ref[...], vb_ref[...], pad], axis=0)

    r_i = lax.broadcasted_iota(jnp.int32, (rows, nk), 0)
    c_i = lax.broadcasted_iota(jnp.int32, (rows, nk), 1)
    if banded:
        lo = (r_i // CHUNK) * CHUNK
        first_frame = jnp.where(pl.program_id(1) > 0, 0, WINDOW)
        valid = (c_i >= lo) & (c_i < lo + WINDOW + CHUNK) & (c_i >= first_frame)
    else:
        valid = c_i < (ka_ref.shape[0] + kb_ref.shape[0])

    lane = lax.broadcasted_iota(jnp.int32, (nk, KV_W), 1)
    low = lane < HEAD_DIM
    kroll = pltpu.roll(kfull, HEAD_DIM, axis=1)
    vroll = pltpu.roll(vfull, HEAD_DIM, axis=1)

    for g in range(N_KV_HEADS):
        k_lo, k_hi = (kfull, kroll) if g == 0 else (kroll, kfull)
        v_lo, v_hi = (vfull, vroll) if g == 0 else (vroll, vfull)
        kpad = (jnp.where(low, k_lo, 0.0).astype(BF16), jnp.where(low, 0.0, k_hi).astype(BF16))
        vpad = (jnp.where(low, v_lo, 0.0).astype(BF16), jnp.where(low, 0.0, v_hi).astype(BF16))
        for pp in range(2):
            slab = 2 * g + pp
            qp = q_ref[:, slab * LANES:(slab + 1) * LANES]
            acc = None
            for hh in range(2):
                sk = sink_ref[2 * slab + hh]
                s = jnp.where(valid, _dot_nt(qp, kpad[hh]), NEG_INF)
                m = jnp.maximum(jnp.max(s, axis=-1, keepdims=True), sk)
                p = jnp.exp(s - m)
                den = jnp.sum(p, axis=-1, keepdims=True) + jnp.exp(sk - m)
                o = _dot(p.astype(BF16), vpad[hh]) / den
                acc = o if acc is None else acc + o
            o_ref[:, slab * LANES:(slab + 1) * LANES] = acc.astype(BF16)


def _attn_call(sink, q, ka, kb, va, vb, banded):
    bx, tx, _ = q.shape
    if banded:
        rows = 2 * CHUNK
        nt = tx // rows
        grid = (bx, nt)
        q_spec = pl.BlockSpec((None, rows, ATTN_W), lambda b, i: (b, i, 0))
        prev = pl.BlockSpec((None, WINDOW, KV_W), lambda b, i: (b, jnp.maximum(i - 1, 0), 0))
        cur = pl.BlockSpec((None, rows, KV_W), lambda b, i: (b, i, 0))
    else:
        rows = tx
        grid = (bx, 1)
        q_spec = pl.BlockSpec((None, rows, ATTN_W), lambda b, i: (b, 0, 0))
        prev = pl.BlockSpec((None, ka.shape[1], KV_W), lambda b, i: (b, 0, 0))
        cur = pl.BlockSpec((None, rows, KV_W), lambda b, i: (b, 0, 0))
    return pl.pallas_call(
        functools.partial(_attn_kernel, banded=banded),
        grid=grid,
        in_specs=[pl.BlockSpec(memory_space=pltpu.SMEM), q_spec, prev, cur, prev, cur],
        out_specs=q_spec,
        out_shape=jax.ShapeDtypeStruct((bx, tx, ATTN_W), BF16),
        compiler_params=_cparams("parallel", "parallel"),
        name="attn",
    )(sink, q, ka, kb, va, vb)


def _route(logits_t, bias_col, topi_ref, topw_ref):
    tt = logits_t.shape[1]
    scores = jax.nn.sigmoid(logits_t)
    biased = scores + bias_col
    sub = lax.broadcasted_iota(jnp.int32, (EXPERTS_PER_GROUP, tt), 0).astype(F32)
    ninf = float('-inf')
    blocks = [biased[EXPERTS_PER_GROUP * g:EXPERTS_PER_GROUP * (g + 1), :] for g in range(N_EXPERT_GROUPS)]

    gscore = jnp.zeros((N_EXPERT_GROUPS, tt), F32)
    for g in range(N_EXPERT_GROUPS):
        blk = blocks[g]
        m1 = jnp.max(blk, axis=0, keepdims=True)
        i1 = jnp.min(jnp.where(blk == m1, sub, float(EXPERTS_PER_GROUP)), axis=0, keepdims=True)
        m2 = jnp.max(jnp.where(sub == i1, ninf, blk), axis=0, keepdims=True)
        gscore = jnp.where(sub == float(g), jnp.broadcast_to(m1 + m2, gscore.shape), gscore)

    grank = jnp.zeros((N_EXPERT_GROUPS, tt), F32)
    for j in range(N_EXPERT_GROUPS):
        rj = jnp.broadcast_to(gscore[j:j + 1, :], gscore.shape)
        beats = (rj > gscore) | ((rj == gscore) & (sub > float(j)))
        grank = grank + jnp.where(beats, 1.0, 0.0)
    gsel = jnp.where(grank < float(TOPK_GROUPS), 1.0, 0.0)

    masked = []
    for g in range(N_EXPERT_GROUPS):
        on = jnp.broadcast_to(gsel[g:g + 1, :], blocks[g].shape) > 0.5
        masked.append(jnp.where(on, blocks[g], NEG_INF))

    ranks = [jnp.zeros((EXPERTS_PER_GROUP, tt), F32) for _ in range(N_EXPERT_GROUPS)]
    for jb in range(N_EXPERT_GROUPS):
        for jj in range(EXPERTS_PER_GROUP):
            rj = jnp.broadcast_to(masked[jb][jj:jj + 1, :], (EXPERTS_PER_GROUP, tt))
            for ib in range(N_EXPERT_GROUPS):
                mi = masked[ib]
                if ib < jb:
                    beats = rj > mi
                elif ib > jb:
                    beats = rj >= mi
                else:
                    beats = (rj > mi) | ((rj == mi) & (sub > float(jj)))
                ranks[ib] = ranks[ib] + jnp.where(beats, 1.0, 0.0)

    topi = jnp.zeros((TOP_K, tt), F32)
    topw = jnp.zeros((TOP_K, tt), F32)
    for r in range(TOP_K):
        ai = jnp.zeros((EXPERTS_PER_GROUP, tt), F32)
        aw = jnp.zeros((EXPERTS_PER_GROUP, tt), F32)
        for g in range(N_EXPERT_GROUPS):
            hit = ranks[g] == float(r)
            ai = ai + jnp.where(hit, sub + float(EXPERTS_PER_GROUP * g), 0.0)
            aw = aw + jnp.where(hit, scores[EXPERTS_PER_GROUP * g:EXPERTS_PER_GROUP * (g + 1), :], 0.0)
        ir = jnp.sum(ai, axis=0, keepdims=True)
        wr = jnp.sum(aw, axis=0, keepdims=True)
        topi = jnp.where(sub == float(r), jnp.broadcast_to(ir, topi.shape), topi)
        topw = jnp.where(sub == float(r), jnp.broadcast_to(wr, topw.shape), topw)
    den = jnp.sum(topw, axis=0, keepdims=True)
    topi_ref[...] = topi.astype(jnp.int32)
    topw_ref[...] = topw / den * ROUTED_SCALE


def _post_kernel(y_ref, u_ref, a_ref, gs_ref, ga_ref, x_ref, g1_ref, sh2_ref, sc2_ref, g2_ref,
                 d_ref, wglu_ref, bglu_ref, wbs_ref, wba_ref, wout_ref, nf_ref, rwt_ref, rb_ref,
                 swg_ref, swu_ref, swd_ref, xp_ref, h2s_ref, topi_ref, topw_ref):
    ys = y_ref[...] + d_ref[...] * u_ref[...]
    z = jax.nn.gelu(ys)
    y2 = z * jax.nn.sigmoid(_dot(z.astype(BF16), wglu_ref[...]) + bglu_ref[...])
    merged = (jax.nn.sigmoid(gs_ref[...]) * _dot(y2.astype(BF16), wbs_ref[...])
              + jax.nn.sigmoid(ga_ref[...]) * _dot(a_ref[...], wba_ref[...]))
    x1 = x_ref[...] + g1_ref[...] * _dot(merged.astype(BF16), wout_ref[...])

    h2 = _rmsnorm(x1, nf_ref[...]) * (1.0 + sc2_ref[...]) + sh2_ref[...]
    tt = h2.shape[0]
    for j in range(SLAB_ROWS):
        h2s_ref[pl.ds(j, tt, stride=SLAB_ROWS), :] = h2[:, j * LANES:(j + 1) * LANES]
    hh, hl = _split(h2)
    rh, rl = _split(rwt_ref[...])
    logits_t = _dot_nt(rh, hh) + (_dot_nt(rh, hl) + _dot_nt(rl, hh))
    _route(logits_t, rb_ref[...], topi_ref, topw_ref)

    shared = _dot((jax.nn.silu(_dot(hh, swg_ref[...])) * _dot(hh, swu_ref[...])).astype(BF16), swd_ref[...])
    xp_ref[...] = x1 + g2_ref[...] * shared


def _post_call(y, u, attn, gs, ga, x, mods, lw, tt):
    bx, tx, _ = x.shape
    nt = tx // tt
    per_tok = mods[0].shape[1] != 1
    mod_spec = (pl.BlockSpec((None, tt, D_MODEL), lambda b, i: (b, i, 0)) if per_tok
                else pl.BlockSpec((None, 1, D_MODEL), lambda b, i: (b, 0, 0)))

    def tok_spec(w):
        return pl.BlockSpec((None, tt, w), lambda b, i: (b, i, 0))

    def full(a):
        return pl.BlockSpec(a.shape, lambda b, i: (0,) * a.ndim)

    weights = [lw['ssm_d'], lw['w_glu'], lw['b_glu'], lw['w_bs'], lw['w_ba'], lw['w_out'], lw['norm_ffn_g'],
               lw['router_wt'], lw['router_bias'], lw['sh_wg'], lw['sh_wu'], lw['sh_wd']]
    return pl.pallas_call(
        _post_kernel,
        grid=(bx, nt),
        in_specs=[tok_spec(D_SSM), tok_spec(D_SSM), tok_spec(ATTN_W), tok_spec(D_MODEL), tok_spec(D_MODEL),
                  tok_spec(D_MODEL), mod_spec, mod_spec, mod_spec, mod_spec] + [full(w) for w in weights],
        out_specs=[tok_spec(D_MODEL),
                   pl.BlockSpec((tt * SLAB_ROWS, LANES), lambda b, i: (b * nt + i, 0)),
                   pl.BlockSpec((TOP_K, tt), lambda b, i: (0, b * nt + i)),
                   pl.BlockSpec((TOP_K, tt), lambda b, i: (0, b * nt + i))],
        out_shape=[jax.ShapeDtypeStruct((bx, tx, D_MODEL), F32),
                   jax.ShapeDtypeStruct((bx * tx * SLAB_ROWS, LANES), F32),
                   jax.ShapeDtypeStruct((TOP_K, bx * tx), jnp.int32),
                   jax.ShapeDtypeStruct((TOP_K, bx * tx), F32)],
        compiler_params=_cparams("parallel", "parallel"),
        name="post",
    )(y, u, attn, gs, ga, x, *mods, *weights)


def _moe_kernel(ptab_ref, ktab_ref, elist_ref, meta_ref, hs_ref, word_hbm, gate_hbm, wg_hbm, wu_hbm, wd_hbm, o_ref,
                word_s, gate_s, xt0, xt1, yt0, yt1, wg_buf, wu_buf, wd_buf, lsem, wsem, *, tm, ptab_w):
    i = pl.program_id(0)
    nslot = word_s.shape[0]
    ngate = gate_s.shape[0]
    acc_ref = o_ref
    slab_mask = -SLAB_ROWS
    gather_mask = (tm * SLAB_ROWS - 1) & slab_mask
    tab0 = i * ptab_w + 1
    npairs = meta_ref[2 * i]
    nrank = meta_ref[2 * i + 1]

    def list_copies():
        return (pltpu.make_async_copy(word_hbm.at[pl.ds(pl.multiple_of(i * nslot, LANES), nslot)],
                                      word_s, lsem.at[0]),
                pltpu.make_async_copy(gate_hbm.at[pl.ds(pl.multiple_of(i * ngate, LANES), ngate)],
                                      gate_s, lsem.at[1]))

    def weight_copies(k):
        ex = elist_ref[i * N_EXPERTS + k]
        slot = k % MOE_WSLOTS
        return (pltpu.make_async_copy(wg_hbm.at[ex], wg_buf.at[slot], wsem.at[slot, 0]),
                pltpu.make_async_copy(wu_hbm.at[ex], wu_buf.at[slot], wsem.at[slot, 1]),
                pltpu.make_async_copy(wd_hbm.at[ex], wd_buf.at[slot], wsem.at[slot, 2]))

    def gather(base, xt_ref):
        for r in range(MOE_BS):
            row = pl.multiple_of(word_s[base + r] & gather_mask, SLAB_ROWS)
            xt_ref[pl.ds(r, SLAB_ROWS, stride=MOE_STRIDE), :] = hs_ref[pl.ds(row, SLAB_ROWS), :]

    def expert(xt_ref, yt_ref, k):
        slot = k % MOE_WSLOTS
        x = jnp.concatenate([xt_ref[j * MOE_STRIDE:j * MOE_STRIDE + MOE_BS, :] for j in range(SLAB_ROWS)],
                            axis=1).astype(BF16)
        act = jax.nn.silu(_dot(x, wg_buf[slot])) * _dot(x, wu_buf[slot])
        y = _dot(act.astype(BF16), wd_buf[slot])
        for j in range(SLAB_ROWS):
            yt_ref[j * MOE_STRIDE:j * MOE_STRIDE + MOE_BS, :] = y[:, j * LANES:(j + 1) * LANES]

    def scatter(base, yt_ref):
        for r0 in range(0, MOE_BS, MOE_RMW_UNROLL):
            upd = []
            for r in range(r0, r0 + MOE_RMW_UNROLL):
                word = word_s[base + r]
                row = pl.multiple_of(word & slab_mask, SLAB_ROWS)
                upd.append((row, acc_ref[pl.ds(row, SLAB_ROWS), :]
                            + gate_s[word] * yt_ref[pl.ds(r, SLAB_ROWS, stride=MOE_STRIDE), :]))
            for row, val in upd:
                acc_ref[pl.ds(row, SLAB_ROWS), :] = val

    for c in list_copies():
        c.start()
    ahead = MOE_WSLOTS - 2
    for r in range(ahead + 1):
        @pl.when(r < nrank)
        def _():
            for c in weight_copies(r):
                c.start()

    acc_ref[...] = jnp.zeros_like(acc_ref)
    yt1[...] = jnp.zeros_like(yt1)
    for c in list_copies():
        c.wait()
    gather(ptab_ref[tab0], xt0)

    def pair(j, carry):
        waited, started = carry
        t = tab0 + 2 * j
        k0 = ktab_ref[t]
        k1 = ktab_ref[t + 1]

        target = jnp.minimum(k1 + ahead, nrank - 1)
        for d in (1, 2):
            @pl.when(started + d <= target)
            def _():
                for c in weight_copies(started + d):
                    c.start()

        for d in (1, 2):
            @pl.when(waited + d <= k1)
            def _():
                for c in weight_copies(waited + d):
                    c.wait()

        gather(ptab_ref[t + 1], xt1)
        expert(xt0, yt0, k0)
        scatter(ptab_ref[t - 1], yt1)
        gather(ptab_ref[t + 2], xt0)
        expert(xt1, yt1, k1)
        scatter(ptab_ref[t], yt0)
        return jnp.maximum(waited, k1), jnp.maximum(started, target)

    lax.fori_loop(0, npairs, pair, (jnp.int32(-1), jnp.minimum(jnp.int32(ahead), nrank - 1)))
    scatter(ptab_ref[tab0 + 2 * npairs - 1], yt1)


def _moe_call(h2s, topi, topw, wg, wu, wd, tm):
    n = h2s.shape[0] // SLAB_ROWS
    nt = n // tm
    nseg = nt * N_EXPERTS
    nreal = tm * TOP_K
    nslot = nreal + (N_EXPERTS + 1) * MOE_BS
    null_base = nslot - MOE_BS
    ptab_w = nreal // MOE_BS + N_EXPERTS + 4

    i32 = jnp.int32
    experts = jnp.arange(N_EXPERTS, dtype=i32)
    counts = jnp.sum((topi.reshape(TOP_K, nt, tm)[..., None] == experts).astype(i32), axis=(0, 2))
    nblk = (counts + MOE_BS - 1) // MOE_BS
    first = jnp.cumsum(counts + MOE_BS, axis=1) - (counts + MOE_BS)
    pend = jnp.cumsum(nblk, axis=1)
    pstart = pend - nblk
    nonempty = (nblk > 0).astype(i32)
    krank = jnp.cumsum(nonempty, axis=1) - 1
    nrank = jnp.sum(nonempty, axis=1)
    elist = jnp.sum(jnp.where((nonempty[:, None, :] == 1) & (krank[:, None, :] == experts[None, :, None]),
                              experts[None, None, :], 0), axis=-1).astype(i32)
    pos = jnp.arange(ptab_w, dtype=i32) - 1
    e_of = jnp.sum((pend[:, None, :] <= pos[None, :, None]).astype(i32), axis=-1)
    e_cl = jnp.minimum(e_of, N_EXPERTS - 1)
    q = pos[None, :] - jnp.take_along_axis(pstart, e_cl, axis=1)
    real_blk = (pos[None, :] >= 0) & (e_of < N_EXPERTS)
    ptab = jnp.where(real_blk, jnp.take_along_axis(first, e_cl, axis=1) + q * MOE_BS, null_base).astype(i32)
    ktab = jnp.where(real_blk, jnp.take_along_axis(krank, e_cl, axis=1), nrank[:, None] - 1).astype(i32)
    meta = jnp.stack([(pend[:, -1] + 1) // 2, nrank], axis=1).astype(i32)

    assert TOP_K == SLAB_ROWS and nreal < (1 << MOE_WORD_BITS) and 2 * nseg + 2 < (1 << (31 - MOE_WORD_BITS))
    slot = jnp.arange(n * TOP_K, dtype=i32)
    key = 2 * ((slot // nreal) * N_EXPERTS + topi.T.reshape(n * TOP_K))
    pad_key = jnp.concatenate([jnp.repeat(2 * jnp.arange(nseg, dtype=i32) + 1, MOE_BS),
                               jnp.repeat(2 * (jnp.arange(nt, dtype=i32) * N_EXPERTS + N_EXPERTS - 1) + 1, MOE_BS)])
    packed = jnp.concatenate([(key << MOE_WORD_BITS) | (slot % nreal), (pad_key << MOE_WORD_BITS) | nreal])
    word_s = lax.sort(packed, is_stable=False) & ((1 << MOE_WORD_BITS) - 1)
    ngate = nreal + LANES
    gate_s = jnp.concatenate([topw.T.reshape(nt, nreal), jnp.zeros((nt, LANES), F32)], axis=1).reshape(nt * ngate)

    stage = pltpu.VMEM((SLAB_ROWS * MOE_STRIDE, LANES), F32)
    hbm = pl.BlockSpec(memory_space=pl.ANY)
    grid_spec = pltpu.PrefetchScalarGridSpec(
        num_scalar_prefetch=4,
        grid=(nt,),
        in_specs=[
            pl.BlockSpec((tm * SLAB_ROWS, LANES), lambda i, *_: (i, 0), pipeline_mode=pl.Buffered(1)),
            hbm, hbm, hbm, hbm, hbm,
        ],
        out_specs=pl.BlockSpec((None, (tm + 1) * SLAB_ROWS, LANES), lambda i, *_: (i, 0, 0),
                               pipeline_mode=pl.Buffered(1)),
        scratch_shapes=[
            pltpu.SMEM((nslot,), i32),
            pltpu.SMEM((ngate,), F32),
            stage, stage, stage, stage,
            pltpu.VMEM((MOE_WSLOTS, D_MODEL, D_EXPERT), BF16),
            pltpu.VMEM((MOE_WSLOTS, D_MODEL, D_EXPERT), BF16),
            pltpu.VMEM((MOE_WSLOTS, D_EXPERT, D_MODEL), BF16),
            pltpu.SemaphoreType.DMA((2,)),
            pltpu.SemaphoreType.DMA((MOE_WSLOTS, 3)),
        ],
    )
    return pl.pallas_call(
        functools.partial(_moe_kernel, tm=tm, ptab_w=ptab_w),
        grid_spec=grid_spec,
        out_shape=jax.ShapeDtypeStruct((nt, (tm + 1) * SLAB_ROWS, LANES), F32),
        compiler_params=_cparams("arbitrary"),
        name="moe",
    )(ptab.reshape(nt * ptab_w), ktab.reshape(nt * ptab_w), elist.reshape(nseg), meta.reshape(nt * 2),
      h2s, word_s, gate_s, wg, wu, wd)


def _combine_kernel(r_ref, xp_ref, g2_ref, fg_ref, o_ref, *, final):
    tt = xp_ref.shape[0]
    routed = jnp.concatenate([r_ref[pl.ds(j, tt, stride=SLAB_ROWS), :] for j in range(SLAB_ROWS)], axis=1)
    x2 = xp_ref[...] + g2_ref[...] * routed
    if final:
        x2 = _rmsnorm(x2, fg_ref[...])
    o_ref[...] = x2


def _combine_call(routed_s, xp, g2, final_g, tt, final):
    bx, tx, _ = xp.shape
    nt = tx // tt
    per_tok = g2.shape[1] != 1
    g2_spec = (pl.BlockSpec((None, tt, D_MODEL), lambda b, i: (b, i, 0)) if per_tok
               else pl.BlockSpec((None, 1, D_MODEL), lambda b, i: (b, 0, 0)))
    tok_spec = pl.BlockSpec((None, tt, D_MODEL), lambda b, i: (b, i, 0))
    per_tile = (routed_s.shape[1] // SLAB_ROWS - 1) // tt
    routed_spec = pl.BlockSpec((None, tt * SLAB_ROWS, LANES),
                               lambda b, i: ((b * nt + i) // per_tile, (b * nt + i) % per_tile, 0))
    return pl.pallas_call(
        functools.partial(_combine_kernel, final=final),
        grid=(bx, nt),
        in_specs=[routed_spec, tok_spec, g2_spec,
                  pl.BlockSpec((1, D_MODEL), lambda b, i: (0, 0))],
        out_specs=tok_spec,
        out_shape=jax.ShapeDtypeStruct((bx, tx, D_MODEL), F32),
        compiler_params=_cparams("parallel", "parallel"),
        name="combine",
    )(routed_s, xp, g2, final_g)


def _rope_tables(pos):
    half = HEAD_DIM // 2
    inv_freq = ROPE_THETA ** (-jnp.arange(half, dtype=F32) / half)
    ang = pos.astype(F32)[:, None] * inv_freq[None, :]
    cos = jnp.cos(ang)
    sin = jnp.sin(ang)
    return jnp.tile(cos, (1, 4)), jnp.tile(jnp.concatenate([-sin, sin], axis=1), (1, 2))


def _trunk(x, mods, pos, past_k, past_v, ssm_h0, layers, final_g, tt, tm, tok_batches):
    bx, tx, _ = x.shape
    b, t = tok_batches
    cos_t, sin_t = _rope_tables(pos)
    new_k, new_v, new_h = [], [], []
    for l, lw in enumerate(layers):
        sh1, sc1, g1, sh2, sc2, g2 = mods[l]
        u, q, k, v, gs, ga = _inproj_call(x, sh1, sc1, lw['norm_attn_g'], lw['w_in'], cos_t, sin_t, tt)
        ks = k.reshape(b, t, KV_W)
        vs = v.reshape(b, t, KV_W)
        qs = q.reshape(b, t, ATTN_W)
        if past_k is None:
            attn = _attn_call(lw['sink'], qs, ks, ks, vs, vs, True)
            new_k.append(ks[:, -WINDOW:].reshape(b, WINDOW, N_KV_HEADS, HEAD_DIM))
            new_v.append(vs[:, -WINDOW:].reshape(b, WINDOW, N_KV_HEADS, HEAD_DIM))
            h0 = jnp.zeros((b, SSM_GROUPS, SSM_STATE, 2), F32)
        else:
            pk = past_k[l].reshape(b, -1, KV_W)
            pv = past_v[l].reshape(b, -1, KV_W)
            attn = _attn_call(lw['sink'], qs, pk, ks, pv, vs, False)
            new_k.append(ks.reshape(b, t, N_KV_HEADS, HEAD_DIM))
            new_v.append(vs.reshape(b, t, N_KV_HEADS, HEAD_DIM))
            h0 = ssm_h0[l]
        y, h_last = _ssm_branch(u, lw['ssm_mats'], h0, b // bx, t // SSM_L)
        new_h.append(h_last)
        xp, h2s, topi, topw = _post_call(y, u, attn.reshape(bx, tx, ATTN_W), gs, ga, x,
                                         (g1, sh2, sc2, g2), lw, tt)
        routed_s = _moe_call(h2s, topi, topw, lw['ex_wg'], lw['ex_wu'], lw['ex_wd'], tm)
        x = _combine_call(routed_s, xp, g2, final_g, tt, l == len(layers) - 1)
    return x, jnp.stack(new_k), jnp.stack(new_v), jnp.stack(new_h)


def kernel(x_prompt, x_sample, cache_k, cache_v, state_ssm, c_prompt, c_sample, ada_w, ada_b, norm_attn_g,
           norm_ffn_g, w_in, ssm_a_re, ssm_a_im, ssm_log_dt, ssm_b_re, ssm_b_im, ssm_c_re, ssm_c_im, ssm_d,
           ssm_w_glu, ssm_b_glu, attn_sink, w_branch_ssm, w_branch_attn, w_out, router_w, router_bias,
           expert_w_gate, expert_w_up, expert_w_down, shared_w_gate, shared_w_up, shared_w_down, final_g):
    depth = ada_w.shape[0]
    bp, tp, _ = x_prompt.shape
    bs, ts, _ = x_sample.shape

    layers = []
    for l in range(depth):
        layers.append({
            'norm_attn_g': norm_attn_g[l][None], 'norm_ffn_g': norm_ffn_g[l][None],
            'w_in': w_in[l].astype(BF16), 'sink': attn_sink[l],
            'ssm_mats': _ssm_mats(ssm_a_re[l], ssm_a_im[l], ssm_log_dt[l], ssm_b_re[l], ssm_b_im[l],
                                  ssm_c_re[l], ssm_c_im[l]),
            'ssm_d': ssm_d[l][None], 'w_glu': ssm_w_glu[l].astype(BF16), 'b_glu': ssm_b_glu[l][None],
            'w_bs': w_branch_ssm[l].astype(BF16), 'w_ba': w_branch_attn[l].astype(BF16),
            'w_out': w_out[l].astype(BF16), 'router_wt': router_w[l].T, 'router_bias': router_bias[l][:, None],
            'sh_wg': shared_w_gate[l].astype(BF16), 'sh_wu': shared_w_up[l].astype(BF16),
            'sh_wd': shared_w_down[l].astype(BF16),
            'ex_wg': expert_w_gate[l].astype(BF16), 'ex_wu': expert_w_up[l].astype(BF16),
            'ex_wd': expert_w_down[l].astype(BF16),
        })
    fg = final_g[None]

    rows = 16
    c_all = jnp.concatenate([c_prompt, c_sample, jnp.zeros((rows - bp - bs, D_MODEL), F32)], axis=0)
    mod = _mod_call(c_all, ada_w, ada_b).reshape(depth, rows, 6, D_MODEL)
    mods_p = [[mod[l, :bp, j][:, None, :] for j in range(6)] for l in range(depth)]
    mods_s = [[jnp.repeat(mod[l, bp:bp + bs, j], ts, axis=0)[None] for j in range(6)] for l in range(depth)]

    pos_p = jnp.arange(tp, dtype=jnp.int32)
    pos_s = jnp.tile(PAST_LEN + jnp.arange(ts, dtype=jnp.int32), bs)

    y_p, k_p, v_p, h_p = _trunk(x_prompt, mods_p, pos_p, None, None, None, layers, fg,
                                tt=512, tm=min(4096, bp * tp), tok_batches=(bp, tp))
    y_s, k_s, v_s, h_s = _trunk(x_sample.reshape(1, bs * ts, D_MODEL), mods_s, pos_s, cache_k, cache_v,
                                state_ssm, layers, fg, tt=bs * ts, tm=bs * ts, tok_batches=(bs, ts))
    return (y_p, y_s.reshape(bs, ts, D_MODEL), k_p, v_p, h_p, k_s, v_s, h_s)
```

```python
import functools

import jax
import jax.numpy as jnp
from jax import lax
from jax.experimental import pallas as pl
from jax.experimental.pallas import tpu as pltpu

F32 = jnp.float32
BF16 = jnp.bfloat16

D_MODEL = 1024
DEPTH = 2
PAST_LEN = 4096
CHUNK = 64
N_HEADS = 8
N_KV_HEADS = 2
HEAD_DIM = 64
WINDOW = 128
ROPE_THETA = 10000.0
D_SSM = 512
SSM_GROUP_CH = 16
SSM_GROUPS = D_SSM // SSM_GROUP_CH
SSM_STATE = 64
N_EXPERTS = 64
N_EXPERT_GROUPS = 8
EXPERTS_PER_GROUP = N_EXPERTS // N_EXPERT_GROUPS
TOPK_GROUPS = 4
TOP_K = 8
D_EXPERT = 256
D_SHARED = 256
ROUTED_SCALE = 2.5
RMS_EPS = 1e-6
NEG_INF = -1e30
ATTN_W = N_HEADS * HEAD_DIM
KV_W = N_KV_HEADS * HEAD_DIM
IN_COLS = D_SSM + ATTN_W + 2 * KV_W + 2 * D_MODEL
IN_SPLITS = (0, D_SSM, D_SSM + ATTN_W, D_SSM + ATTN_W + KV_W, D_SSM + ATTN_W + 2 * KV_W,
             D_SSM + ATTN_W + 2 * KV_W + D_MODEL, IN_COLS)

LANES = 128
SSM_L = 16
SSM_OCT_G = LANES // SSM_GROUP_CH
SSM_OCTS = SSM_GROUPS // SSM_OCT_G
SSM_FLAT = SSM_L * LANES
SSM_SW = SSM_OCT_G * SSM_STATE
SSM_POW_ROWS = 16
VMEM_LIMIT = 56 * 1024 * 1024
SUBLANES = 8
SLAB_ROWS = D_MODEL // LANES
MOE_BS = 128
MOE_STRIDE = MOE_BS + SUBLANES
MOE_RMW_UNROLL = 4
MOE_WSLOTS = 4
MOE_WORD_BITS = 16


def _cparams(*sem):
    return pltpu.CompilerParams(dimension_semantics=sem, vmem_limit_bytes=VMEM_LIMIT)


def _dot(a, b):
    return jnp.dot(a, b, preferred_element_type=F32)


def _dot_nt(a, b):
    return lax.dot_general(a, b, (((1,), (1,)), ((), ())), preferred_element_type=F32)


def _split(x):
    hi = x.astype(BF16)
    lo = (x - hi.astype(F32)).astype(BF16)
    return hi, lo


def _rmsnorm(x, g):
    return x * lax.rsqrt(jnp.mean(x * x, axis=-1, keepdims=True) + RMS_EPS) * g


def _mod_kernel(c_ref, w_ref, b_ref, o_ref):
    cond = jax.nn.silu(c_ref[...])
    ch, cl = _split(cond)
    wh, wl = _split(w_ref[...])
    o_ref[...] = _dot(ch, wh) + (_dot(ch, wl) + _dot(cl, wh)) + b_ref[...]


def _mod_call(c_all, ada_w, ada_b):
    depth = ada_w.shape[0]
    rows = c_all.shape[0]
    nj = 6
    return pl.pallas_call(
        _mod_kernel,
        grid=(depth, nj),
        in_specs=[
            pl.BlockSpec((rows, D_MODEL), lambda l, j: (0, 0)),
            pl.BlockSpec((None, D_MODEL, D_MODEL), lambda l, j: (l, 0, j)),
            pl.BlockSpec((None, 1, D_MODEL), lambda l, j: (l, 0, j)),
        ],
        out_specs=pl.BlockSpec((None, rows, D_MODEL), lambda l, j: (l, 0, j)),
        out_shape=jax.ShapeDtypeStruct((depth, rows, nj * D_MODEL), F32),
        compiler_params=_cparams("parallel", "parallel"),
        name="mod",
    )(c_all, ada_w, ada_b.reshape(depth, 1, nj * D_MODEL))


def _rope2(t, cos, sin_signed, first_half):
    swapped = jnp.where(first_half, pltpu.roll(t, LANES - HEAD_DIM // 2, axis=1),
                        pltpu.roll(t, HEAD_DIM // 2, axis=1))
    return t * cos + swapped * sin_signed


def _inproj_kernel(x_ref, sh_ref, sc_ref, g_ref, w_ref, cos_ref, sin_ref,
                   u_ref, q_ref, k_ref, v_ref, gs_ref, ga_ref):
    h = _rmsnorm(x_ref[...], g_ref[...]) * (1.0 + sc_ref[...]) + sh_ref[...]
    hb = h.astype(BF16)

    def proj(i):
        return _dot(hb, w_ref[:, IN_SPLITS[i]:IN_SPLITS[i + 1]])

    u_ref[...] = proj(0)
    cos = cos_ref[...]
    sin = sin_ref[...]
    lane = lax.broadcasted_iota(jnp.int32, cos.shape, 1)
    first_half = (lane % HEAD_DIM) < (HEAD_DIM // 2)
    q = proj(1)
    for j in range(ATTN_W // LANES):
        sl = slice(j * LANES, (j + 1) * LANES)
        q_ref[:, sl] = (_rope2(q[:, sl], cos, sin, first_half) * (HEAD_DIM ** -0.5)).astype(BF16)
    k_ref[...] = _rope2(proj(2), cos, sin, first_half)
    v_ref[...] = proj(3)
    gs_ref[...] = proj(4)
    ga_ref[...] = proj(5)


def _inproj_call(x, sh1, sc1, norm_g, w_in_bf, cos_t, sin_t, tt):
    bx, tx, _ = x.shape
    nt = tx // tt
    per_tok = sh1.shape[1] != 1
    mod_spec = (pl.BlockSpec((None, tt, D_MODEL), lambda b, i: (b, i, 0)) if per_tok
                else pl.BlockSpec((None, 1, D_MODEL), lambda b, i: (b, 0, 0)))

    def tok_spec(w):
        return pl.BlockSpec((None, tt, w), lambda b, i: (b, i, 0))

    def tok_shape(w, dt):
        return jax.ShapeDtypeStruct((bx, tx, w), dt)

    return pl.pallas_call(
        _inproj_kernel,
        grid=(bx, nt),
        in_specs=[
            tok_spec(D_MODEL), mod_spec, mod_spec,
            pl.BlockSpec((1, D_MODEL), lambda b, i: (0, 0)),
            pl.BlockSpec((D_MODEL, IN_COLS), lambda b, i: (0, 0)),
            pl.BlockSpec((tt, LANES), lambda b, i: (i, 0)),
            pl.BlockSpec((tt, LANES), lambda b, i: (i, 0)),
        ],
        out_specs=[tok_spec(D_SSM), tok_spec(ATTN_W), tok_spec(KV_W), tok_spec(KV_W),
                   tok_spec(D_MODEL), tok_spec(D_MODEL)],
        out_shape=[tok_shape(D_SSM, F32), tok_shape(ATTN_W, BF16), tok_shape(KV_W, F32),
                   tok_shape(KV_W, F32), tok_shape(D_MODEL, F32), tok_shape(D_MODEL, F32)],
        compiler_params=_cparams("parallel", "parallel"),
        name="inproj",
    )(x, sh1, sc1, norm_g, w_in_bf, cos_t, sin_t)


def _ssm_kernel(u_ref, kt_ref, bd_ref, rv_ref, cd_ref, fw_ref, ap_ref, h0_ref, y_ref, s_ref,
                t_ref, bl_ref, cl_ref, *, nb, cb):
    nc = nb * cb
    sw = SSM_SW

    @pl.when(pl.program_id(1) == 0)
    def _():
        zero = jnp.zeros((LANES, LANES), BF16)
        for s in range(SSM_L):
            for t in range(SSM_L):
                t_ref[s * LANES:(s + 1) * LANES, t * LANES:(t + 1) * LANES] = kt_ref[t - s] if t >= s else zero
        bre = bd_ref[:, :sw]
        bim = bd_ref[:, sw:]
        for s in range(SSM_L):
            rre = rv_ref[s:s + 1, :sw]
            rim = rv_ref[s:s + 1, sw:]
            bl_ref[s * LANES:(s + 1) * LANES, :sw] = (bre * rre - bim * rim).astype(BF16)
            bl_ref[s * LANES:(s + 1) * LANES, sw:] = (bre * rim + bim * rre).astype(BF16)
        cre = cd_ref[:sw, :]
        cim = cd_ref[sw:, :]
        for t in range(SSM_L):
            fre = fw_ref[:sw, t:t + 1]
            fim = fw_ref[sw:, t:t + 1]
            cl_ref[:sw, t * LANES:(t + 1) * LANES] = (cre * fre - cim * fim).astype(BF16)
            cl_ref[sw:, t * LANES:(t + 1) * LANES] = (-(cre * fim + cim * fre)).astype(BF16)

    uf = jnp.concatenate([u_ref[pl.ds(s, nc, stride=SSM_L), :].astype(BF16) for s in range(SSM_L)], axis=1)
    y = _dot(uf, t_ref[...])
    v = _dot(uf, bl_ref[...])
    xre = v[:, :sw]
    xim = v[:, sw:]

    row = lax.broadcasted_iota(jnp.int32, (nc, sw), 0)
    cidx = row & (cb - 1)
    bidx = row // cb
    h0 = h0_ref[...]
    h0re = jnp.zeros((nc, sw), F32)
    h0im = jnp.zeros((nc, sw), F32)
    for b in range(nb):
        h0re = jnp.where(bidx == b, h0[b:b + 1, :sw], h0re)
        h0im = jnp.where(bidx == b, h0[b:b + 1, sw:], h0im)
    first = cidx == 0
    are = ap_ref[0:1, :sw]
    aim = ap_ref[0:1, sw:]
    xre = xre + jnp.where(first, are * h0re - aim * h0im, 0.0)
    xim = xim + jnp.where(first, are * h0im + aim * h0re, 0.0)

    d = 1
    k = 0
    while d < cb:
        are = ap_ref[k:k + 1, :sw]
        aim = ap_ref[k:k + 1, sw:]
        keep = cidx >= d
        sre = jnp.where(keep, pltpu.roll(xre, d, axis=0), 0.0)
        sim = jnp.where(keep, pltpu.roll(xim, d, axis=0), 0.0)
        xre, xim = xre + (are * sre - aim * sim), xim + (are * sim + aim * sre)
        d *= 2
        k += 1

    for b in range(nb):
        r = b * cb + cb - 1
        s_ref[b:b + 1, :sw] = xre[r:r + 1, :]
        s_ref[b:b + 1, sw:] = xim[r:r + 1, :]

    pre = jnp.where(first, h0re, pltpu.roll(xre, 1, axis=0))
    pim = jnp.where(first, h0im, pltpu.roll(xim, 1, axis=0))
    y = y + _dot(pre.astype(BF16), cl_ref[:sw, :]) + _dot(pim.astype(BF16), cl_ref[sw:, :])
    for t in range(SSM_L):
        y_ref[pl.ds(t, nc, stride=SSM_L), :] = y[:, t * LANES:(t + 1) * LANES]


def _ssm_call(u, mats, h0o, nb, cb):
    kt, bd, rv, cd, fw, ap = mats
    bx, tx, _ = u.shape

    def const_spec(a):
        return pl.BlockSpec((None,) + a.shape[1:], lambda o, b: (o,) + (0,) * (a.ndim - 1))

    tok_spec = pl.BlockSpec((None, tx, LANES), lambda o, b: (b, 0, o))
    st_spec = pl.BlockSpec((None, None, nb, 2 * SSM_SW), lambda o, b: (o, b, 0, 0))
    return pl.pallas_call(
        functools.partial(_ssm_kernel, nb=nb, cb=cb),
        grid=(SSM_OCTS, bx),
        in_specs=[tok_spec] + [const_spec(a) for a in (kt, bd, rv, cd, fw, ap)] + [st_spec],
        out_specs=[tok_spec, st_spec],
        out_shape=[jax.ShapeDtypeStruct((bx, tx, D_SSM), F32),
                   jax.ShapeDtypeStruct((SSM_OCTS, bx, nb, 2 * SSM_SW), F32)],
        scratch_shapes=[pltpu.VMEM((SSM_FLAT, SSM_FLAT), BF16), pltpu.VMEM((SSM_FLAT, 2 * SSM_SW), BF16),
                        pltpu.VMEM((2 * SSM_SW, SSM_FLAT), BF16)],
        compiler_params=_cparams("arbitrary", "arbitrary"),
        name="ssm",
    )(u, kt, bd, rv, cd, fw, ap, h0o)


def _oct_lanes(a):
    lead = a.shape[:-2]
    a = a.reshape(*lead, SSM_OCTS, SSM_SW)
    return jnp.moveaxis(a, -2, 0)


def _ssm_mats(a_re, a_im, log_dt, b_re, b_im, c_re, c_im):
    dt = jnp.exp(log_dt)[:, None]
    lre = a_re * dt
    lim = a_im * dt
    mag = jnp.exp(lre)
    ab_re = mag * jnp.cos(lim)
    ab_im = mag * jnp.sin(lim)
    den = a_re * a_re + a_im * a_im
    n_re = ab_re - 1.0
    f_re = (n_re * a_re + ab_im * a_im) / den
    f_im = (ab_im * a_re - n_re * a_im) / den
    bb_re = f_re[..., None] * b_re - f_im[..., None] * b_im
    bb_im = f_re[..., None] * b_im + f_im[..., None] * b_re

    def lam_pow(tau):
        tau = tau[:, None, None]
        m = jnp.exp(tau * lre)
        return m * jnp.cos(tau * lim), m * jnp.sin(tau * lim)

    pw_re, pw_im = lam_pow(jnp.arange(SSM_L + 1, dtype=F32))
    cp_re = c_re[None] * pw_re[:, :, None, :] - c_im[None] * pw_im[:, :, None, :]
    cp_im = c_re[None] * pw_im[:, :, None, :] + c_im[None] * pw_re[:, :, None, :]
    kern = jnp.sum(cp_re[..., None] * bb_re[None, :, None] - cp_im[..., None] * bb_im[None, :, None],
                   axis=3)
    eye = jnp.eye(SSM_OCT_G, dtype=F32)
    og = (SSM_OCTS, SSM_OCT_G)

    def slab_diag(m):
        lead = m.shape[:-3]
        a, b = m.shape[-2:]
        m = m.reshape(*lead, *og, a, b)
        m = m[..., :, None, :] * eye[:, None, :, None]
        return m.reshape(*lead, SSM_OCTS, SSM_OCT_G * a, SSM_OCT_G * b)

    kt = slab_diag(kern.transpose(0, 1, 3, 2)).astype(BF16).transpose(1, 0, 2, 3)
    bd = jnp.concatenate([slab_diag(bb_re.transpose(0, 2, 1)), slab_diag(bb_im.transpose(0, 2, 1))], axis=-1)
    rev = SSM_L - 1 - jnp.arange(SSM_L)
    rv = jnp.concatenate([_oct_lanes(pw_re[rev]), _oct_lanes(pw_im[rev])], axis=-1)
    cd = jnp.concatenate([slab_diag(c_re.transpose(0, 2, 1)), slab_diag(c_im.transpose(0, 2, 1))], axis=1)
    fw = jnp.concatenate([_oct_lanes(pw_re[1:]), _oct_lanes(pw_im[1:])], axis=-1).transpose(0, 2, 1)
    fw = jnp.pad(fw, ((0, 0), (0, 0), (0, LANES - SSM_L)))

    steps = float(SSM_L) * (2.0 ** jnp.arange(SSM_POW_ROWS, dtype=F32))
    ap_re, ap_im = lam_pow(steps)
    ap = jnp.concatenate([_oct_lanes(ap_re), _oct_lanes(ap_im)], axis=-1)
    return kt, bd, rv, cd, fw, ap


def _ssm_branch(u, mats, h0, nb, cb):
    bx = u.shape[0]
    h0o = jnp.concatenate([_oct_lanes(h0[..., 0]), _oct_lanes(h0[..., 1])], axis=-1)
    y, s = _ssm_call(u, mats, h0o.reshape(SSM_OCTS, bx, nb, 2 * SSM_SW), nb, cb)
    s = s.reshape(SSM_OCTS, bx * nb, 2, SSM_OCT_G, SSM_STATE).transpose(1, 0, 3, 4, 2)
    return y, s.reshape(bx * nb, SSM_GROUPS, SSM_STATE, 2)


def _attn_kernel(sink_ref, q_ref, ka_ref, kb_ref, va_ref, vb_ref, o_ref, *, banded):
    rows = q_ref.shape[0]
    nk = 2 * LANES
    if banded:
        kfull = jnp.concatenate([ka_ref[...], kb_ref[...]], axis=0)
        vfull = jnp.concatenate([va_ref[...], vb_ref[...]], axis=0)
    else:
        pad = jnp.zeros((nk - ka_ref.shape[0] - kb_ref.shape[0], KV_W), F32)
        kfull = jnp.concatenate([ka_ref[...], kb_ref[...], pad], axis=0)
        vfull = jnp.concatenate([va_ref[...], vb_ref[...], pad], axis=0)

    r_i = lax.broadcasted_iota(jnp.int32, (rows, nk), 0)
    c_i = lax.broadcasted_iota(jnp.int32, (rows, nk), 1)
    if banded:
        lo = (r_i // CHUNK) * CHUNK
        first_frame = jnp.where(pl.program_id(1) > 0, 0, WINDOW)
        valid = (c_i >= lo) & (c_i < lo + WINDOW + CHUNK) & (c_i >= first_frame)
    else:
        valid = c_i < (ka_ref.shape[0] + kb_ref.shape[0])

    lane = lax.broadcasted_iota(jnp.int32, (nk, KV_W), 1)
    low = lane < HEAD_DIM
    kroll = pltpu.roll(kfull, HEAD_DIM, axis=1)
    vroll = pltpu.roll(vfull, HEAD_DIM, axis=1)

    for g in range(N_KV_HEADS):
        k_lo, k_hi = (kfull, kroll) if g == 0 else (kroll, kfull)
        v_lo, v_hi = (vfull, vroll) if g == 0 else (vroll, vfull)
        kpad = (jnp.where(low, k_lo, 0.0).astype(BF16), jnp.where(low, 0.0, k_hi).astype(BF16))
        vpad = (jnp.where(low, v_lo, 0.0).astype(BF16), jnp.where(low, 0.0, v_hi).astype(BF16))
        for pp in range(2):
            slab = 2 * g + pp
            qp = q_ref[:, slab * LANES:(slab + 1) * LANES]
            acc = None
            for hh in range(2):
                sk = sink_ref[2 * slab + hh]
                s = jnp.where(valid, _dot_nt(qp, kpad[hh]), NEG_INF)
                m = jnp.maximum(jnp.max(s, axis=-1, keepdims=True), sk)
                p = jnp.exp(s - m)
                den = jnp.sum(p, axis=-1, keepdims=True) + jnp.exp(sk - m)
                o = _dot(p.astype(BF16), vpad[hh]) / den
                acc = o if acc is None else acc + o
            o_ref[:, slab * LANES:(slab + 1) * LANES] = acc.astype(BF16)


def _attn_call(sink, q, ka, kb, va, vb, banded):
    bx, tx, _ = q.shape
    if banded:
        rows = 2 * CHUNK
        nt = tx // rows
        grid = (bx, nt)
        q_spec = pl.BlockSpec((None, rows, ATTN_W), lambda b, i: (b, i, 0))
        prev = pl.BlockSpec((None, WINDOW, KV_W), lambda b, i: (b, jnp.maximum(i - 1, 0), 0))
        cur = pl.BlockSpec((None, rows, KV_W), lambda b, i: (b, i, 0))
    else:
        rows = tx
        grid = (bx, 1)
        q_spec = pl.BlockSpec((None, rows, ATTN_W), lambda b, i: (b, 0, 0))
        prev = pl.BlockSpec((None, ka.shape[1], KV_W), lambda b, i: (b, 0, 0))
        cur = pl.BlockSpec((None, rows, KV_W), lambda b, i: (b, 0, 0))
    return pl.pallas_call(
        functools.partial(_attn_kernel, banded=banded),
        grid=grid,
        in_specs=[pl.BlockSpec(memory_space=pltpu.SMEM), q_spec, prev, cur, prev, cur],
        out_specs=q_spec,
        out_shape=jax.ShapeDtypeStruct((bx, tx, ATTN_W), BF16),
        compiler_params=_cparams("parallel", "parallel"),
        name="attn",
    )(sink, q, ka, kb, va, vb)


def _route(logits_t, bias_col, topi_ref, topw_ref):
    tt = logits_t.shape[1]
    scores = jax.nn.sigmoid(logits_t)
    biased = scores + bias_col
    sub = lax.broadcasted_iota(jnp.int32, (EXPERTS_PER_GROUP, tt), 0).astype(F32)
    ninf = float('-inf')
    blocks = [biased[EXPERTS_PER_GROUP * g:EXPERTS_PER_GROUP * (g + 1), :] for g in range(N_EXPERT_GROUPS)]

    gscore = jnp.zeros((N_EXPERT_GROUPS, tt), F32)
    for g in range(N_EXPERT_GROUPS):
        blk = blocks[g]
        m1 = jnp.max(blk, axis=0, keepdims=True)
        i1 = jnp.min(jnp.where(blk == m1, sub, float(EXPERTS_PER_GROUP)), axis=0, keepdims=True)
        m2 = jnp.max(jnp.where(sub == i1, ninf, blk), axis=0, keepdims=True)
        gscore = jnp.where(sub == float(g), jnp.broadcast_to(m1 + m2, gscore.shape), gscore)

    grank = jnp.zeros((N_EXPERT_GROUPS, tt), F32)
    for j in range(N_EXPERT_GROUPS):
        rj = jnp.broadcast_to(gscore[j:j + 1, :], gscore.shape)
        beats = (rj > gscore) | ((rj == gscore) & (sub > float(j)))
        grank = grank + jnp.where(beats, 1.0, 0.0)
    gsel = jnp.where(grank < float(TOPK_GROUPS), 1.0, 0.0)

    masked = []
    for g in range(N_EXPERT_GROUPS):
        on = jnp.broadcast_to(gsel[g:g + 1, :], blocks[g].shape) > 0.5
        masked.append(jnp.where(on, blocks[g], NEG_INF))

    ranks = [jnp.zeros((EXPERTS_PER_GROUP, tt), F32) for _ in range(N_EXPERT_GROUPS)]
    for jb in range(N_EXPERT_GROUPS):
        for jj in range(EXPERTS_PER_GROUP):
            rj = jnp.broadcast_to(masked[jb][jj:jj + 1, :], (EXPERTS_PER_GROUP, tt))
            for ib in range(N_EXPERT_GROUPS):
                mi = masked[ib]
                if ib < jb:
                    beats = rj > mi
                elif ib > jb:
                    beats = rj >= mi
                else:
                    beats = (rj > mi) | ((rj == mi) & (sub > float(jj)))
                ranks[ib] = ranks[ib] + jnp.where(beats, 1.0, 0.0)

    topi = jnp.zeros((TOP_K, tt), F32)
    topw = jnp.zeros((TOP_K, tt), F32)
    for r in range(TOP_K):
        ai = jnp.zeros((EXPERTS_PER_GROUP, tt), F32)
        aw = jnp.zeros((EXPERTS_PER_GROUP, tt), F32)
        for g in range(N_EXPERT_GROUPS):
            hit = ranks[g] == float(r)
            ai = ai + jnp.where(hit, sub + float(EXPERTS_PER_GROUP * g), 0.0)
            aw = aw + jnp.where(hit, scores[EXPERTS_PER_GROUP * g:EXPERTS_PER_GROUP * (g + 1), :], 0.0)
        ir = jnp.sum(ai, axis=0, keepdims=True)
        wr = jnp.sum(aw, axis=0, keepdims=True)
        topi = jnp.where(sub == float(r), jnp.broadcast_to(ir, topi.shape), topi)
        topw = jnp.where(sub == float(r), jnp.broadcast_to(wr, topw.shape), topw)
    den = jnp.sum(topw, axis=0, keepdims=True)
    topi_ref[...] = topi.astype(jnp.int32)
    topw_ref[...] = topw / den * ROUTED_SCALE


def _post_kernel(y_ref, u_ref, a_ref, gs_ref, ga_ref, x_ref, g1_ref, sh2_ref, sc2_ref, g2_ref,
                 d_ref, wglu_ref, bglu_ref, wbs_ref, wba_ref, wout_ref, nf_ref, rwt_ref, rb_ref,
                 swg_ref, swu_ref, swd_ref, xp_ref, h2s_ref, topi_ref, topw_ref):
    ys = y_ref[...] + d_ref[...] * u_ref[...]
    z = jax.nn.gelu(ys)
    y2 = z * jax.nn.sigmoid(_dot(z.astype(BF16), wglu_ref[...]) + bglu_ref[...])
    merged = (jax.nn.sigmoid(gs_ref[...]) * _dot(y2.astype(BF16), wbs_ref[...])
              + jax.nn.sigmoid(ga_ref[...]) * _dot(a_ref[...], wba_ref[...]))
    x1 = x_ref[...] + g1_ref[...] * _dot(merged.astype(BF16), wout_ref[...])

    h2 = _rmsnorm(x1, nf_ref[...]) * (1.0 + sc2_ref[...]) + sh2_ref[...]
    tt = h2.shape[0]
    for j in range(SLAB_ROWS):
        h2s_ref[pl.ds(j, tt, stride=SLAB_ROWS), :] = h2[:, j * LANES:(j + 1) * LANES]
    hh, hl = _split(h2)
    rh, rl = _split(rwt_ref[...])
    logits_t = _dot_nt(rh, hh) + (_dot_nt(rh, hl) + _dot_nt(rl, hh))
    _route(logits_t, rb_ref[...], topi_ref, topw_ref)

    shared = _dot((jax.nn.silu(_dot(hh, swg_ref[...])) * _dot(hh, swu_ref[...])).astype(BF16), swd_ref[...])
    xp_ref[...] = x1 + g2_ref[...] * shared


def _post_call(y, u, attn, gs, ga, x, mods, lw, tt):
    bx, tx, _ = x.shape
    nt = tx // tt
    per_tok = mods[0].shape[1] != 1
    mod_spec = (pl.BlockSpec((None, tt, D_MODEL), lambda b, i: (b, i, 0)) if per_tok
                else pl.BlockSpec((None, 1, D_MODEL), lambda b, i: (b, 0, 0)))

    def tok_spec(w):
        return pl.BlockSpec((None, tt, w), lambda b, i: (b, i, 0))

    def full(a):
        return pl.BlockSpec(a.shape, lambda b, i: (0,) * a.ndim)

    weights = [lw['ssm_d'], lw['w_glu'], lw['b_glu'], lw['w_bs'], lw['w_ba'], lw['w_out'], lw['norm_ffn_g'],
               lw['router_wt'], lw['router_bias'], lw['sh_wg'], lw['sh_wu'], lw['sh_wd']]
    return pl.pallas_call(
        _post_kernel,
        grid=(bx, nt),
        in_specs=[tok_spec(D_SSM), tok_spec(D_SSM), tok_spec(ATTN_W), tok_spec(D_MODEL), tok_spec(D_MODEL),
                  tok_spec(D_MODEL), mod_spec, mod_spec, mod_spec, mod_spec] + [full(w) for w in weights],
        out_specs=[tok_spec(D_MODEL),
                   pl.BlockSpec((tt * SLAB_ROWS, LANES), lambda b, i: (b * nt + i, 0)),
                   pl.BlockSpec((TOP_K, tt), lambda b, i: (0, b * nt + i)),
                   pl.BlockSpec((TOP_K, tt), lambda b, i: (0, b * nt + i))],
        out_shape=[jax.ShapeDtypeStruct((bx, tx, D_MODEL), F32),
                   jax.ShapeDtypeStruct((bx * tx * SLAB_ROWS, LANES), F32),
                   jax.ShapeDtypeStruct((TOP_K, bx * tx), jnp.int32),
                   jax.ShapeDtypeStruct((TOP_K, bx * tx), F32)],
        compiler_params=_cparams("parallel", "parallel"),
        name="post",
    )(y, u, attn, gs, ga, x, *mods, *weights)


def _moe_kernel(ptab_ref, ktab_ref, elist_ref, meta_ref, hs_ref, word_hbm, gate_hbm, wg_hbm, wu_hbm, wd_hbm, o_ref,
                word_s, gate_s, xt0, xt1, yt0, yt1, wg_buf, wu_buf, wd_buf, lsem, wsem, *, tm, ptab_w, layer):
    i = pl.program_id(0)
    nslot = word_s.shape[0]
    ngate = gate_s.shape[0]
    acc_ref = o_ref
    slab_mask = -SLAB_ROWS
    gather_mask = (tm * SLAB_ROWS - 1) & slab_mask
    tab0 = i * ptab_w + 1
    npairs = meta_ref[2 * i]
    nrank = meta_ref[2 * i + 1]

    def list_copies():
        return (pltpu.make_async_copy(word_hbm.at[pl.ds(pl.multiple_of(i * nslot, LANES), nslot)],
                                      word_s, lsem.at[0]),
                pltpu.make_async_copy(gate_hbm.at[pl.ds(pl.multiple_of(i * ngate, LANES), ngate)],
                                      gate_s, lsem.at[1]))

    def weight_copies(k):
        ex = elist_ref[i * N_EXPERTS + k]
        slot = k % MOE_WSLOTS
        return (pltpu.make_async_copy(wg_hbm.at[layer, ex], wg_buf.at[slot], wsem.at[slot, 0]),
                pltpu.make_async_copy(wu_hbm.at[layer, ex], wu_buf.at[slot], wsem.at[slot, 1]),
                pltpu.make_async_copy(wd_hbm.at[layer, ex], wd_buf.at[slot], wsem.at[slot, 2]))

    def gather(base, xt_ref):
        for r in range(MOE_BS):
            row = pl.multiple_of(word_s[base + r] & gather_mask, SLAB_ROWS)
            xt_ref[pl.ds(r, SLAB_ROWS, stride=MOE_STRIDE), :] = hs_ref[pl.ds(row, SLAB_ROWS), :]

    def expert(xt_ref, yt_ref, k):
        slot = k % MOE_WSLOTS
        x = jnp.concatenate([xt_ref[j * MOE_STRIDE:j * MOE_STRIDE + MOE_BS, :] for j in range(SLAB_ROWS)],
                            axis=1).astype(BF16)
        act = jax.nn.silu(_dot(x, wg_buf[slot])) * _dot(x, wu_buf[slot])
        y = _dot(act.astype(BF16), wd_buf[slot])
        for j in range(SLAB_ROWS):
            yt_ref[j * MOE_STRIDE:j * MOE_STRIDE + MOE_BS, :] = y[:, j * LANES:(j + 1) * LANES]

    def scatter(base, yt_ref):
        for r0 in range(0, MOE_BS, MOE_RMW_UNROLL):
            upd = []
            for r in range(r0, r0 + MOE_RMW_UNROLL):
                word = word_s[base + r]
                row = pl.multiple_of(word & slab_mask, SLAB_ROWS)
                upd.append((row, acc_ref[pl.ds(row, SLAB_ROWS), :]
                            + gate_s[word] * yt_ref[pl.ds(r, SLAB_ROWS, stride=MOE_STRIDE), :]))
            for row, val in upd:
                acc_ref[pl.ds(row, SLAB_ROWS), :] = val

    for c in list_copies():
        c.start()
    ahead = MOE_WSLOTS - 2
    for r in range(ahead + 1):
        @pl.when(r < nrank)
        def _():
            for c in weight_copies(r):
                c.start()

    acc_ref[...] = jnp.zeros_like(acc_ref)
    yt1[...] = jnp.zeros_like(yt1)
    for c in list_copies():
        c.wait()
    gather(ptab_ref[tab0], xt0)

    def pair(j, carry):
        waited, started = carry
        t = tab0 + 2 * j
        k0 = ktab_ref[t]
        k1 = ktab_ref[t + 1]

        target = jnp.minimum(k1 + ahead, nrank - 1)
        for d in (1, 2):
            @pl.when(started + d <= target)
            def _():
                for c in weight_copies(started + d):
                    c.start()

        for d in (1, 2):
            @pl.when(waited + d <= k1)
            def _():
                for c in weight_copies(waited + d):
                    c.wait()

        gather(ptab_ref[t + 1], xt1)
        expert(xt0, yt0, k0)
        scatter(ptab_ref[t - 1], yt1)
        gather(ptab_ref[t + 2], xt0)
        expert(xt1, yt1, k1)
        scatter(ptab_ref[t], yt0)
        return jnp.maximum(waited, k1), jnp.maximum(started, target)

    lax.fori_loop(0, npairs, pair, (jnp.int32(-1), jnp.minimum(jnp.int32(ahead), nrank - 1)))
    scatter(ptab_ref[tab0 + 2 * npairs - 1], yt1)


def _moe_call(h2s, topi, topw, wg, wu, wd, layer, tm):
    n = h2s.shape[0] // SLAB_ROWS
    nt = n // tm
    nseg = nt * N_EXPERTS
    nreal = tm * TOP_K
    nslot = nreal + (N_EXPERTS + 1) * MOE_BS
    null_base = nslot - MOE_BS
    ptab_w = nreal // MOE_BS + N_EXPERTS + 4

    i32 = jnp.int32
    experts = jnp.arange(N_EXPERTS, dtype=i32)
    counts = jnp.sum((topi.reshape(TOP_K, nt, tm)[..., None] == experts).astype(i32), axis=(0, 2))
    nblk = (counts + MOE_BS - 1) // MOE_BS
    first = jnp.cumsum(counts + MOE_BS, axis=1) - (counts + MOE_BS)
    pend = jnp.cumsum(nblk, axis=1)
    pstart = pend - nblk
    nonempty = (nblk > 0).astype(i32)
    krank = jnp.cumsum(nonempty, axis=1) - 1
    nrank = jnp.sum(nonempty, axis=1)
    elist = jnp.sum(jnp.where((nonempty[:, None, :] == 1) & (krank[:, None, :] == experts[None, :, None]),
                              experts[None, None, :], 0), axis=-1).astype(i32)
    pos = jnp.arange(ptab_w, dtype=i32) - 1
    e_of = jnp.sum((pend[:, None, :] <= pos[None, :, None]).astype(i32), axis=-1)
    e_cl = jnp.minimum(e_of, N_EXPERTS - 1)
    q = pos[None, :] - jnp.take_along_axis(pstart, e_cl, axis=1)
    real_blk = (pos[None, :] >= 0) & (e_of < N_EXPERTS)
    ptab = jnp.where(real_blk, jnp.take_along_axis(first, e_cl, axis=1) + q * MOE_BS, null_base).astype(i32)
    ktab = jnp.where(real_blk, jnp.take_along_axis(krank, e_cl, axis=1), nrank[:, None] - 1).astype(i32)
    meta = jnp.stack([(pend[:, -1] + 1) // 2, nrank], axis=1).astype(i32)

    assert TOP_K == SLAB_ROWS and nreal < (1 << MOE_WORD_BITS) and 2 * nseg + 2 < (1 << (31 - MOE_WORD_BITS))
    slot = jnp.arange(n * TOP_K, dtype=i32)
    key = 2 * ((slot // nreal) * N_EXPERTS + topi.T.reshape(n * TOP_K))
    pad_key = jnp.concatenate([jnp.repeat(2 * jnp.arange(nseg, dtype=i32) + 1, MOE_BS),
                               jnp.repeat(2 * (jnp.arange(nt, dtype=i32) * N_EXPERTS + N_EXPERTS - 1) + 1, MOE_BS)])
    packed = jnp.concatenate([(key << MOE_WORD_BITS) | (slot % nreal), (pad_key << MOE_WORD_BITS) | nreal])
    word_s = lax.sort(packed, is_stable=False) & ((1 << MOE_WORD_BITS) - 1)
    ngate = nreal + LANES
    gate_s = jnp.concatenate([topw.T.reshape(nt, nreal), jnp.zeros((nt, LANES), F32)], axis=1).reshape(nt * ngate)

    stage = pltpu.VMEM((SLAB_ROWS * MOE_STRIDE, LANES), F32)
    hbm = pl.BlockSpec(memory_space=pl.ANY)
    grid_spec = pltpu.PrefetchScalarGridSpec(
        num_scalar_prefetch=4,
        grid=(nt,),
        in_specs=[
            pl.BlockSpec((tm * SLAB_ROWS, LANES), lambda i, *_: (i, 0), pipeline_mode=pl.Buffered(1)),
            hbm, hbm, hbm, hbm, hbm,
        ],
        out_specs=pl.BlockSpec((None, (tm + 1) * SLAB_ROWS, LANES), lambda i, *_: (i, 0, 0),
                               pipeline_mode=pl.Buffered(1)),
        scratch_shapes=[
            pltpu.SMEM((nslot,), i32),
            pltpu.SMEM((ngate,), F32),
            stage, stage, stage, stage,
            pltpu.VMEM((MOE_WSLOTS, D_MODEL, D_EXPERT), BF16),
            pltpu.VMEM((MOE_WSLOTS, D_MODEL, D_EXPERT), BF16),
            pltpu.VMEM((MOE_WSLOTS, D_EXPERT, D_MODEL), BF16),
            pltpu.SemaphoreType.DMA((2,)),
            pltpu.SemaphoreType.DMA((MOE_WSLOTS, 3)),
        ],
    )
    return pl.pallas_call(
        functools.partial(_moe_kernel, tm=tm, ptab_w=ptab_w, layer=layer),
        grid_spec=grid_spec,
        out_shape=jax.ShapeDtypeStruct((nt, (tm + 1) * SLAB_ROWS, LANES), F32),
        compiler_params=_cparams("arbitrary"),
        name="moe",
    )(ptab.reshape(nt * ptab_w), ktab.reshape(nt * ptab_w), elist.reshape(nseg), meta.reshape(nt * 2),
      h2s, word_s, gate_s, wg, wu, wd)


def _combine_kernel(r_ref, xp_ref, g2_ref, fg_ref, o_ref, *, final):
    tt = xp_ref.shape[0]
    routed = jnp.concatenate([r_ref[pl.ds(j, tt, stride=SLAB_ROWS), :] for j in range(SLAB_ROWS)], axis=1)
    x2 = xp_ref[...] + g2_ref[...] * routed
    if final:
        x2 = _rmsnorm(x2, fg_ref[...])
    o_ref[...] = x2


def _combine_call(routed_s, xp, g2, final_g, tt, final):
    bx, tx, _ = xp.shape
    nt = tx // tt
    per_tok = g2.shape[1] != 1
    g2_spec = (pl.BlockSpec((None, tt, D_MODEL), lambda b, i: (b, i, 0)) if per_tok
               else pl.BlockSpec((None, 1, D_MODEL), lambda b, i: (b, 0, 0)))
    tok_spec = pl.BlockSpec((None, tt, D_MODEL), lambda b, i: (b, i, 0))
    per_tile = (routed_s.shape[1] // SLAB_ROWS - 1) // tt
    routed_spec = pl.BlockSpec((None, tt * SLAB_ROWS, LANES),
                               lambda b, i: ((b * nt + i) // per_tile, (b * nt + i) % per_tile, 0))
    return pl.pallas_call(
        functools.partial(_combine_kernel, final=final),
        grid=(bx, nt),
        in_specs=[routed_spec, tok_spec, g2_spec,
                  pl.BlockSpec((1, D_MODEL), lambda b, i: (0, 0))],
        out_specs=tok_spec,
        out_shape=jax.ShapeDtypeStruct((bx, tx, D_MODEL), F32),
        compiler_params=_cparams("parallel", "parallel"),
        name="combine",
    )(routed_s, xp, g2, final_g)


def _rope_tables(pos):
    half = HEAD_DIM // 2
    inv_freq = ROPE_THETA ** (-jnp.arange(half, dtype=F32) / half)
    ang = pos.astype(F32)[:, None] * inv_freq[None, :]
    cos = jnp.cos(ang)
    sin = jnp.sin(ang)
    return jnp.tile(cos, (1, 4)), jnp.tile(jnp.concatenate([-sin, sin], axis=1), (1, 2))


def _trunk(x, mods, pos, past_k, past_v, ssm_h0, layers, experts, final_g, tt, tm, tok_batches):
    bx, tx, _ = x.shape
    b, t = tok_batches
    cos_t, sin_t = _rope_tables(pos)
    new_k, new_v, new_h = [], [], []
    for l, lw in enumerate(layers):
        sh1, sc1, g1, sh2, sc2, g2 = mods[l]
        u, q, k, v, gs, ga = _inproj_call(x, sh1, sc1, lw['norm_attn_g'], lw['w_in'], cos_t, sin_t, tt)
        ks = k.reshape(b, t, KV_W)
        vs = v.reshape(b, t, KV_W)
        qs = q.reshape(b, t, ATTN_W)
        if past_k is None:
            attn = _attn_call(lw['sink'], qs, ks, ks, vs, vs, True)
            new_k.append(ks[:, -WINDOW:].reshape(b, WINDOW, N_KV_HEADS, HEAD_DIM))
            new_v.append(vs[:, -WINDOW:].reshape(b, WINDOW, N_KV_HEADS, HEAD_DIM))
            h0 = jnp.zeros((b, SSM_GROUPS, SSM_STATE, 2), F32)
        else:
            pk = past_k[l].reshape(b, -1, KV_W)
            pv = past_v[l].reshape(b, -1, KV_W)
            attn = _attn_call(lw['sink'], qs, pk, ks, pv, vs, False)
            new_k.append(ks.reshape(b, t, N_KV_HEADS, HEAD_DIM))
            new_v.append(vs.reshape(b, t, N_KV_HEADS, HEAD_DIM))
            h0 = ssm_h0[l]
        y, h_last = _ssm_branch(u, lw['ssm_mats'], h0, b // bx, t // SSM_L)
        new_h.append(h_last)
        xp, h2s, topi, topw = _post_call(y, u, attn.reshape(bx, tx, ATTN_W), gs, ga, x,
                                         (g1, sh2, sc2, g2), lw, tt)
        routed_s = _moe_call(h2s, topi, topw, *experts, l, tm)
        x = _combine_call(routed_s, xp, g2, final_g, tt, l == len(layers) - 1)
    return x, jnp.stack(new_k), jnp.stack(new_v), jnp.stack(new_h)


def kernel(x_prompt, x_sample, cache_k, cache_v, state_ssm, c_prompt, c_sample, ada_w, ada_b, norm_attn_g,
           norm_ffn_g, w_in, ssm_a_re, ssm_a_im, ssm_log_dt, ssm_b_re, ssm_b_im, ssm_c_re, ssm_c_im, ssm_d,
           ssm_w_glu, ssm_b_glu, attn_sink, w_branch_ssm, w_branch_attn, w_out, router_w, router_bias,
           expert_w_gate, expert_w_up, expert_w_down, shared_w_gate, shared_w_up, shared_w_down, final_g):
    depth = ada_w.shape[0]
    bp, tp, _ = x_prompt.shape
    bs, ts, _ = x_sample.shape

    layers = []
    for l in range(depth):
        layers.append({
            'norm_attn_g': norm_attn_g[l][None], 'norm_ffn_g': norm_ffn_g[l][None],
            'w_in': w_in[l].astype(BF16), 'sink': attn_sink[l],
            'ssm_mats': _ssm_mats(ssm_a_re[l], ssm_a_im[l], ssm_log_dt[l], ssm_b_re[l], ssm_b_im[l],
                                  ssm_c_re[l], ssm_c_im[l]),
            'ssm_d': ssm_d[l][None], 'w_glu': ssm_w_glu[l].astype(BF16), 'b_glu': ssm_b_glu[l][None],
            'w_bs': w_branch_ssm[l].astype(BF16), 'w_ba': w_branch_attn[l].astype(BF16),
            'w_out': w_out[l].astype(BF16), 'router_wt': router_w[l].T, 'router_bias': router_bias[l][:, None],
            'sh_wg': shared_w_gate[l].astype(BF16), 'sh_wu': shared_w_up[l].astype(BF16),
            'sh_wd': shared_w_down[l].astype(BF16),
        })
    experts = (expert_w_gate.astype(BF16), expert_w_up.astype(BF16), expert_w_down.astype(BF16))
    fg = final_g[None]

    rows = 16
    c_all = jnp.concatenate([c_prompt, c_sample, jnp.zeros((rows - bp - bs, D_MODEL), F32)], axis=0)
    mod = _mod_call(c_all, ada_w, ada_b).reshape(depth, rows, 6, D_MODEL)
    mods_p = [[mod[l, :bp, j][:, None, :] for j in range(6)] for l in range(depth)]
    mods_s = [[jnp.repeat(mod[l, bp:bp + bs, j], ts, axis=0)[None] for j in range(6)] for l in range(depth)]

    pos_p = jnp.arange(tp, dtype=jnp.int32)
    pos_s = jnp.tile(PAST_LEN + jnp.arange(ts, dtype=jnp.int32), bs)

    y_p, k_p, v_p, h_p = _trunk(x_prompt, mods_p, pos_p, None, None, None, layers, experts, fg,
                                tt=512, tm=min(4096, bp * tp), tok_batches=(bp, tp))
    y_s, k_s, v_s, h_s = _trunk(x_sample.reshape(1, bs * ts, D_MODEL), mods_s, pos_s, cache_k, cache_v,
                                state_ssm, layers, experts, fg, tt=bs * ts, tm=bs * ts, tok_batches=(bs, ts))
    return (y_p, y_s.reshape(bs, ts, D_MODEL), k_p, v_p, h_p, k_s, v_s, h_s)
```

```python
import functools

import jax
import jax.numpy as jnp
from jax import lax
from jax.experimental import pallas as pl
from jax.experimental.pallas import tpu as pltpu

F32 = jnp.float32
BF16 = jnp.bfloat16

D_MODEL = 1024
DEPTH = 2
PAST_LEN = 4096
CHUNK = 64
N_HEADS = 8
N_KV_HEADS = 2
HEAD_DIM = 64
WINDOW = 128
ROPE_THETA = 10000.0
D_SSM = 512
SSM_GROUP_CH = 16
SSM_GROUPS = D_SSM // SSM_GROUP_CH
SSM_STATE = 64
N_EXPERTS = 64
N_EXPERT_GROUPS = 8
EXPERTS_PER_GROUP = N_EXPERTS // N_EXPERT_GROUPS
TOPK_GROUPS = 4
TOP_K = 8
D_EXPERT = 256
D_SHARED = 256
ROUTED_SCALE = 2.5
RMS_EPS = 1e-6
NEG_INF = -1e30
ATTN_W = N_HEADS * HEAD_DIM
KV_W = N_KV_HEADS * HEAD_DIM
IN_COLS = D_SSM + ATTN_W + 2 * KV_W + 2 * D_MODEL
IN_SPLITS = (0, D_SSM, D_SSM + ATTN_W, D_SSM + ATTN_W + KV_W, D_SSM + ATTN_W + 2 * KV_W,
             D_SSM + ATTN_W + 2 * KV_W + D_MODEL, IN_COLS)

LANES = 128
SSM_L = 16
SSM_OCT_G = LANES // SSM_GROUP_CH
SSM_OCTS = SSM_GROUPS // SSM_OCT_G
SSM_FLAT = SSM_L * LANES
SSM_SW = SSM_OCT_G * SSM_STATE
SSM_POW_ROWS = 16
VMEM_LIMIT = 56 * 1024 * 1024
SUBLANES = 8
ATTN_ROWS = 512
SLAB_ROWS = D_MODEL // LANES
MOE_BS = 128
MOE_STRIDE = MOE_BS + SUBLANES
MOE_RMW_UNROLL = 4
MOE_WSLOTS = 4
MOE_WORD_BITS = 16


def _cparams(*sem):
    return pltpu.CompilerParams(dimension_semantics=sem, vmem_limit_bytes=VMEM_LIMIT)


def _dot(a, b):
    return jnp.dot(a, b, preferred_element_type=F32)


def _dot_nt(a, b):
    return lax.dot_general(a, b, (((1,), (1,)), ((), ())), preferred_element_type=F32)


def _split(x):
    hi = x.astype(BF16)
    lo = (x - hi.astype(F32)).astype(BF16)
    return hi, lo


def _rmsnorm(x, g):
    return x * lax.rsqrt(jnp.mean(x * x, axis=-1, keepdims=True) + RMS_EPS) * g


def _mod_kernel(c_ref, w_ref, b_ref, o_ref):
    cond = jax.nn.silu(c_ref[...])
    ch, cl = _split(cond)
    wh, wl = _split(w_ref[...])
    o_ref[...] = _dot(ch, wh) + (_dot(ch, wl) + _dot(cl, wh)) + b_ref[...]


def _mod_call(c_all, ada_w, ada_b):
    depth = ada_w.shape[0]
    rows = c_all.shape[0]
    nj = 6
    return pl.pallas_call(
        _mod_kernel,
        grid=(depth, nj),
        in_specs=[
            pl.BlockSpec((rows, D_MODEL), lambda l, j: (0, 0)),
            pl.BlockSpec((None, D_MODEL, D_MODEL), lambda l, j: (l, 0, j)),
            pl.BlockSpec((None, 1, D_MODEL), lambda l, j: (l, 0, j)),
        ],
        out_specs=pl.BlockSpec((None, rows, D_MODEL), lambda l, j: (l, 0, j)),
        out_shape=jax.ShapeDtypeStruct((depth, rows, nj * D_MODEL), F32),
        compiler_params=_cparams("parallel", "parallel"),
        name="mod",
    )(c_all, ada_w, ada_b.reshape(depth, 1, nj * D_MODEL))


def _rope2(t, cos, sin_signed, first_half):
    swapped = jnp.where(first_half, pltpu.roll(t, LANES - HEAD_DIM // 2, axis=1),
                        pltpu.roll(t, HEAD_DIM // 2, axis=1))
    return t * cos + swapped * sin_signed


def _inproj_kernel(x_ref, sh_ref, sc_ref, g_ref, w_ref, cos_ref, sin_ref,
                   u_ref, q_ref, k_ref, v_ref, gs_ref, ga_ref):
    h = _rmsnorm(x_ref[...], g_ref[...]) * (1.0 + sc_ref[...]) + sh_ref[...]
    hb = h.astype(BF16)

    def proj(i):
        return _dot(hb, w_ref[:, IN_SPLITS[i]:IN_SPLITS[i + 1]])

    u_ref[...] = proj(0)
    cos = cos_ref[...]
    sin = sin_ref[...]
    lane = lax.broadcasted_iota(jnp.int32, cos.shape, 1)
    first_half = (lane % HEAD_DIM) < (HEAD_DIM // 2)
    q = proj(1)
    for j in range(ATTN_W // LANES):
        sl = slice(j * LANES, (j + 1) * LANES)
        q_ref[:, sl] = (_rope2(q[:, sl], cos, sin, first_half) * (HEAD_DIM ** -0.5)).astype(BF16)
    k_ref[...] = _rope2(proj(2), cos, sin, first_half)
    v_ref[...] = proj(3)
    gs_ref[...] = proj(4)
    ga_ref[...] = proj(5)


def _inproj_call(x, sh1, sc1, norm_g, w_in_bf, cos_t, sin_t, tt):
    bx, tx, _ = x.shape
    nt = tx // tt
    per_tok = sh1.shape[1] != 1
    mod_spec = (pl.BlockSpec((None, tt, D_MODEL), lambda b, i: (b, i, 0)) if per_tok
                else pl.BlockSpec((None, 1, D_MODEL), lambda b, i: (b, 0, 0)))

    def tok_spec(w):
        return pl.BlockSpec((None, tt, w), lambda b, i: (b, i, 0))

    def tok_shape(w, dt):
        return jax.ShapeDtypeStruct((bx, tx, w), dt)

    return pl.pallas_call(
        _inproj_kernel,
        grid=(bx, nt),
        in_specs=[
            tok_spec(D_MODEL), mod_spec, mod_spec,
            pl.BlockSpec((1, D_MODEL), lambda b, i: (0, 0)),
            pl.BlockSpec((D_MODEL, IN_COLS), lambda b, i: (0, 0)),
            pl.BlockSpec((tt, LANES), lambda b, i: (i, 0)),
            pl.BlockSpec((tt, LANES), lambda b, i: (i, 0)),
        ],
        out_specs=[tok_spec(D_SSM), tok_spec(ATTN_W), tok_spec(KV_W), tok_spec(KV_W),
                   tok_spec(D_MODEL), tok_spec(D_MODEL)],
        out_shape=[tok_shape(D_SSM, F32), tok_shape(ATTN_W, BF16), tok_shape(KV_W, F32),
                   tok_shape(KV_W, F32), tok_shape(D_MODEL, F32), tok_shape(D_MODEL, F32)],
        compiler_params=_cparams("parallel", "parallel"),
        name="inproj",
    )(x, sh1, sc1, norm_g, w_in_bf, cos_t, sin_t)


def _ssm_kernel(u_ref, kt_ref, bd_ref, rv_ref, cd_ref, fw_ref, ap_ref, h0_ref, y_ref, s_ref,
                t_ref, bl_ref, cl_ref, *, nb, cb):
    nc = nb * cb
    sw = SSM_SW

    @pl.when(pl.program_id(1) == 0)
    def _():
        zero = jnp.zeros((LANES, LANES), BF16)
        for s in range(SSM_L):
            for t in range(SSM_L):
                t_ref[s * LANES:(s + 1) * LANES, t * LANES:(t + 1) * LANES] = kt_ref[t - s] if t >= s else zero
        bre = bd_ref[:, :sw]
        bim = bd_ref[:, sw:]
        for s in range(SSM_L):
            rre = rv_ref[s:s + 1, :sw]
            rim = rv_ref[s:s + 1, sw:]
            bl_ref[s * LANES:(s + 1) * LANES, :sw] = (bre * rre - bim * rim).astype(BF16)
            bl_ref[s * LANES:(s + 1) * LANES, sw:] = (bre * rim + bim * rre).astype(BF16)
        cre = cd_ref[:sw, :]
        cim = cd_ref[sw:, :]
        for t in range(SSM_L):
            fre = fw_ref[:sw, t:t + 1]
            fim = fw_ref[sw:, t:t + 1]
            cl_ref[:sw, t * LANES:(t + 1) * LANES] = (cre * fre - cim * fim).astype(BF16)
            cl_ref[sw:, t * LANES:(t + 1) * LANES] = (-(cre * fim + cim * fre)).astype(BF16)

    uf = jnp.concatenate([u_ref[pl.ds(s, nc, stride=SSM_L), :].astype(BF16) for s in range(SSM_L)], axis=1)
    y = _dot(uf, t_ref[...])
    v = _dot(uf, bl_ref[...])
    xre = v[:, :sw]
    xim = v[:, sw:]

    row = lax.broadcasted_iota(jnp.int32, (nc, sw), 0)
    cidx = row & (cb - 1)
    bidx = row // cb
    h0 = h0_ref[...]
    h0re = jnp.zeros((nc, sw), F32)
    h0im = jnp.zeros((nc, sw), F32)
    for b in range(nb):
        h0re = jnp.where(bidx == b, h0[b:b + 1, :sw], h0re)
        h0im = jnp.where(bidx == b, h0[b:b + 1, sw:], h0im)
    first = cidx == 0
    are = ap_ref[0:1, :sw]
    aim = ap_ref[0:1, sw:]
    xre = xre + jnp.where(first, are * h0re - aim * h0im, 0.0)
    xim = xim + jnp.where(first, are * h0im + aim * h0re, 0.0)

    d = 1
    k = 0
    while d < cb:
        are = ap_ref[k:k + 1, :sw]
        aim = ap_ref[k:k + 1, sw:]
        keep = cidx >= d
        sre = jnp.where(keep, pltpu.roll(xre, d, axis=0), 0.0)
        sim = jnp.where(keep, pltpu.roll(xim, d, axis=0), 0.0)
        xre, xim = xre + (are * sre - aim * sim), xim + (are * sim + aim * sre)
        d *= 2
        k += 1

    for b in range(nb):
        r = b * cb + cb - 1
        s_ref[b:b + 1, :sw] = xre[r:r + 1, :]
        s_ref[b:b + 1, sw:] = xim[r:r + 1, :]

    pre = jnp.where(first, h0re, pltpu.roll(xre, 1, axis=0))
    pim = jnp.where(first, h0im, pltpu.roll(xim, 1, axis=0))
    y = y + _dot(pre.astype(BF16), cl_ref[:sw, :]) + _dot(pim.astype(BF16), cl_ref[sw:, :])
    for t in range(SSM_L):
        y_ref[pl.ds(t, nc, stride=SSM_L), :] = y[:, t * LANES:(t + 1) * LANES]


def _ssm_call(u, mats, h0o, nb, cb):
    kt, bd, rv, cd, fw, ap = mats
    bx, tx, _ = u.shape

    def const_spec(a):
        return pl.BlockSpec((None,) + a.shape[1:], lambda o, b: (o,) + (0,) * (a.ndim - 1))

    tok_spec = pl.BlockSpec((None, tx, LANES), lambda o, b: (b, 0, o))
    st_spec = pl.BlockSpec((None, None, nb, 2 * SSM_SW), lambda o, b: (o, b, 0, 0))
    return pl.pallas_call(
        functools.partial(_ssm_kernel, nb=nb, cb=cb),
        grid=(SSM_OCTS, bx),
        in_specs=[tok_spec] + [const_spec(a) for a in (kt, bd, rv, cd, fw, ap)] + [st_spec],
        out_specs=[tok_spec, st_spec],
        out_shape=[jax.ShapeDtypeStruct((bx, tx, D_SSM), F32),
                   jax.ShapeDtypeStruct((SSM_OCTS, bx, nb, 2 * SSM_SW), F32)],
        scratch_shapes=[pltpu.VMEM((SSM_FLAT, SSM_FLAT), BF16), pltpu.VMEM((SSM_FLAT, 2 * SSM_SW), BF16),
                        pltpu.VMEM((2 * SSM_SW, SSM_FLAT), BF16)],
        compiler_params=_cparams("arbitrary", "arbitrary"),
        name="ssm",
    )(u, kt, bd, rv, cd, fw, ap, h0o)


def _oct_lanes(a):
    lead = a.shape[:-2]
    a = a.reshape(*lead, SSM_OCTS, SSM_SW)
    return jnp.moveaxis(a, -2, 0)


def _ssm_mats(a_re, a_im, log_dt, b_re, b_im, c_re, c_im):
    dt = jnp.exp(log_dt)[:, None]
    lre = a_re * dt
    lim = a_im * dt
    mag = jnp.exp(lre)
    ab_re = mag * jnp.cos(lim)
    ab_im = mag * jnp.sin(lim)
    den = a_re * a_re + a_im * a_im
    n_re = ab_re - 1.0
    f_re = (n_re * a_re + ab_im * a_im) / den
    f_im = (ab_im * a_re - n_re * a_im) / den
    bb_re = f_re[..., None] * b_re - f_im[..., None] * b_im
    bb_im = f_re[..., None] * b_im + f_im[..., None] * b_re

    def lam_pow(tau):
        tau = tau[:, None, None]
        m = jnp.exp(tau * lre)
        return m * jnp.cos(tau * lim), m * jnp.sin(tau * lim)

    pw_re, pw_im = lam_pow(jnp.arange(SSM_L + 1, dtype=F32))
    cp_re = c_re[None] * pw_re[:, :, None, :] - c_im[None] * pw_im[:, :, None, :]
    cp_im = c_re[None] * pw_im[:, :, None, :] + c_im[None] * pw_re[:, :, None, :]
    kern = jnp.sum(cp_re[..., None] * bb_re[None, :, None] - cp_im[..., None] * bb_im[None, :, None],
                   axis=3)
    eye = jnp.eye(SSM_OCT_G, dtype=F32)
    og = (SSM_OCTS, SSM_OCT_G)

    def slab_diag(m):
        lead = m.shape[:-3]
        a, b = m.shape[-2:]
        m = m.reshape(*lead, *og, a, b)
        m = m[..., :, None, :] * eye[:, None, :, None]
        return m.reshape(*lead, SSM_OCTS, SSM_OCT_G * a, SSM_OCT_G * b)

    kt = slab_diag(kern.transpose(0, 1, 3, 2)).astype(BF16).transpose(1, 0, 2, 3)
    bd = jnp.concatenate([slab_diag(bb_re.transpose(0, 2, 1)), slab_diag(bb_im.transpose(0, 2, 1))], axis=-1)
    rev = SSM_L - 1 - jnp.arange(SSM_L)
    rv = jnp.concatenate([_oct_lanes(pw_re[rev]), _oct_lanes(pw_im[rev])], axis=-1)
    cd = jnp.concatenate([slab_diag(c_re.transpose(0, 2, 1)), slab_diag(c_im.transpose(0, 2, 1))], axis=1)
    fw = jnp.concatenate([_oct_lanes(pw_re[1:]), _oct_lanes(pw_im[1:])], axis=-1).transpose(0, 2, 1)
    fw = jnp.pad(fw, ((0, 0), (0, 0), (0, LANES - SSM_L)))

    steps = float(SSM_L) * (2.0 ** jnp.arange(SSM_POW_ROWS, dtype=F32))
    ap_re, ap_im = lam_pow(steps)
    ap = jnp.concatenate([_oct_lanes(ap_re), _oct_lanes(ap_im)], axis=-1)
    return kt, bd, rv, cd, fw, ap


def _ssm_branch(u, mats, h0, nb, cb):
    bx = u.shape[0]
    h0o = jnp.concatenate([_oct_lanes(h0[..., 0]), _oct_lanes(h0[..., 1])], axis=-1)
    y, s = _ssm_call(u, mats, h0o.reshape(SSM_OCTS, bx, nb, 2 * SSM_SW), nb, cb)
    s = s.reshape(SSM_OCTS, bx * nb, 2, SSM_OCT_G, SSM_STATE).transpose(1, 0, 3, 4, 2)
    return y, s.reshape(bx * nb, SSM_GROUPS, SSM_STATE, 2)


def _attn_kernel(sink_ref, q_ref, ka_ref, kb_ref, va_ref, vb_ref, o_ref, *, banded):
    nk = 2 * LANES
    if banded:
        sub = 2 * CHUNK
        kfull = jnp.concatenate([ka_ref[...], kb_ref[...]], axis=0)
        vfull = jnp.concatenate([va_ref[...], vb_ref[...]], axis=0)
    else:
        sub = q_ref.shape[0]
        pad = jnp.zeros((nk - ka_ref.shape[0] - kb_ref.shape[0], KV_W), F32)
        kfull = jnp.concatenate([ka_ref[...], kb_ref[...], pad], axis=0)
        vfull = jnp.concatenate([va_ref[...], vb_ref[...], pad], axis=0)
    nsub = q_ref.shape[0] // sub

    r_i = lax.broadcasted_iota(jnp.int32, (sub, nk), 0)
    c_i = lax.broadcasted_iota(jnp.int32, (sub, nk), 1)
    if banded:
        lo = (r_i // CHUNK) * CHUNK
        valid = (c_i >= lo) & (c_i < lo + WINDOW + CHUNK)
        first_frame = jnp.where(pl.program_id(1) > 0, 0, WINDOW)
        valids = [valid & (c_i >= first_frame)] + [valid] * (nsub - 1)
    else:
        valids = [c_i < (ka_ref.shape[0] + kb_ref.shape[0])]

    lane = lax.broadcasted_iota(jnp.int32, kfull.shape, 1)
    low = lane < HEAD_DIM
    kroll = pltpu.roll(kfull, HEAD_DIM, axis=1)
    vroll = pltpu.roll(vfull, HEAD_DIM, axis=1)

    for g in range(N_KV_HEADS):
        k_lo, k_hi = (kfull, kroll) if g == 0 else (kroll, kfull)
        v_lo, v_hi = (vfull, vroll) if g == 0 else (vroll, vfull)
        kpad = (jnp.where(low, k_lo, 0.0).astype(BF16), jnp.where(low, 0.0, k_hi).astype(BF16))
        vpad = (jnp.where(low, v_lo, 0.0).astype(BF16), jnp.where(low, 0.0, v_hi).astype(BF16))
        for b in range(nsub):
            rs = slice(b * sub, (b + 1) * sub)
            ks = slice(b * sub, b * sub + nk)
            for pp in range(2):
                slab = 2 * g + pp
                qp = q_ref[rs, slab * LANES:(slab + 1) * LANES]
                acc = None
                for hh in range(2):
                    sk = sink_ref[2 * slab + hh]
                    s = jnp.where(valids[b], _dot_nt(qp, kpad[hh][ks]), NEG_INF)
                    m = jnp.maximum(jnp.max(s, axis=-1, keepdims=True), sk)
                    p = jnp.exp(s - m)
                    den = jnp.sum(p, axis=-1, keepdims=True) + jnp.exp(sk - m)
                    o = _dot(p.astype(BF16), vpad[hh][ks]) / den
                    acc = o if acc is None else acc + o
                o_ref[rs, slab * LANES:(slab + 1) * LANES] = acc.astype(BF16)


def _attn_call(sink, q, ka, kb, va, vb, banded):
    bx, tx, _ = q.shape
    if banded:
        rows = ATTN_ROWS
        nt = tx // rows
        grid = (bx, nt)
        q_spec = pl.BlockSpec((None, rows, ATTN_W), lambda b, i: (b, i, 0))
        prev = pl.BlockSpec((None, WINDOW, KV_W), lambda b, i: (b, jnp.maximum(i * (rows // WINDOW) - 1, 0), 0))
        cur = pl.BlockSpec((None, rows, KV_W), lambda b, i: (b, i, 0))
    else:
        rows = tx
        grid = (bx, 1)
        q_spec = pl.BlockSpec((None, rows, ATTN_W), lambda b, i: (b, 0, 0))
        prev = pl.BlockSpec((None, ka.shape[1], KV_W), lambda b, i: (b, 0, 0))
        cur = pl.BlockSpec((None, rows, KV_W), lambda b, i: (b, 0, 0))
    return pl.pallas_call(
        functools.partial(_attn_kernel, banded=banded),
        grid=grid,
        in_specs=[pl.BlockSpec(memory_space=pltpu.SMEM), q_spec, prev, cur, prev, cur],
        out_specs=q_spec,
        out_shape=jax.ShapeDtypeStruct((bx, tx, ATTN_W), BF16),
        compiler_params=_cparams("parallel", "parallel"),
        name="attn",
    )(sink, q, ka, kb, va, vb)


def _route(logits_t, bias_col, topi_ref, topw_ref):
    tt = logits_t.shape[1]
    scores = jax.nn.sigmoid(logits_t)
    biased = scores + bias_col
    sub = lax.broadcasted_iota(jnp.int32, (EXPERTS_PER_GROUP, tt), 0).astype(F32)
    ninf = float('-inf')
    blocks = [biased[EXPERTS_PER_GROUP * g:EXPERTS_PER_GROUP * (g + 1), :] for g in range(N_EXPERT_GROUPS)]

    gscore = jnp.zeros((N_EXPERT_GROUPS, tt), F32)
    for g in range(N_EXPERT_GROUPS):
        blk = blocks[g]
        m1 = jnp.max(blk, axis=0, keepdims=True)
        i1 = jnp.min(jnp.where(blk == m1, sub, float(EXPERTS_PER_GROUP)), axis=0, keepdims=True)
        m2 = jnp.max(jnp.where(sub == i1, ninf, blk), axis=0, keepdims=True)
        gscore = jnp.where(sub == float(g), jnp.broadcast_to(m1 + m2, gscore.shape), gscore)

    grank = jnp.zeros((N_EXPERT_GROUPS, tt), F32)
    for j in range(N_EXPERT_GROUPS):
        rj = jnp.broadcast_to(gscore[j:j + 1, :], gscore.shape)
        beats = (rj > gscore) | ((rj == gscore) & (sub > float(j)))
        grank = grank + jnp.where(beats, 1.0, 0.0)
    gsel = jnp.where(grank < float(TOPK_GROUPS), 1.0, 0.0)

    masked = []
    for g in range(N_EXPERT_GROUPS):
        on = jnp.broadcast_to(gsel[g:g + 1, :], blocks[g].shape) > 0.5
        masked.append(jnp.where(on, blocks[g], NEG_INF))

    ranks = [jnp.zeros((EXPERTS_PER_GROUP, tt), F32) for _ in range(N_EXPERT_GROUPS)]
    for jb in range(N_EXPERT_GROUPS):
        for jj in range(EXPERTS_PER_GROUP):
            rj = jnp.broadcast_to(masked[jb][jj:jj + 1, :], (EXPERTS_PER_GROUP, tt))
            for ib in range(N_EXPERT_GROUPS):
                mi = masked[ib]
                if ib < jb:
                    beats = rj > mi
                elif ib > jb:
                    beats = rj >= mi
                else:
                    beats = (rj > mi) | ((rj == mi) & (sub > float(jj)))
                ranks[ib] = ranks[ib] + jnp.where(beats, 1.0, 0.0)

    topi = jnp.zeros((TOP_K, tt), F32)
    topw = jnp.zeros((TOP_K, tt), F32)
    for r in range(TOP_K):
        ai = jnp.zeros((EXPERTS_PER_GROUP, tt), F32)
        aw = jnp.zeros((EXPERTS_PER_GROUP, tt), F32)
        for g in range(N_EXPERT_GROUPS):
            hit = ranks[g] == float(r)
            ai = ai + jnp.where(hit, sub + float(EXPERTS_PER_GROUP * g), 0.0)
            aw = aw + jnp.where(hit, scores[EXPERTS_PER_GROUP * g:EXPERTS_PER_GROUP * (g + 1), :], 0.0)
        ir = jnp.sum(ai, axis=0, keepdims=True)
        wr = jnp.sum(aw, axis=0, keepdims=True)
        topi = jnp.where(sub == float(r), jnp.broadcast_to(ir, topi.shape), topi)
        topw = jnp.where(sub == float(r), jnp.broadcast_to(wr, topw.shape), topw)
    den = jnp.sum(topw, axis=0, keepdims=True)
    topi_ref[...] = topi.astype(jnp.int32)
    topw_ref[...] = topw / den * ROUTED_SCALE


def _post_kernel(y_ref, u_ref, a_ref, gs_ref, ga_ref, x_ref, g1_ref, sh2_ref, sc2_ref, g2_ref,
                 d_ref, wglu_ref, bglu_ref, wbs_ref, wba_ref, wout_ref, nf_ref, rwt_ref, rb_ref,
                 swg_ref, swu_ref, swd_ref, xp_ref, h2s_ref, topi_ref, topw_ref):
    ys = y_ref[...] + d_ref[...] * u_ref[...]
    z = jax.nn.gelu(ys)
    y2 = z * jax.nn.sigmoid(_dot(z.astype(BF16), wglu_ref[...]) + bglu_ref[...])
    merged = (jax.nn.sigmoid(gs_ref[...]) * _dot(y2.astype(BF16), wbs_ref[...])
              + jax.nn.sigmoid(ga_ref[...]) * _dot(a_ref[...], wba_ref[...]))
    x1 = x_ref[...] + g1_ref[...] * _dot(merged.astype(BF16), wout_ref[...])

    h2 = _rmsnorm(x1, nf_ref[...]) * (1.0 + sc2_ref[...]) + sh2_ref[...]
    tt = h2.shape[0]
    for j in range(SLAB_ROWS):
        h2s_ref[pl.ds(j, tt, stride=SLAB_ROWS), :] = h2[:, j * LANES:(j + 1) * LANES]
    hh, hl = _split(h2)
    rh, rl = _split(rwt_ref[...])
    logits_t = _dot_nt(rh, hh) + (_dot_nt(rh, hl) + _dot_nt(rl, hh))
    _route(logits_t, rb_ref[...], topi_ref, topw_ref)

    shared = _dot((jax.nn.silu(_dot(hh, swg_ref[...])) * _dot(hh, swu_ref[...])).astype(BF16), swd_ref[...])
    xp_ref[...] = x1 + g2_ref[...] * shared


def _post_call(y, u, attn, gs, ga, x, mods, lw, tt):
    bx, tx, _ = x.shape
    nt = tx // tt
    per_tok = mods[0].shape[1] != 1
    mod_spec = (pl.BlockSpec((None, tt, D_MODEL), lambda b, i: (b, i, 0)) if per_tok
                else pl.BlockSpec((None, 1, D_MODEL), lambda b, i: (b, 0, 0)))

    def tok_spec(w):
        return pl.BlockSpec((None, tt, w), lambda b, i: (b, i, 0))

    def full(a):
        return pl.BlockSpec(a.shape, lambda b, i: (0,) * a.ndim)

    weights = [lw['ssm_d'], lw['w_glu'], lw['b_glu'], lw['w_bs'], lw['w_ba'], lw['w_out'], lw['norm_ffn_g'],
               lw['router_wt'], lw['router_bias'], lw['sh_wg'], lw['sh_wu'], lw['sh_wd']]
    return pl.pallas_call(
        _post_kernel,
        grid=(bx, nt),
        in_specs=[tok_spec(D_SSM), tok_spec(D_SSM), tok_spec(ATTN_W), tok_spec(D_MODEL), tok_spec(D_MODEL),
                  tok_spec(D_MODEL), mod_spec, mod_spec, mod_spec, mod_spec] + [full(w) for w in weights],
        out_specs=[tok_spec(D_MODEL),
                   pl.BlockSpec((tt * SLAB_ROWS, LANES), lambda b, i: (b * nt + i, 0)),
                   pl.BlockSpec((TOP_K, tt), lambda b, i: (0, b * nt + i)),
                   pl.BlockSpec((TOP_K, tt), lambda b, i: (0, b * nt + i))],
        out_shape=[jax.ShapeDtypeStruct((bx, tx, D_MODEL), F32),
                   jax.ShapeDtypeStruct((bx * tx * SLAB_ROWS, LANES), F32),
                   jax.ShapeDtypeStruct((TOP_K, bx * tx), jnp.int32),
                   jax.ShapeDtypeStruct((TOP_K, bx * tx), F32)],
        compiler_params=_cparams("parallel", "parallel"),
        name="post",
    )(y, u, attn, gs, ga, x, *mods, *weights)


def _moe_kernel(ptab_ref, ktab_ref, elist_ref, meta_ref, hs_ref, word_hbm, gate_hbm, wg_hbm, wu_hbm, wd_hbm, o_ref,
                word_s, gate_s, xt0, xt1, yt0, yt1, wg_buf, wu_buf, wd_buf, lsem, wsem, *, tm, ptab_w, layer):
    i = pl.program_id(0)
    nslot = word_s.shape[0]
    ngate = gate_s.shape[0]
    acc_ref = o_ref
    slab_mask = -SLAB_ROWS
    gather_mask = (tm * SLAB_ROWS - 1) & slab_mask
    tab0 = i * ptab_w + 1
    npairs = meta_ref[2 * i]
    nrank = meta_ref[2 * i + 1]

    def list_copies():
        return (pltpu.make_async_copy(word_hbm.at[pl.ds(pl.multiple_of(i * nslot, LANES), nslot)],
                                      word_s, lsem.at[0]),
                pltpu.make_async_copy(gate_hbm.at[pl.ds(pl.multiple_of(i * ngate, LANES), ngate)],
                                      gate_s, lsem.at[1]))

    def weight_copies(k):
        ex = elist_ref[i * N_EXPERTS + k]
        slot = k % MOE_WSLOTS
        return (pltpu.make_async_copy(wg_hbm.at[layer, ex], wg_buf.at[slot], wsem.at[slot, 0]),
                pltpu.make_async_copy(wu_hbm.at[layer, ex], wu_buf.at[slot], wsem.at[slot, 1]),
                pltpu.make_async_copy(wd_hbm.at[layer, ex], wd_buf.at[slot], wsem.at[slot, 2]))

    def gather(base, xt_ref):
        for r in range(MOE_BS):
            row = pl.multiple_of(word_s[base + r] & gather_mask, SLAB_ROWS)
            xt_ref[pl.ds(r, SLAB_ROWS, stride=MOE_STRIDE), :] = hs_ref[pl.ds(row, SLAB_ROWS), :]

    def expert(xt_ref, yt_ref, k):
        slot = k % MOE_WSLOTS
        x = jnp.concatenate([xt_ref[j * MOE_STRIDE:j * MOE_STRIDE + MOE_BS, :] for j in range(SLAB_ROWS)],
                            axis=1).astype(BF16)
        act = jax.nn.silu(_dot(x, wg_buf[slot])) * _dot(x, wu_buf[slot])
        y = _dot(act.astype(BF16), wd_buf[slot])
        for j in range(SLAB_ROWS):
            yt_ref[j * MOE_STRIDE:j * MOE_STRIDE + MOE_BS, :] = y[:, j * LANES:(j + 1) * LANES]

    def scatter(base, yt_ref):
        for r0 in range(0, MOE_BS, MOE_RMW_UNROLL):
            upd = []
            for r in range(r0, r0 + MOE_RMW_UNROLL):
                word = word_s[base + r]
                row = pl.multiple_of(word & slab_mask, SLAB_ROWS)
                upd.append((row, acc_ref[pl.ds(row, SLAB_ROWS), :]
                            + gate_s[word] * yt_ref[pl.ds(r, SLAB_ROWS, stride=MOE_STRIDE), :]))
            for row, val in upd:
                acc_ref[pl.ds(row, SLAB_ROWS), :] = val

    for c in list_copies():
        c.start()
    ahead = MOE_WSLOTS - 2
    for r in range(ahead + 1):
        @pl.when(r < nrank)
        def _():
            for c in weight_copies(r):
                c.start()

    acc_ref[...] = jnp.zeros_like(acc_ref)
    yt1[...] = jnp.zeros_like(yt1)
    for c in list_copies():
        c.wait()
    gather(ptab_ref[tab0], xt0)

    def pair(j, carry):
        waited, started = carry
        t = tab0 + 2 * j
        k0 = ktab_ref[t]
        k1 = ktab_ref[t + 1]

        target = jnp.minimum(k1 + ahead, nrank - 1)
        for d in (1, 2):
            @pl.when(started + d <= target)
            def _():
                for c in weight_copies(started + d):
                    c.start()

        for d in (1, 2):
            @pl.when(waited + d <= k1)
            def _():
                for c in weight_copies(waited + d):
                    c.wait()

        gather(ptab_ref[t + 1], xt1)
        expert(xt0, yt0, k0)
        scatter(ptab_ref[t - 1], yt1)
        gather(ptab_ref[t + 2], xt0)
        expert(xt1, yt1, k1)
        scatter(ptab_ref[t], yt0)
        return jnp.maximum(waited, k1), jnp.maximum(started, target)

    lax.fori_loop(0, npairs, pair, (jnp.int32(-1), jnp.minimum(jnp.int32(ahead), nrank - 1)))
    scatter(ptab_ref[tab0 + 2 * npairs - 1], yt1)


def _moe_call(h2s, topi, topw, wg, wu, wd, layer, tm):
    n = h2s.shape[0] // SLAB_ROWS
    nt = n // tm
    nseg = nt * N_EXPERTS
    nreal = tm * TOP_K
    nslot = nreal + (N_EXPERTS + 1) * MOE_BS
    null_base = nslot - MOE_BS
    ptab_w = nreal // MOE_BS + N_EXPERTS + 4

    i32 = jnp.int32
    experts = jnp.arange(N_EXPERTS, dtype=i32)
    counts = jnp.sum((topi.reshape(TOP_K, nt, tm)[..., None] == experts).astype(i32), axis=(0, 2))
    nblk = (counts + MOE_BS - 1) // MOE_BS
    first = jnp.cumsum(counts + MOE_BS, axis=1) - (counts + MOE_BS)
    pend = jnp.cumsum(nblk, axis=1)
    pstart = pend - nblk
    nonempty = (nblk > 0).astype(i32)
    krank = jnp.cumsum(nonempty, axis=1) - 1
    nrank = jnp.sum(nonempty, axis=1)
    elist = jnp.sum(jnp.where((nonempty[:, None, :] == 1) & (krank[:, None, :] == experts[None, :, None]),
                              experts[None, None, :], 0), axis=-1).astype(i32)
    pos = jnp.arange(ptab_w, dtype=i32) - 1
    e_of = jnp.sum((pend[:, None, :] <= pos[None, :, None]).astype(i32), axis=-1)
    e_cl = jnp.minimum(e_of, N_EXPERTS - 1)
    q = pos[None, :] - jnp.take_along_axis(pstart, e_cl, axis=1)
    real_blk = (pos[None, :] >= 0) & (e_of < N_EXPERTS)
    ptab = jnp.where(real_blk, jnp.take_along_axis(first, e_cl, axis=1) + q * MOE_BS, null_base).astype(i32)
    ktab = jnp.where(real_blk, jnp.take_along_axis(krank, e_cl, axis=1), nrank[:, None] - 1).astype(i32)
    meta = jnp.stack([(pend[:, -1] + 1) // 2, nrank], axis=1).astype(i32)

    assert TOP_K == SLAB_ROWS and nreal < (1 << MOE_WORD_BITS) and 2 * nseg + 2 < (1 << (31 - MOE_WORD_BITS))
    slot = jnp.arange(n * TOP_K, dtype=i32)
    key = 2 * ((slot // nreal) * N_EXPERTS + topi.T.reshape(n * TOP_K))
    pad_key = jnp.concatenate([jnp.repeat(2 * jnp.arange(nseg, dtype=i32) + 1, MOE_BS),
                               jnp.repeat(2 * (jnp.arange(nt, dtype=i32) * N_EXPERTS + N_EXPERTS - 1) + 1, MOE_BS)])
    packed = jnp.concatenate([(key << MOE_WORD_BITS) | (slot % nreal), (pad_key << MOE_WORD_BITS) | nreal])
    word_s = lax.sort(packed, is_stable=False) & ((1 << MOE_WORD_BITS) - 1)
    ngate = nreal + LANES
    gate_s = jnp.concatenate([topw.T.reshape(nt, nreal), jnp.zeros((nt, LANES), F32)], axis=1).reshape(nt * ngate)

    stage = pltpu.VMEM((SLAB_ROWS * MOE_STRIDE, LANES), F32)
    hbm = pl.BlockSpec(memory_space=pl.ANY)
    grid_spec = pltpu.PrefetchScalarGridSpec(
        num_scalar_prefetch=4,
        grid=(nt,),
        in_specs=[
            pl.BlockSpec((tm * SLAB_ROWS, LANES), lambda i, *_: (i, 0), pipeline_mode=pl.Buffered(1)),
            hbm, hbm, hbm, hbm, hbm,
        ],
        out_specs=pl.BlockSpec((None, (tm + 1) * SLAB_ROWS, LANES), lambda i, *_: (i, 0, 0),
                               pipeline_mode=pl.Buffered(1)),
        scratch_shapes=[
            pltpu.SMEM((nslot,), i32),
            pltpu.SMEM((ngate,), F32),
            stage, stage, stage, stage,
            pltpu.VMEM((MOE_WSLOTS, D_MODEL, D_EXPERT), BF16),
            pltpu.VMEM((MOE_WSLOTS, D_MODEL, D_EXPERT), BF16),
            pltpu.VMEM((MOE_WSLOTS, D_EXPERT, D_MODEL), BF16),
            pltpu.SemaphoreType.DMA((2,)),
            pltpu.SemaphoreType.DMA((MOE_WSLOTS, 3)),
        ],
    )
    return pl.pallas_call(
        functools.partial(_moe_kernel, tm=tm, ptab_w=ptab_w, layer=layer),
        grid_spec=grid_spec,
        out_shape=jax.ShapeDtypeStruct((nt, (tm + 1) * SLAB_ROWS, LANES), F32),
        compiler_params=_cparams("arbitrary"),
        name="moe",
    )(ptab.reshape(nt * ptab_w), ktab.reshape(nt * ptab_w), elist.reshape(nseg), meta.reshape(nt * 2),
      h2s, word_s, gate_s, wg, wu, wd)


def _combine_kernel(r_ref, xp_ref, g2_ref, fg_ref, o_ref, *, final):
    tt = xp_ref.shape[0]
    routed = jnp.concatenate([r_ref[pl.ds(j, tt, stride=SLAB_ROWS), :] for j in range(SLAB_ROWS)], axis=1)
    x2 = xp_ref[...] + g2_ref[...] * routed
    if final:
        x2 = _rmsnorm(x2, fg_ref[...])
    o_ref[...] = x2


def _combine_call(routed_s, xp, g2, final_g, tt, final):
    bx, tx, _ = xp.shape
    nt = tx // tt
    per_tok = g2.shape[1] != 1
    g2_spec = (pl.BlockSpec((None, tt, D_MODEL), lambda b, i: (b, i, 0)) if per_tok
               else pl.BlockSpec((None, 1, D_MODEL), lambda b, i: (b, 0, 0)))
    tok_spec = pl.BlockSpec((None, tt, D_MODEL), lambda b, i: (b, i, 0))
    per_tile = (routed_s.shape[1] // SLAB_ROWS - 1) // tt
    routed_spec = pl.BlockSpec((None, tt * SLAB_ROWS, LANES),
                               lambda b, i: ((b * nt + i) // per_tile, (b * nt + i) % per_tile, 0))
    return pl.pallas_call(
        functools.partial(_combine_kernel, final=final),
        grid=(bx, nt),
        in_specs=[routed_spec, tok_spec, g2_spec,
                  pl.BlockSpec((1, D_MODEL), lambda b, i: (0, 0))],
        out_specs=tok_spec,
        out_shape=jax.ShapeDtypeStruct((bx, tx, D_MODEL), F32),
        compiler_params=_cparams("parallel", "parallel"),
        name="combine",
    )(routed_s, xp, g2, final_g)


def _rope_tables(pos):
    half = HEAD_DIM // 2
    inv_freq = ROPE_THETA ** (-jnp.arange(half, dtype=F32) / half)
    ang = pos.astype(F32)[:, None] * inv_freq[None, :]
    cos = jnp.cos(ang)
    sin = jnp.sin(ang)
    return jnp.tile(cos, (1, 4)), jnp.tile(jnp.concatenate([-sin, sin], axis=1), (1, 2))


def _trunk(x, mods, pos, past_k, past_v, ssm_h0, layers, experts, final_g, tt, tm, tok_batches):
    bx, tx, _ = x.shape
    b, t = tok_batches
    cos_t, sin_t = _rope_tables(pos)
    new_k, new_v, new_h = [], [], []
    for l, lw in enumerate(layers):
        sh1, sc1, g1, sh2, sc2, g2 = mods[l]
        u, q, k, v, gs, ga = _inproj_call(x, sh1, sc1, lw['norm_attn_g'], lw['w_in'], cos_t, sin_t, tt)
        ks = k.reshape(b, t, KV_W)
        vs = v.reshape(b, t, KV_W)
        qs = q.reshape(b, t, ATTN_W)
        if past_k is None:
            attn = _attn_call(lw['sink'], qs, ks, ks, vs, vs, True)
            new_k.append(ks[:, -WINDOW:].reshape(b, WINDOW, N_KV_HEADS, HEAD_DIM))
            new_v.append(vs[:, -WINDOW:].reshape(b, WINDOW, N_KV_HEADS, HEAD_DIM))
            h0 = jnp.zeros((b, SSM_GROUPS, SSM_STATE, 2), F32)
        else:
            pk = past_k[l].reshape(b, -1, KV_W)
            pv = past_v[l].reshape(b, -1, KV_W)
            attn = _attn_call(lw['sink'], qs, pk, ks, pv, vs, False)
            new_k.append(ks.reshape(b, t, N_KV_HEADS, HEAD_DIM))
            new_v.append(vs.reshape(b, t, N_KV_HEADS, HEAD_DIM))
            h0 = ssm_h0[l]
        y, h_last = _ssm_branch(u, lw['ssm_mats'], h0, b // bx, t // SSM_L)
        new_h.append(h_last)
        xp, h2s, topi, topw = _post_call(y, u, attn.reshape(bx, tx, ATTN_W), gs, ga, x,
                                         (g1, sh2, sc2, g2), lw, tt)
        routed_s = _moe_call(h2s, topi, topw, *experts, l, tm)
        x = _combine_call(routed_s, xp, g2, final_g, tt, l == len(layers) - 1)
    return x, jnp.stack(new_k), jnp.stack(new_v), jnp.stack(new_h)


def kernel(x_prompt, x_sample, cache_k, cache_v, state_ssm, c_prompt, c_sample, ada_w, ada_b, norm_attn_g,
           norm_ffn_g, w_in, ssm_a_re, ssm_a_im, ssm_log_dt, ssm_b_re, ssm_b_im, ssm_c_re, ssm_c_im, ssm_d,
           ssm_w_glu, ssm_b_glu, attn_sink, w_branch_ssm, w_branch_attn, w_out, router_w, router_bias,
           expert_w_gate, expert_w_up, expert_w_down, shared_w_gate, shared_w_up, shared_w_down, final_g):
    depth = ada_w.shape[0]
    bp, tp, _ = x_prompt.shape
    bs, ts, _ = x_sample.shape

    layers = []
    for l in range(depth):
        layers.append({
            'norm_attn_g': norm_attn_g[l][None], 'norm_ffn_g': norm_ffn_g[l][None],
            'w_in': w_in[l].astype(BF16), 'sink': attn_sink[l],
            'ssm_mats': _ssm_mats(ssm_a_re[l], ssm_a_im[l], ssm_log_dt[l], ssm_b_re[l], ssm_b_im[l],
                                  ssm_c_re[l], ssm_c_im[l]),
            'ssm_d': ssm_d[l][None], 'w_glu': ssm_w_glu[l].astype(BF16), 'b_glu': ssm_b_glu[l][None],
            'w_bs': w_branch_ssm[l].astype(BF16), 'w_ba': w_branch_attn[l].astype(BF16),
            'w_out': w_out[l].astype(BF16), 'router_wt': router_w[l].T, 'router_bias': router_bias[l][:, None],
            'sh_wg': shared_w_gate[l].astype(BF16), 'sh_wu': shared_w_up[l].astype(BF16),
            'sh_wd': shared_w_down[l].astype(BF16),
        })
    experts = (expert_w_gate.astype(BF16), expert_w_up.astype(BF16), expert_w_down.astype(BF16))
    fg = final_g[None]

    rows = 16
    c_all = jnp.concatenate([c_prompt, c_sample, jnp.zeros((rows - bp - bs, D_MODEL), F32)], axis=0)
    mod = _mod_call(c_all, ada_w, ada_b).reshape(depth, rows, 6, D_MODEL)
    mods_p = [[mod[l, :bp, j][:, None, :] for j in range(6)] for l in range(depth)]
    mods_s = [[jnp.repeat(mod[l, bp:bp + bs, j], ts, axis=0)[None] for j in range(6)] for l in range(depth)]

    pos_p = jnp.arange(tp, dtype=jnp.int32)
    pos_s = jnp.tile(PAST_LEN + jnp.arange(ts, dtype=jnp.int32), bs)

    y_p, k_p, v_p, h_p = _trunk(x_prompt, mods_p, pos_p, None, None, None, layers, experts, fg,
                                tt=512, tm=min(4096, bp * tp), tok_batches=(bp, tp))
    y_s, k_s, v_s, h_s = _trunk(x_sample.reshape(1, bs * ts, D_MODEL), mods_s, pos_s, cache_k, cache_v,
                                state_ssm, layers, experts, fg, tt=bs * ts, tm=bs * ts, tok_batches=(bs, ts))
    return (y_p, y_s.reshape(bs, ts, D_MODEL), k_p, v_p, h_p, k_s, v_s, h_s)
```

```python
import functools

import jax
import jax.numpy as jnp
from jax import lax
from jax.experimental import pallas as pl
from jax.experimental.pallas import tpu as pltpu

F32 = jnp.float32
BF16 = jnp.bfloat16

D_MODEL = 1024
DEPTH = 2
PAST_LEN = 4096
CHUNK = 64
N_HEADS = 8
N_KV_HEADS = 2
HEAD_DIM = 64
WINDOW = 128
ROPE_THETA = 10000.0
D_SSM = 512
SSM_GROUP_CH = 16
SSM_GROUPS = D_SSM // SSM_GROUP_CH
SSM_STATE = 64
N_EXPERTS = 64
N_EXPERT_GROUPS = 8
EXPERTS_PER_GROUP = N_EXPERTS // N_EXPERT_GROUPS
TOPK_GROUPS = 4
TOP_K = 8
D_EXPERT = 256
D_SHARED = 256
ROUTED_SCALE = 2.5
RMS_EPS = 1e-6
NEG_INF = -1e30
ATTN_W = N_HEADS * HEAD_DIM
KV_W = N_KV_HEADS * HEAD_DIM
IN_COLS = D_SSM + ATTN_W + 2 * KV_W + 2 * D_MODEL
IN_SPLITS = (0, D_SSM, D_SSM + ATTN_W, D_SSM + ATTN_W + KV_W, D_SSM + ATTN_W + 2 * KV_W,
             D_SSM + ATTN_W + 2 * KV_W + D_MODEL, IN_COLS)

LANES = 128
SSM_L = 16
SSM_OCT_G = LANES // SSM_GROUP_CH
SSM_OCTS = SSM_GROUPS // SSM_OCT_G
SSM_FLAT = SSM_L * LANES
SSM_SW = SSM_OCT_G * SSM_STATE
SSM_POW_ROWS = 16
VMEM_LIMIT = 56 * 1024 * 1024
SUBLANES = 8
ATTN_ROWS = 512
SLAB_ROWS = D_MODEL // LANES
MOE_BLOCK_LARGE = 192
MOE_BLOCK_SMALL = 128
MOE_PAD = 256
MOE_RMW_UNROLL = 4
MOE_WSLOTS = 4
MOE_WORD_BITS = 16


def _cparams(*sem):
    return pltpu.CompilerParams(dimension_semantics=sem, vmem_limit_bytes=VMEM_LIMIT)


def _dot(a, b):
    return jnp.dot(a, b, preferred_element_type=F32)


def _dot_nt(a, b):
    return lax.dot_general(a, b, (((1,), (1,)), ((), ())), preferred_element_type=F32)


def _split(x):
    hi = x.astype(BF16)
    lo = (x - hi.astype(F32)).astype(BF16)
    return hi, lo


def _rmsnorm(x, g):
    return x * lax.rsqrt(jnp.mean(x * x, axis=-1, keepdims=True) + RMS_EPS) * g


def _mod_kernel(c_ref, w_ref, b_ref, o_ref):
    cond = jax.nn.silu(c_ref[...])
    ch, cl = _split(cond)
    wh, wl = _split(w_ref[...])
    o_ref[...] = _dot(ch, wh) + (_dot(ch, wl) + _dot(cl, wh)) + b_ref[...]


def _mod_call(c_all, ada_w, ada_b):
    depth = ada_w.shape[0]
    rows = c_all.shape[0]
    nj = 6
    return pl.pallas_call(
        _mod_kernel,
        grid=(depth, nj),
        in_specs=[
            pl.BlockSpec((rows, D_MODEL), lambda l, j: (0, 0)),
            pl.BlockSpec((None, D_MODEL, D_MODEL), lambda l, j: (l, 0, j)),
            pl.BlockSpec((None, 1, D_MODEL), lambda l, j: (l, 0, j)),
        ],
        out_specs=pl.BlockSpec((None, rows, D_MODEL), lambda l, j: (l, 0, j)),
        out_shape=jax.ShapeDtypeStruct((depth, rows, nj * D_MODEL), F32),
        compiler_params=_cparams("parallel", "parallel"),
        name="mod",
    )(c_all, ada_w, ada_b.reshape(depth, 1, nj * D_MODEL))


def _rope2(t, cos, sin_signed, first_half):
    swapped = jnp.where(first_half, pltpu.roll(t, LANES - HEAD_DIM // 2, axis=1),
                        pltpu.roll(t, HEAD_DIM // 2, axis=1))
    return t * cos + swapped * sin_signed


def _inproj_kernel(x_ref, sh_ref, sc_ref, g_ref, w_ref, cos_ref, sin_ref,
                   u_ref, q_ref, k_ref, v_ref, gs_ref, ga_ref):
    h = _rmsnorm(x_ref[...], g_ref[...]) * (1.0 + sc_ref[...]) + sh_ref[...]
    hb = h.astype(BF16)

    def proj(i):
        return _dot(hb, w_ref[:, IN_SPLITS[i]:IN_SPLITS[i + 1]])

    u_ref[...] = proj(0)
    cos = cos_ref[...]
    sin = sin_ref[...]
    lane = lax.broadcasted_iota(jnp.int32, cos.shape, 1)
    first_half = (lane % HEAD_DIM) < (HEAD_DIM // 2)
    q = proj(1)
    for j in range(ATTN_W // LANES):
        sl = slice(j * LANES, (j + 1) * LANES)
        q_ref[:, sl] = (_rope2(q[:, sl], cos, sin, first_half) * (HEAD_DIM ** -0.5)).astype(BF16)
    k_ref[...] = _rope2(proj(2), cos, sin, first_half)
    v_ref[...] = proj(3)
    gs_ref[...] = proj(4)
    ga_ref[...] = proj(5)


def _inproj_call(x, sh1, sc1, norm_g, w_in_bf, cos_t, sin_t, tt):
    bx, tx, _ = x.shape
    nt = tx // tt
    per_tok = sh1.shape[1] != 1
    mod_spec = (pl.BlockSpec((None, tt, D_MODEL), lambda b, i: (b, i, 0)) if per_tok
                else pl.BlockSpec((None, 1, D_MODEL), lambda b, i: (b, 0, 0)))

    def tok_spec(w):
        return pl.BlockSpec((None, tt, w), lambda b, i: (b, i, 0))

    def tok_shape(w, dt):
        return jax.ShapeDtypeStruct((bx, tx, w), dt)

    return pl.pallas_call(
        _inproj_kernel,
        grid=(bx, nt),
        in_specs=[
            tok_spec(D_MODEL), mod_spec, mod_spec,
            pl.BlockSpec((1, D_MODEL), lambda b, i: (0, 0)),
            pl.BlockSpec((D_MODEL, IN_COLS), lambda b, i: (0, 0)),
            pl.BlockSpec((tt, LANES), lambda b, i: (i, 0)),
            pl.BlockSpec((tt, LANES), lambda b, i: (i, 0)),
        ],
        out_specs=[tok_spec(D_SSM), tok_spec(ATTN_W), tok_spec(KV_W), tok_spec(KV_W),
                   tok_spec(D_MODEL), tok_spec(D_MODEL)],
        out_shape=[tok_shape(D_SSM, F32), tok_shape(ATTN_W, BF16), tok_shape(KV_W, F32),
                   tok_shape(KV_W, F32), tok_shape(D_MODEL, F32), tok_shape(D_MODEL, F32)],
        compiler_params=_cparams("parallel", "parallel"),
        name="inproj",
    )(x, sh1, sc1, norm_g, w_in_bf, cos_t, sin_t)


def _ssm_kernel(u_ref, kt_ref, bd_ref, rv_ref, cd_ref, fw_ref, ap_ref, h0_ref, y_ref, s_ref,
                t_ref, bl_ref, cl_ref, *, nb, cb):
    nc = nb * cb
    sw = SSM_SW

    @pl.when(pl.program_id(1) == 0)
    def _():
        zero = jnp.zeros((LANES, LANES), BF16)
        for s in range(SSM_L):
            for t in range(SSM_L):
                t_ref[s * LANES:(s + 1) * LANES, t * LANES:(t + 1) * LANES] = kt_ref[t - s] if t >= s else zero
        bre = bd_ref[:, :sw]
        bim = bd_ref[:, sw:]
        for s in range(SSM_L):
            rre = rv_ref[s:s + 1, :sw]
            rim = rv_ref[s:s + 1, sw:]
            bl_ref[s * LANES:(s + 1) * LANES, :sw] = (bre * rre - bim * rim).astype(BF16)
            bl_ref[s * LANES:(s + 1) * LANES, sw:] = (bre * rim + bim * rre).astype(BF16)
        cre = cd_ref[:sw, :]
        cim = cd_ref[sw:, :]
        for t in range(SSM_L):
            fre = fw_ref[:sw, t:t + 1]
            fim = fw_ref[sw:, t:t + 1]
            cl_ref[:sw, t * LANES:(t + 1) * LANES] = (cre * fre - cim * fim).astype(BF16)
            cl_ref[sw:, t * LANES:(t + 1) * LANES] = (-(cre * fim + cim * fre)).astype(BF16)

    uf = jnp.concatenate([u_ref[pl.ds(s, nc, stride=SSM_L), :].astype(BF16) for s in range(SSM_L)], axis=1)
    y = _dot(uf, t_ref[...])
    v = _dot(uf, bl_ref[...])
    xre = v[:, :sw]
    xim = v[:, sw:]

    row = lax.broadcasted_iota(jnp.int32, (nc, sw), 0)
    cidx = row & (cb - 1)
    bidx = row // cb
    h0 = h0_ref[...]
    h0re = jnp.zeros((nc, sw), F32)
    h0im = jnp.zeros((nc, sw), F32)
    for b in range(nb):
        h0re = jnp.where(bidx == b, h0[b:b + 1, :sw], h0re)
        h0im = jnp.where(bidx == b, h0[b:b + 1, sw:], h0im)
    first = cidx == 0
    are = ap_ref[0:1, :sw]
    aim = ap_ref[0:1, sw:]
    xre = xre + jnp.where(first, are * h0re - aim * h0im, 0.0)
    xim = xim + jnp.where(first, are * h0im + aim * h0re, 0.0)

    d = 1
    k = 0
    while d < cb:
        are = ap_ref[k:k + 1, :sw]
        aim = ap_ref[k:k + 1, sw:]
        keep = cidx >= d
        sre = jnp.where(keep, pltpu.roll(xre, d, axis=0), 0.0)
        sim = jnp.where(keep, pltpu.roll(xim, d, axis=0), 0.0)
        xre, xim = xre + (are * sre - aim * sim), xim + (are * sim + aim * sre)
        d *= 2
        k += 1

    for b in range(nb):
        r = b * cb + cb - 1
        s_ref[b:b + 1, :sw] = xre[r:r + 1, :]
        s_ref[b:b + 1, sw:] = xim[r:r + 1, :]

    pre = jnp.where(first, h0re, pltpu.roll(xre, 1, axis=0))
    pim = jnp.where(first, h0im, pltpu.roll(xim, 1, axis=0))
    y = y + _dot(pre.astype(BF16), cl_ref[:sw, :]) + _dot(pim.astype(BF16), cl_ref[sw:, :])
    for t in range(SSM_L):
        y_ref[pl.ds(t, nc, stride=SSM_L), :] = y[:, t * LANES:(t + 1) * LANES]


def _ssm_call(u, mats, h0o, nb, cb):
    kt, bd, rv, cd, fw, ap = mats
    bx, tx, _ = u.shape

    def const_spec(a):
        return pl.BlockSpec((None,) + a.shape[1:], lambda o, b: (o,) + (0,) * (a.ndim - 1))

    tok_spec = pl.BlockSpec((None, tx, LANES), lambda o, b: (b, 0, o))
    st_spec = pl.BlockSpec((None, None, nb, 2 * SSM_SW), lambda o, b: (o, b, 0, 0))
    return pl.pallas_call(
        functools.partial(_ssm_kernel, nb=nb, cb=cb),
        grid=(SSM_OCTS, bx),
        in_specs=[tok_spec] + [const_spec(a) for a in (kt, bd, rv, cd, fw, ap)] + [st_spec],
        out_specs=[tok_spec, st_spec],
        out_shape=[jax.ShapeDtypeStruct((bx, tx, D_SSM), F32),
                   jax.ShapeDtypeStruct((SSM_OCTS, bx, nb, 2 * SSM_SW), F32)],
        scratch_shapes=[pltpu.VMEM((SSM_FLAT, SSM_FLAT), BF16), pltpu.VMEM((SSM_FLAT, 2 * SSM_SW), BF16),
                        pltpu.VMEM((2 * SSM_SW, SSM_FLAT), BF16)],
        compiler_params=_cparams("arbitrary", "arbitrary"),
        name="ssm",
    )(u, kt, bd, rv, cd, fw, ap, h0o)


def _oct_lanes(a):
    lead = a.shape[:-2]
    a = a.reshape(*lead, SSM_OCTS, SSM_SW)
    return jnp.moveaxis(a, -2, 0)


def _ssm_mats(a_re, a_im, log_dt, b_re, b_im, c_re, c_im):
    dt = jnp.exp(log_dt)[:, None]
    lre = a_re * dt
    lim = a_im * dt
    mag = jnp.exp(lre)
    ab_re = mag * jnp.cos(lim)
    ab_im = mag * jnp.sin(lim)
    den = a_re * a_re + a_im * a_im
    n_re = ab_re - 1.0
    f_re = (n_re * a_re + ab_im * a_im) / den
    f_im = (ab_im * a_re - n_re * a_im) / den
    bb_re = f_re[..., None] * b_re - f_im[..., None] * b_im
    bb_im = f_re[..., None] * b_im + f_im[..., None] * b_re

    def lam_pow(tau):
        tau = tau[:, None, None]
        m = jnp.exp(tau * lre)
        return m * jnp.cos(tau * lim), m * jnp.sin(tau * lim)

    pw_re, pw_im = lam_pow(jnp.arange(SSM_L + 1, dtype=F32))
    cp_re = c_re[None] * pw_re[:, :, None, :] - c_im[None] * pw_im[:, :, None, :]
    cp_im = c_re[None] * pw_im[:, :, None, :] + c_im[None] * pw_re[:, :, None, :]
    kern = jnp.sum(cp_re[..., None] * bb_re[None, :, None] - cp_im[..., None] * bb_im[None, :, None],
                   axis=3)
    eye = jnp.eye(SSM_OCT_G, dtype=F32)
    og = (SSM_OCTS, SSM_OCT_G)

    def slab_diag(m):
        lead = m.shape[:-3]
        a, b = m.shape[-2:]
        m = m.reshape(*lead, *og, a, b)
        m = m[..., :, None, :] * eye[:, None, :, None]
        return m.reshape(*lead, SSM_OCTS, SSM_OCT_G * a, SSM_OCT_G * b)

    kt = slab_diag(kern.transpose(0, 1, 3, 2)).astype(BF16).transpose(1, 0, 2, 3)
    bd = jnp.concatenate([slab_diag(bb_re.transpose(0, 2, 1)), slab_diag(bb_im.transpose(0, 2, 1))], axis=-1)
    rev = SSM_L - 1 - jnp.arange(SSM_L)
    rv = jnp.concatenate([_oct_lanes(pw_re[rev]), _oct_lanes(pw_im[rev])], axis=-1)
    cd = jnp.concatenate([slab_diag(c_re.transpose(0, 2, 1)), slab_diag(c_im.transpose(0, 2, 1))], axis=1)
    fw = jnp.concatenate([_oct_lanes(pw_re[1:]), _oct_lanes(pw_im[1:])], axis=-1).transpose(0, 2, 1)
    fw = jnp.pad(fw, ((0, 0), (0, 0), (0, LANES - SSM_L)))

    steps = float(SSM_L) * (2.0 ** jnp.arange(SSM_POW_ROWS, dtype=F32))
    ap_re, ap_im = lam_pow(steps)
    ap = jnp.concatenate([_oct_lanes(ap_re), _oct_lanes(ap_im)], axis=-1)
    return kt, bd, rv, cd, fw, ap


def _ssm_branch(u, mats, h0, nb, cb):
    bx = u.shape[0]
    h0o = jnp.concatenate([_oct_lanes(h0[..., 0]), _oct_lanes(h0[..., 1])], axis=-1)
    y, s = _ssm_call(u, mats, h0o.reshape(SSM_OCTS, bx, nb, 2 * SSM_SW), nb, cb)
    s = s.reshape(SSM_OCTS, bx * nb, 2, SSM_OCT_G, SSM_STATE).transpose(1, 0, 3, 4, 2)
    return y, s.reshape(bx * nb, SSM_GROUPS, SSM_STATE, 2)


def _attn_kernel(sink_ref, q_ref, ka_ref, kb_ref, va_ref, vb_ref, o_ref, *, banded):
    nk = 2 * LANES
    if banded:
        sub = 2 * CHUNK
        kfull = jnp.concatenate([ka_ref[...], kb_ref[...]], axis=0)
        vfull = jnp.concatenate([va_ref[...], vb_ref[...]], axis=0)
    else:
        sub = q_ref.shape[0]
        pad = jnp.zeros((nk - ka_ref.shape[0] - kb_ref.shape[0], KV_W), F32)
        kfull = jnp.concatenate([ka_ref[...], kb_ref[...], pad], axis=0)
        vfull = jnp.concatenate([va_ref[...], vb_ref[...], pad], axis=0)
    nsub = q_ref.shape[0] // sub

    r_i = lax.broadcasted_iota(jnp.int32, (sub, nk), 0)
    c_i = lax.broadcasted_iota(jnp.int32, (sub, nk), 1)
    if banded:
        lo = (r_i // CHUNK) * CHUNK
        valid = (c_i >= lo) & (c_i < lo + WINDOW + CHUNK)
        first_frame = jnp.where(pl.program_id(1) > 0, 0, WINDOW)
        valids = [valid & (c_i >= first_frame)] + [valid] * (nsub - 1)
    else:
        valids = [c_i < (ka_ref.shape[0] + kb_ref.shape[0])]

    lane = lax.broadcasted_iota(jnp.int32, kfull.shape, 1)
    low = lane < HEAD_DIM
    kroll = pltpu.roll(kfull, HEAD_DIM, axis=1)
    vroll = pltpu.roll(vfull, HEAD_DIM, axis=1)

    for g in range(N_KV_HEADS):
        k_lo, k_hi = (kfull, kroll) if g == 0 else (kroll, kfull)
        v_lo, v_hi = (vfull, vroll) if g == 0 else (vroll, vfull)
        kpad = (jnp.where(low, k_lo, 0.0).astype(BF16), jnp.where(low, 0.0, k_hi).astype(BF16))
        vpad = (jnp.where(low, v_lo, 0.0).astype(BF16), jnp.where(low, 0.0, v_hi).astype(BF16))
        for b in range(nsub):
            rs = slice(b * sub, (b + 1) * sub)
            ks = slice(b * sub, b * sub + nk)
            for pp in range(2):
                slab = 2 * g + pp
                qp = q_ref[rs, slab * LANES:(slab + 1) * LANES]
                acc = None
                for hh in range(2):
                    sk = sink_ref[2 * slab + hh]
                    s = jnp.where(valids[b], _dot_nt(qp, kpad[hh][ks]), NEG_INF)
                    m = jnp.maximum(jnp.max(s, axis=-1, keepdims=True), sk)
                    p = jnp.exp(s - m)
                    den = jnp.sum(p, axis=-1, keepdims=True) + jnp.exp(sk - m)
                    o = _dot(p.astype(BF16), vpad[hh][ks]) / den
                    acc = o if acc is None else acc + o
                o_ref[rs, slab * LANES:(slab + 1) * LANES] = acc.astype(BF16)


def _attn_call(sink, q, ka, kb, va, vb, banded):
    bx, tx, _ = q.shape
    if banded:
        rows = ATTN_ROWS
        nt = tx // rows
        grid = (bx, nt)
        q_spec = pl.BlockSpec((None, rows, ATTN_W), lambda b, i: (b, i, 0))
        prev = pl.BlockSpec((None, WINDOW, KV_W), lambda b, i: (b, jnp.maximum(i * (rows // WINDOW) - 1, 0), 0))
        cur = pl.BlockSpec((None, rows, KV_W), lambda b, i: (b, i, 0))
    else:
        rows = tx
        grid = (bx, 1)
        q_spec = pl.BlockSpec((None, rows, ATTN_W), lambda b, i: (b, 0, 0))
        prev = pl.BlockSpec((None, ka.shape[1], KV_W), lambda b, i: (b, 0, 0))
        cur = pl.BlockSpec((None, rows, KV_W), lambda b, i: (b, 0, 0))
    return pl.pallas_call(
        functools.partial(_attn_kernel, banded=banded),
        grid=grid,
        in_specs=[pl.BlockSpec(memory_space=pltpu.SMEM), q_spec, prev, cur, prev, cur],
        out_specs=q_spec,
        out_shape=jax.ShapeDtypeStruct((bx, tx, ATTN_W), BF16),
        compiler_params=_cparams("parallel", "parallel"),
        name="attn",
    )(sink, q, ka, kb, va, vb)


def _route(logits_t, bias_col, topi_ref, topw_ref):
    tt = logits_t.shape[1]
    scores = jax.nn.sigmoid(logits_t)
    biased = scores + bias_col
    sub = lax.broadcasted_iota(jnp.int32, (EXPERTS_PER_GROUP, tt), 0).astype(F32)
    ninf = float('-inf')
    blocks = [biased[EXPERTS_PER_GROUP * g:EXPERTS_PER_GROUP * (g + 1), :] for g in range(N_EXPERT_GROUPS)]

    gscore = jnp.zeros((N_EXPERT_GROUPS, tt), F32)
    for g in range(N_EXPERT_GROUPS):
        blk = blocks[g]
        m1 = jnp.max(blk, axis=0, keepdims=True)
        i1 = jnp.min(jnp.where(blk == m1, sub, float(EXPERTS_PER_GROUP)), axis=0, keepdims=True)
        m2 = jnp.max(jnp.where(sub == i1, ninf, blk), axis=0, keepdims=True)
        gscore = jnp.where(sub == float(g), jnp.broadcast_to(m1 + m2, gscore.shape), gscore)

    grank = jnp.zeros((N_EXPERT_GROUPS, tt), F32)
    for j in range(N_EXPERT_GROUPS):
        rj = jnp.broadcast_to(gscore[j:j + 1, :], gscore.shape)
        beats = (rj > gscore) | ((rj == gscore) & (sub > float(j)))
        grank = grank + jnp.where(beats, 1.0, 0.0)
    gsel = jnp.where(grank < float(TOPK_GROUPS), 1.0, 0.0)

    masked = []
    for g in range(N_EXPERT_GROUPS):
        on = jnp.broadcast_to(gsel[g:g + 1, :], blocks[g].shape) > 0.5
        masked.append(jnp.where(on, blocks[g], NEG_INF))

    ranks = [jnp.zeros((EXPERTS_PER_GROUP, tt), F32) for _ in range(N_EXPERT_GROUPS)]
    for jb in range(N_EXPERT_GROUPS):
        for jj in range(EXPERTS_PER_GROUP):
            rj = jnp.broadcast_to(masked[jb][jj:jj + 1, :], (EXPERTS_PER_GROUP, tt))
            for ib in range(N_EXPERT_GROUPS):
                mi = masked[ib]
                if ib < jb:
                    beats = rj > mi
                elif ib > jb:
                    beats = rj >= mi
                else:
                    beats = (rj > mi) | ((rj == mi) & (sub > float(jj)))
                ranks[ib] = ranks[ib] + jnp.where(beats, 1.0, 0.0)

    topi = jnp.zeros((TOP_K, tt), F32)
    topw = jnp.zeros((TOP_K, tt), F32)
    for r in range(TOP_K):
        ai = jnp.zeros((EXPERTS_PER_GROUP, tt), F32)
        aw = jnp.zeros((EXPERTS_PER_GROUP, tt), F32)
        for g in range(N_EXPERT_GROUPS):
            hit = ranks[g] == float(r)
            ai = ai + jnp.where(hit, sub + float(EXPERTS_PER_GROUP * g), 0.0)
            aw = aw + jnp.where(hit, scores[EXPERTS_PER_GROUP * g:EXPERTS_PER_GROUP * (g + 1), :], 0.0)
        ir = jnp.sum(ai, axis=0, keepdims=True)
        wr = jnp.sum(aw, axis=0, keepdims=True)
        topi = jnp.where(sub == float(r), jnp.broadcast_to(ir, topi.shape), topi)
        topw = jnp.where(sub == float(r), jnp.broadcast_to(wr, topw.shape), topw)
    den = jnp.sum(topw, axis=0, keepdims=True)
    topi_ref[...] = topi.astype(jnp.int32)
    topw_ref[...] = topw / den * ROUTED_SCALE


def _post_kernel(y_ref, u_ref, a_ref, gs_ref, ga_ref, x_ref, g1_ref, sh2_ref, sc2_ref, g2_ref,
                 d_ref, wglu_ref, bglu_ref, wbs_ref, wba_ref, wout_ref, nf_ref, rwt_ref, rb_ref,
                 swg_ref, swu_ref, swd_ref, xp_ref, h2s_ref, topi_ref, topw_ref):
    ys = y_ref[...] + d_ref[...] * u_ref[...]
    z = jax.nn.gelu(ys)
    y2 = z * jax.nn.sigmoid(_dot(z.astype(BF16), wglu_ref[...]) + bglu_ref[...])
    merged = (jax.nn.sigmoid(gs_ref[...]) * _dot(y2.astype(BF16), wbs_ref[...])
              + jax.nn.sigmoid(ga_ref[...]) * _dot(a_ref[...], wba_ref[...]))
    x1 = x_ref[...] + g1_ref[...] * _dot(merged.astype(BF16), wout_ref[...])

    h2 = _rmsnorm(x1, nf_ref[...]) * (1.0 + sc2_ref[...]) + sh2_ref[...]
    tt = h2.shape[0]
    for j in range(SLAB_ROWS):
        h2s_ref[pl.ds(j, tt, stride=SLAB_ROWS), :] = h2[:, j * LANES:(j + 1) * LANES]
    hh, hl = _split(h2)
    rh, rl = _split(rwt_ref[...])
    logits_t = _dot_nt(rh, hh) + (_dot_nt(rh, hl) + _dot_nt(rl, hh))
    _route(logits_t, rb_ref[...], topi_ref, topw_ref)

    shared = _dot((jax.nn.silu(_dot(hh, swg_ref[...])) * _dot(hh, swu_ref[...])).astype(BF16), swd_ref[...])
    xp_ref[...] = x1 + g2_ref[...] * shared


def _post_call(y, u, attn, gs, ga, x, mods, lw, tt):
    bx, tx, _ = x.shape
    nt = tx // tt
    per_tok = mods[0].shape[1] != 1
    mod_spec = (pl.BlockSpec((None, tt, D_MODEL), lambda b, i: (b, i, 0)) if per_tok
                else pl.BlockSpec((None, 1, D_MODEL), lambda b, i: (b, 0, 0)))

    def tok_spec(w):
        return pl.BlockSpec((None, tt, w), lambda b, i: (b, i, 0))

    def full(a):
        return pl.BlockSpec(a.shape, lambda b, i: (0,) * a.ndim)

    weights = [lw['ssm_d'], lw['w_glu'], lw['b_glu'], lw['w_bs'], lw['w_ba'], lw['w_out'], lw['norm_ffn_g'],
               lw['router_wt'], lw['router_bias'], lw['sh_wg'], lw['sh_wu'], lw['sh_wd']]
    return pl.pallas_call(
        _post_kernel,
        grid=(bx, nt),
        in_specs=[tok_spec(D_SSM), tok_spec(D_SSM), tok_spec(ATTN_W), tok_spec(D_MODEL), tok_spec(D_MODEL),
                  tok_spec(D_MODEL), mod_spec, mod_spec, mod_spec, mod_spec] + [full(w) for w in weights],
        out_specs=[tok_spec(D_MODEL),
                   pl.BlockSpec((tt * SLAB_ROWS, LANES), lambda b, i: (b * nt + i, 0)),
                   pl.BlockSpec((TOP_K, tt), lambda b, i: (0, b * nt + i)),
                   pl.BlockSpec((TOP_K, tt), lambda b, i: (0, b * nt + i))],
        out_shape=[jax.ShapeDtypeStruct((bx, tx, D_MODEL), F32),
                   jax.ShapeDtypeStruct((bx * tx * SLAB_ROWS, LANES), F32),
                   jax.ShapeDtypeStruct((TOP_K, bx * tx), jnp.int32),
                   jax.ShapeDtypeStruct((TOP_K, bx * tx), F32)],
        compiler_params=_cparams("parallel", "parallel"),
        name="post",
    )(y, u, attn, gs, ga, x, *mods, *weights)


def _moe_kernel(ptab_ref, ktab_ref, elist_ref, meta_ref, hs_ref, word_hbm, gate_hbm, wg_hbm, wu_hbm, wd_hbm, o_ref,
                word_s, gate_s, xt0, xt1, yt0, yt1, wg_buf, wu_buf, wd_buf, lsem, wsem, *, tm, bs, ptab_w, layer):
    i = pl.program_id(0)
    stride = bs + SUBLANES
    nslot = word_s.shape[0]
    ngate = gate_s.shape[0]
    acc_ref = o_ref
    slab_mask = -SLAB_ROWS
    gather_mask = (tm * SLAB_ROWS - 1) & slab_mask
    tab0 = i * ptab_w + 1
    npairs = meta_ref[2 * i]
    nrank = meta_ref[2 * i + 1]

    def list_copies():
        return (pltpu.make_async_copy(word_hbm.at[pl.ds(pl.multiple_of(i * nslot, LANES), nslot)],
                                      word_s, lsem.at[0]),
                pltpu.make_async_copy(gate_hbm.at[pl.ds(pl.multiple_of(i * ngate, LANES), ngate)],
                                      gate_s, lsem.at[1]))

    def weight_copies(k):
        ex = elist_ref[i * N_EXPERTS + k]
        slot = k % MOE_WSLOTS
        return (pltpu.make_async_copy(wg_hbm.at[layer, ex], wg_buf.at[slot], wsem.at[slot, 0]),
                pltpu.make_async_copy(wu_hbm.at[layer, ex], wu_buf.at[slot], wsem.at[slot, 1]),
                pltpu.make_async_copy(wd_hbm.at[layer, ex], wd_buf.at[slot], wsem.at[slot, 2]))

    def gather(base, xt_ref):
        for r in range(bs):
            row = pl.multiple_of(word_s[base + r] & gather_mask, SLAB_ROWS)
            xt_ref[pl.ds(r, SLAB_ROWS, stride=stride), :] = hs_ref[pl.ds(row, SLAB_ROWS), :]

    def expert(xt_ref, yt_ref, k):
        slot = k % MOE_WSLOTS
        x = jnp.concatenate([xt_ref[j * stride:j * stride + bs, :] for j in range(SLAB_ROWS)],
                            axis=1).astype(BF16)
        act = jax.nn.silu(_dot(x, wg_buf[slot])) * _dot(x, wu_buf[slot])
        y = _dot(act.astype(BF16), wd_buf[slot])
        for j in range(SLAB_ROWS):
            yt_ref[j * stride:j * stride + bs, :] = y[:, j * LANES:(j + 1) * LANES]

    def scatter(base, yt_ref):
        for r0 in range(0, bs, MOE_RMW_UNROLL):
            upd = []
            for r in range(r0, r0 + MOE_RMW_UNROLL):
                word = word_s[base + r]
                row = pl.multiple_of(word & slab_mask, SLAB_ROWS)
                upd.append((row, acc_ref[pl.ds(row, SLAB_ROWS), :]
                            + gate_s[word] * yt_ref[pl.ds(r, SLAB_ROWS, stride=stride), :]))
            for row, val in upd:
                acc_ref[pl.ds(row, SLAB_ROWS), :] = val

    for c in list_copies():
        c.start()
    ahead = MOE_WSLOTS - 2
    for r in range(ahead + 1):
        @pl.when(r < nrank)
        def _():
            for c in weight_copies(r):
                c.start()

    acc_ref[...] = jnp.zeros_like(acc_ref)
    yt1[...] = jnp.zeros_like(yt1)
    for c in list_copies():
        c.wait()
    gather(ptab_ref[tab0], xt0)

    def pair(j, carry):
        waited, started = carry
        t = tab0 + 2 * j
        k0 = ktab_ref[t]
        k1 = ktab_ref[t + 1]

        target = jnp.minimum(k1 + ahead, nrank - 1)
        for d in (1, 2):
            @pl.when(started + d <= target)
            def _():
                for c in weight_copies(started + d):
                    c.start()

        for d in (1, 2):
            @pl.when(waited + d <= k1)
            def _():
                for c in weight_copies(waited + d):
                    c.wait()

        gather(ptab_ref[t + 1], xt1)
        expert(xt0, yt0, k0)
        scatter(ptab_ref[t - 1], yt1)
        gather(ptab_ref[t + 2], xt0)
        expert(xt1, yt1, k1)
        scatter(ptab_ref[t], yt0)
        return jnp.maximum(waited, k1), jnp.maximum(started, target)

    lax.fori_loop(0, npairs, pair, (jnp.int32(-1), jnp.minimum(jnp.int32(ahead), nrank - 1)))
    scatter(ptab_ref[tab0 + 2 * npairs - 1], yt1)


def _moe_call(h2s, topi, topw, wg, wu, wd, layer, tm):
    n = h2s.shape[0] // SLAB_ROWS
    nt = n // tm
    nseg = nt * N_EXPERTS
    nreal = tm * TOP_K
    bs = MOE_BLOCK_LARGE if nreal // N_EXPERTS >= 2 * MOE_BLOCK_LARGE else MOE_BLOCK_SMALL
    npad = MOE_PAD
    nslot = nreal + (N_EXPERTS + 1) * npad
    null_base = nslot - npad
    ptab_w = nreal // bs + N_EXPERTS + 4

    i32 = jnp.int32
    experts = jnp.arange(N_EXPERTS, dtype=i32)
    counts = jnp.sum((topi.reshape(TOP_K, nt, tm)[..., None] == experts).astype(i32), axis=(0, 2))
    nblk = (counts + bs - 1) // bs
    first = jnp.cumsum(counts + npad, axis=1) - (counts + npad)
    pend = jnp.cumsum(nblk, axis=1)
    pstart = pend - nblk
    nonempty = (nblk > 0).astype(i32)
    krank = jnp.cumsum(nonempty, axis=1) - 1
    nrank = jnp.sum(nonempty, axis=1)
    elist = jnp.sum(jnp.where((nonempty[:, None, :] == 1) & (krank[:, None, :] == experts[None, :, None]),
                              experts[None, None, :], 0), axis=-1).astype(i32)
    pos = jnp.arange(ptab_w, dtype=i32) - 1
    e_of = jnp.sum((pend[:, None, :] <= pos[None, :, None]).astype(i32), axis=-1)
    e_cl = jnp.minimum(e_of, N_EXPERTS - 1)
    q = pos[None, :] - jnp.take_along_axis(pstart, e_cl, axis=1)
    real_blk = (pos[None, :] >= 0) & (e_of < N_EXPERTS)
    ptab = jnp.where(real_blk, jnp.take_along_axis(first, e_cl, axis=1) + q * bs, null_base).astype(i32)
    ktab = jnp.where(real_blk, jnp.take_along_axis(krank, e_cl, axis=1), nrank[:, None] - 1).astype(i32)
    meta = jnp.stack([(pend[:, -1] + 1) // 2, nrank], axis=1).astype(i32)

    assert TOP_K == SLAB_ROWS and nreal < (1 << MOE_WORD_BITS) and 2 * nseg + 2 < (1 << (31 - MOE_WORD_BITS))
    slot = jnp.arange(n * TOP_K, dtype=i32)
    key = 2 * ((slot // nreal) * N_EXPERTS + topi.T.reshape(n * TOP_K))
    pad_key = jnp.concatenate([jnp.repeat(2 * jnp.arange(nseg, dtype=i32) + 1, npad),
                               jnp.repeat(2 * (jnp.arange(nt, dtype=i32) * N_EXPERTS + N_EXPERTS - 1) + 1, npad)])
    packed = jnp.concatenate([(key << MOE_WORD_BITS) | (slot % nreal), (pad_key << MOE_WORD_BITS) | nreal])
    word_s = lax.sort(packed, is_stable=False) & ((1 << MOE_WORD_BITS) - 1)
    ngate = nreal + LANES
    gate_s = jnp.concatenate([topw.T.reshape(nt, nreal), jnp.zeros((nt, LANES), F32)], axis=1).reshape(nt * ngate)

    stage = pltpu.VMEM((SLAB_ROWS * (bs + SUBLANES), LANES), F32)
    hbm = pl.BlockSpec(memory_space=pl.ANY)
    grid_spec = pltpu.PrefetchScalarGridSpec(
        num_scalar_prefetch=4,
        grid=(nt,),
        in_specs=[
            pl.BlockSpec((tm * SLAB_ROWS, LANES), lambda i, *_: (i, 0), pipeline_mode=pl.Buffered(1)),
            hbm, hbm, hbm, hbm, hbm,
        ],
        out_specs=pl.BlockSpec((None, (tm + 1) * SLAB_ROWS, LANES), lambda i, *_: (i, 0, 0),
                               pipeline_mode=pl.Buffered(1)),
        scratch_shapes=[
            pltpu.SMEM((nslot,), i32),
            pltpu.SMEM((ngate,), F32),
            stage, stage, stage, stage,
            pltpu.VMEM((MOE_WSLOTS, D_MODEL, D_EXPERT), BF16),
            pltpu.VMEM((MOE_WSLOTS, D_MODEL, D_EXPERT), BF16),
            pltpu.VMEM((MOE_WSLOTS, D_EXPERT, D_MODEL), BF16),
            pltpu.SemaphoreType.DMA((2,)),
            pltpu.SemaphoreType.DMA((MOE_WSLOTS, 3)),
        ],
    )
    return pl.pallas_call(
        functools.partial(_moe_kernel, tm=tm, bs=bs, ptab_w=ptab_w, layer=layer),
        grid_spec=grid_spec,
        out_shape=jax.ShapeDtypeStruct((nt, (tm + 1) * SLAB_ROWS, LANES), F32),
        compiler_params=_cparams("arbitrary"),
        name="moe",
    )(ptab.reshape(nt * ptab_w), ktab.reshape(nt * ptab_w), elist.reshape(nseg), meta.reshape(nt * 2),
      h2s, word_s, gate_s, wg, wu, wd)


def _combine_kernel(r_ref, xp_ref, g2_ref, fg_ref, o_ref, *, final):
    tt = xp_ref.shape[0]
    routed = jnp.concatenate([r_ref[pl.ds(j, tt, stride=SLAB_ROWS), :] for j in range(SLAB_ROWS)], axis=1)
    x2 = xp_ref[...] + g2_ref[...] * routed
    if final:
        x2 = _rmsnorm(x2, fg_ref[...])
    o_ref[...] = x2


def _combine_call(routed_s, xp, g2, final_g, tt, final):
    bx, tx, _ = xp.shape
    nt = tx // tt
    per_tok = g2.shape[1] != 1
    g2_spec = (pl.BlockSpec((None, tt, D_MODEL), lambda b, i: (b, i, 0)) if per_tok
               else pl.BlockSpec((None, 1, D_MODEL), lambda b, i: (b, 0, 0)))
    tok_spec = pl.BlockSpec((None, tt, D_MODEL), lambda b, i: (b, i, 0))
    per_tile = (routed_s.shape[1] // SLAB_ROWS - 1) // tt
    routed_spec = pl.BlockSpec((None, tt * SLAB_ROWS, LANES),
                               lambda b, i: ((b * nt + i) // per_tile, (b * nt + i) % per_tile, 0))
    return pl.pallas_call(
        functools.partial(_combine_kernel, final=final),
        grid=(bx, nt),
        in_specs=[routed_spec, tok_spec, g2_spec,
                  pl.BlockSpec((1, D_MODEL), lambda b, i: (0, 0))],
        out_specs=tok_spec,
        out_shape=jax.ShapeDtypeStruct((bx, tx, D_MODEL), F32),
        compiler_params=_cparams("parallel", "parallel"),
        name="combine",
    )(routed_s, xp, g2, final_g)


def _rope_tables(pos):
    half = HEAD_DIM // 2
    inv_freq = ROPE_THETA ** (-jnp.arange(half, dtype=F32) / half)
    ang = pos.astype(F32)[:, None] * inv_freq[None, :]
    cos = jnp.cos(ang)
    sin = jnp.sin(ang)
    return jnp.tile(cos, (1, 4)), jnp.tile(jnp.concatenate([-sin, sin], axis=1), (1, 2))


def _trunk(x, mods, pos, past_k, past_v, ssm_h0, layers, experts, final_g, tt, tm, tok_batches):
    bx, tx, _ = x.shape
    b, t = tok_batches
    cos_t, sin_t = _rope_tables(pos)
    new_k, new_v, new_h = [], [], []
    for l, lw in enumerate(layers):
        sh1, sc1, g1, sh2, sc2, g2 = mods[l]
        u, q, k, v, gs, ga = _inproj_call(x, sh1, sc1, lw['norm_attn_g'], lw['w_in'], cos_t, sin_t, tt)
        ks = k.reshape(b, t, KV_W)
        vs = v.reshape(b, t, KV_W)
        qs = q.reshape(b, t, ATTN_W)
        if past_k is None:
            attn = _attn_call(lw['sink'], qs, ks, ks, vs, vs, True)
            new_k.append(ks[:, -WINDOW:].reshape(b, WINDOW, N_KV_HEADS, HEAD_DIM))
            new_v.append(vs[:, -WINDOW:].reshape(b, WINDOW, N_KV_HEADS, HEAD_DIM))
            h0 = jnp.zeros((b, SSM_GROUPS, SSM_STATE, 2), F32)
        else:
            pk = past_k[l].reshape(b, -1, KV_W)
            pv = past_v[l].reshape(b, -1, KV_W)
            attn = _attn_call(lw['sink'], qs, pk, ks, pv, vs, False)
            new_k.append(ks.reshape(b, t, N_KV_HEADS, HEAD_DIM))
            new_v.append(vs.reshape(b, t, N_KV_HEADS, HEAD_DIM))
            h0 = ssm_h0[l]
        y, h_last = _ssm_branch(u, lw['ssm_mats'], h0, b // bx, t // SSM_L)
        new_h.append(h_last)
        xp, h2s, topi, topw = _post_call(y, u, attn.reshape(bx, tx, ATTN_W), gs, ga, x,
                                         (g1, sh2, sc2, g2), lw, tt)
        routed_s = _moe_call(h2s, topi, topw, *experts, l, tm)
        x = _combine_call(routed_s, xp, g2, final_g, tt, l == len(layers) - 1)
    return x, jnp.stack(new_k), jnp.stack(new_v), jnp.stack(new_h)


def kernel(x_prompt, x_sample, cache_k, cache_v, state_ssm, c_prompt, c_sample, ada_w, ada_b, norm_attn_g,
           norm_ffn_g, w_in, ssm_a_re, ssm_a_im, ssm_log_dt, ssm_b_re, ssm_b_im, ssm_c_re, ssm_c_im, ssm_d,
           ssm_w_glu, ssm_b_glu, attn_sink, w_branch_ssm, w_branch_attn, w_out, router_w, router_bias,
           expert_w_gate, expert_w_up, expert_w_down, shared_w_gate, shared_w_up, shared_w_down, final_g):
    depth = ada_w.shape[0]
    bp, tp, _ = x_prompt.shape
    bs, ts, _ = x_sample.shape

    layers = []
    for l in range(depth):
        layers.append({
            'norm_attn_g': norm_attn_g[l][None], 'norm_ffn_g': norm_ffn_g[l][None],
            'w_in': w_in[l].astype(BF16), 'sink': attn_sink[l],
            'ssm_mats': _ssm_mats(ssm_a_re[l], ssm_a_im[l], ssm_log_dt[l], ssm_b_re[l], ssm_b_im[l],
                                  ssm_c_re[l], ssm_c_im[l]),
            'ssm_d': ssm_d[l][None], 'w_glu': ssm_w_glu[l].astype(BF16), 'b_glu': ssm_b_glu[l][None],
            'w_bs': w_branch_ssm[l].astype(BF16), 'w_ba': w_branch_attn[l].astype(BF16),
            'w_out': w_out[l].astype(BF16), 'router_wt': router_w[l].T, 'router_bias': router_bias[l][:, None],
            'sh_wg': shared_w_gate[l].astype(BF16), 'sh_wu': shared_w_up[l].astype(BF16),
            'sh_wd': shared_w_down[l].astype(BF16),
        })
    experts = (expert_w_gate.astype(BF16), expert_w_up.astype(BF16), expert_w_down.astype(BF16))
    fg = final_g[None]

    rows = 16
    c_all = jnp.concatenate([c_prompt, c_sample, jnp.zeros((rows - bp - bs, D_MODEL), F32)], axis=0)
    mod = _mod_call(c_all, ada_w, ada_b).reshape(depth, rows, 6, D_MODEL)
    mods_p = [[mod[l, :bp, j][:, None, :] for j in range(6)] for l in range(depth)]
    mods_s = [[jnp.repeat(mod[l, bp:bp + bs, j], ts, axis=0)[None] for j in range(6)] for l in range(depth)]

    pos_p = jnp.arange(tp, dtype=jnp.int32)
    pos_s = jnp.tile(PAST_LEN + jnp.arange(ts, dtype=jnp.int32), bs)

    y_p, k_p, v_p, h_p = _trunk(x_prompt, mods_p, pos_p, None, None, None, layers, experts, fg,
                                tt=512, tm=min(4096, bp * tp), tok_batches=(bp, tp))
    y_s, k_s, v_s, h_s = _trunk(x_sample.reshape(1, bs * ts, D_MODEL), mods_s, pos_s, cache_k, cache_v,
                                state_ssm, layers, experts, fg, tt=bs * ts, tm=bs * ts, tok_batches=(bs, ts))
    return (y_p, y_s.reshape(bs, ts, D_MODEL), k_p, v_p, h_p, k_s, v_s, h_s)
```

```python
import functools

import jax
import jax.numpy as jnp
from jax import lax
from jax.experimental import pallas as pl
from jax.experimental.pallas import tpu as pltpu

F32 = jnp.float32
BF16 = jnp.bfloat16

D_MODEL = 1024
DEPTH = 2
PAST_LEN = 4096
CHUNK = 64
N_HEADS = 8
N_KV_HEADS = 2
HEAD_DIM = 64
WINDOW = 128
ROPE_THETA = 10000.0
D_SSM = 512
SSM_GROUP_CH = 16
SSM_GROUPS = D_SSM // SSM_GROUP_CH
SSM_STATE = 64
N_EXPERTS = 64
N_EXPERT_GROUPS = 8
EXPERTS_PER_GROUP = N_EXPERTS // N_EXPERT_GROUPS
TOPK_GROUPS = 4
TOP_K = 8
D_EXPERT = 256
D_SHARED = 256
ROUTED_SCALE = 2.5
RMS_EPS = 1e-6
NEG_INF = -1e30
ATTN_W = N_HEADS * HEAD_DIM
KV_W = N_KV_HEADS * HEAD_DIM
IN_COLS = D_SSM + ATTN_W + 2 * KV_W + 2 * D_MODEL
IN_SPLITS = (0, D_SSM, D_SSM + ATTN_W, D_SSM + ATTN_W + KV_W, D_SSM + ATTN_W + 2 * KV_W,
             D_SSM + ATTN_W + 2 * KV_W + D_MODEL, IN_COLS)

LANES = 128
SSM_L = 16
SSM_OCT_G = LANES // SSM_GROUP_CH
SSM_OCTS = SSM_GROUPS // SSM_OCT_G
SSM_FLAT = SSM_L * LANES
SSM_SW = SSM_OCT_G * SSM_STATE
SSM_POW_ROWS = 16
VMEM_LIMIT = 56 * 1024 * 1024
SUBLANES = 8
ATTN_ROWS = 512
SLAB_ROWS = D_MODEL // LANES
MOE_BLOCK_LARGE = 192
MOE_BLOCK_SMALL = 128
MOE_PAD = 256
MOE_RMW_UNROLL = 4
MOE_WSLOTS = 4
MOE_WORD_BITS = 16


def _cparams(*sem):
    return pltpu.CompilerParams(dimension_semantics=sem, vmem_limit_bytes=VMEM_LIMIT)


def _dot(a, b):
    return jnp.dot(a, b, preferred_element_type=F32)


def _dot_nt(a, b):
    return lax.dot_general(a, b, (((1,), (1,)), ((), ())), preferred_element_type=F32)


def _split(x):
    hi = x.astype(BF16)
    lo = (x - hi.astype(F32)).astype(BF16)
    return hi, lo


def _rmsnorm(x, g):
    return x * lax.rsqrt(jnp.mean(x * x, axis=-1, keepdims=True) + RMS_EPS) * g


def _mod_kernel(c_ref, w_ref, b_ref, o_ref):
    cond = jax.nn.silu(c_ref[...])
    ch, cl = _split(cond)
    wh, wl = _split(w_ref[...])
    o_ref[...] = _dot(ch, wh) + (_dot(ch, wl) + _dot(cl, wh)) + b_ref[...]


def _mod_call(c_all, ada_w, ada_b):
    depth = ada_w.shape[0]
    rows = c_all.shape[0]
    nj = 6
    return pl.pallas_call(
        _mod_kernel,
        grid=(depth, nj),
        in_specs=[
            pl.BlockSpec((rows, D_MODEL), lambda l, j: (0, 0)),
            pl.BlockSpec((None, D_MODEL, D_MODEL), lambda l, j: (l, 0, j)),
            pl.BlockSpec((None, 1, D_MODEL), lambda l, j: (l, 0, j)),
        ],
        out_specs=pl.BlockSpec((None, rows, D_MODEL), lambda l, j: (l, 0, j)),
        out_shape=jax.ShapeDtypeStruct((depth, rows, nj * D_MODEL), F32),
        compiler_params=_cparams("parallel", "parallel"),
        name="mod",
    )(c_all, ada_w, ada_b.reshape(depth, 1, nj * D_MODEL))


def _rope2(t, cos, sin_signed, first_half):
    swapped = jnp.where(first_half, pltpu.roll(t, LANES - HEAD_DIM // 2, axis=1),
                        pltpu.roll(t, HEAD_DIM // 2, axis=1))
    return t * cos + swapped * sin_signed


def _inproj_kernel(x_ref, sh_ref, sc_ref, g_ref, w_ref, cos_ref, sin_ref,
                   u_ref, q_ref, k_ref, v_ref, gs_ref, ga_ref):
    h = _rmsnorm(x_ref[...], g_ref[...]) * (1.0 + sc_ref[...]) + sh_ref[...]
    hb = h.astype(BF16)

    def proj(i):
        return _dot(hb, w_ref[:, IN_SPLITS[i]:IN_SPLITS[i + 1]])

    u_ref[...] = proj(0)
    cos = cos_ref[...]
    sin = sin_ref[...]
    lane = lax.broadcasted_iota(jnp.int32, cos.shape, 1)
    first_half = (lane % HEAD_DIM) < (HEAD_DIM // 2)
    q = proj(1)
    for j in range(ATTN_W // LANES):
        sl = slice(j * LANES, (j + 1) * LANES)
        q_ref[:, sl] = (_rope2(q[:, sl], cos, sin, first_half) * (HEAD_DIM ** -0.5)).astype(BF16)
    k_ref[...] = _rope2(proj(2), cos, sin, first_half)
    v_ref[...] = proj(3)
    gs_ref[...] = proj(4)
    ga_ref[...] = proj(5)


def _inproj_call(x, sh1, sc1, norm_g, w_in_bf, cos_t, sin_t, tt):
    bx, tx, _ = x.shape
    nt = tx // tt
    per_tok = sh1.shape[1] != 1
    mod_spec = (pl.BlockSpec((None, tt, D_MODEL), lambda b, i: (b, i, 0)) if per_tok
                else pl.BlockSpec((None, 1, D_MODEL), lambda b, i: (b, 0, 0)))

    def tok_spec(w):
        return pl.BlockSpec((None, tt, w), lambda b, i: (b, i, 0))

    def tok_shape(w, dt):
        return jax.ShapeDtypeStruct((bx, tx, w), dt)

    return pl.pallas_call(
        _inproj_kernel,
        grid=(bx, nt),
        in_specs=[
            tok_spec(D_MODEL), mod_spec, mod_spec,
            pl.BlockSpec((1, D_MODEL), lambda b, i: (0, 0)),
            pl.BlockSpec((D_MODEL, IN_COLS), lambda b, i: (0, 0)),
            pl.BlockSpec((tt, LANES), lambda b, i: (i, 0)),
            pl.BlockSpec((tt, LANES), lambda b, i: (i, 0)),
        ],
        out_specs=[tok_spec(D_SSM), tok_spec(ATTN_W), tok_spec(KV_W), tok_spec(KV_W),
                   tok_spec(D_MODEL), tok_spec(D_MODEL)],
        out_shape=[tok_shape(D_SSM, F32), tok_shape(ATTN_W, BF16), tok_shape(KV_W, F32),
                   tok_shape(KV_W, F32), tok_shape(D_MODEL, F32), tok_shape(D_MODEL, F32)],
        compiler_params=_cparams("parallel", "parallel"),
        name="inproj",
    )(x, sh1, sc1, norm_g, w_in_bf, cos_t, sin_t)


def _ssm_kernel(u_ref, kt_ref, bd_ref, rv_ref, cd_ref, fw_ref, ap_ref, h0_ref, y_ref, s_ref,
                t_ref, bl_ref, cl_ref, *, nb, cb):
    nc = nb * cb
    sw = SSM_SW

    @pl.when(pl.program_id(1) == 0)
    def _():
        zero = jnp.zeros((LANES, LANES), BF16)
        for s in range(SSM_L):
            for t in range(SSM_L):
                t_ref[s * LANES:(s + 1) * LANES, t * LANES:(t + 1) * LANES] = kt_ref[t - s] if t >= s else zero
        bre = bd_ref[:, :sw]
        bim = bd_ref[:, sw:]
        for s in range(SSM_L):
            rre = rv_ref[s:s + 1, :sw]
            rim = rv_ref[s:s + 1, sw:]
            bl_ref[s * LANES:(s + 1) * LANES, :sw] = (bre * rre - bim * rim).astype(BF16)
            bl_ref[s * LANES:(s + 1) * LANES, sw:] = (bre * rim + bim * rre).astype(BF16)
        cre = cd_ref[:sw, :]
        cim = cd_ref[sw:, :]
        for t in range(SSM_L):
            fre = fw_ref[:sw, t:t + 1]
            fim = fw_ref[sw:, t:t + 1]
            cl_ref[:sw, t * LANES:(t + 1) * LANES] = (cre * fre - cim * fim).astype(BF16)
            cl_ref[sw:, t * LANES:(t + 1) * LANES] = (-(cre * fim + cim * fre)).astype(BF16)

    uf = jnp.concatenate([u_ref[pl.ds(s, nc, stride=SSM_L), :].astype(BF16) for s in range(SSM_L)], axis=1)
    y = _dot(uf, t_ref[...])
    v = _dot(uf, bl_ref[...])
    xre = v[:, :sw]
    xim = v[:, sw:]

    row = lax.broadcasted_iota(jnp.int32, (nc, sw), 0)
    cidx = row & (cb - 1)
    bidx = row // cb
    h0 = h0_ref[...]
    h0re = jnp.zeros((nc, sw), F32)
    h0im = jnp.zeros((nc, sw), F32)
    for b in range(nb):
        h0re = jnp.where(bidx == b, h0[b:b + 1, :sw], h0re)
        h0im = jnp.where(bidx == b, h0[b:b + 1, sw:], h0im)
    first = cidx == 0
    are = ap_ref[0:1, :sw]
    aim = ap_ref[0:1, sw:]
    xre = xre + jnp.where(first, are * h0re - aim * h0im, 0.0)
    xim = xim + jnp.where(first, are * h0im + aim * h0re, 0.0)

    d = 1
    k = 0
    while d < cb:
        are = ap_ref[k:k + 1, :sw]
        aim = ap_ref[k:k + 1, sw:]
        keep = cidx >= d
        sre = jnp.where(keep, pltpu.roll(xre, d, axis=0), 0.0)
        sim = jnp.where(keep, pltpu.roll(xim, d, axis=0), 0.0)
        xre, xim = xre + (are * sre - aim * sim), xim + (are * sim + aim * sre)
        d *= 2
        k += 1

    for b in range(nb):
        r = b * cb + cb - 1
        s_ref[b:b + 1, :sw] = xre[r:r + 1, :]
        s_ref[b:b + 1, sw:] = xim[r:r + 1, :]

    pre = jnp.where(first, h0re, pltpu.roll(xre, 1, axis=0))
    pim = jnp.where(first, h0im, pltpu.roll(xim, 1, axis=0))
    y = y + _dot(pre.astype(BF16), cl_ref[:sw, :]) + _dot(pim.astype(BF16), cl_ref[sw:, :])
    for t in range(SSM_L):
        y_ref[pl.ds(t, nc, stride=SSM_L), :] = y[:, t * LANES:(t + 1) * LANES]


def _ssm_call(u, mats, h0o, nb, cb):
    kt, bd, rv, cd, fw, ap = mats
    bx, tx, _ = u.shape

    def const_spec(a):
        return pl.BlockSpec((None,) + a.shape[1:], lambda o, b: (o,) + (0,) * (a.ndim - 1))

    tok_spec = pl.BlockSpec((None, tx, LANES), lambda o, b: (b, 0, o))
    st_spec = pl.BlockSpec((None, None, nb, 2 * SSM_SW), lambda o, b: (o, b, 0, 0))
    return pl.pallas_call(
        functools.partial(_ssm_kernel, nb=nb, cb=cb),
        grid=(SSM_OCTS, bx),
        in_specs=[tok_spec] + [const_spec(a) for a in (kt, bd, rv, cd, fw, ap)] + [st_spec],
        out_specs=[tok_spec, st_spec],
        out_shape=[jax.ShapeDtypeStruct((bx, tx, D_SSM), F32),
                   jax.ShapeDtypeStruct((SSM_OCTS, bx, nb, 2 * SSM_SW), F32)],
        scratch_shapes=[pltpu.VMEM((SSM_FLAT, SSM_FLAT), BF16), pltpu.VMEM((SSM_FLAT, 2 * SSM_SW), BF16),
                        pltpu.VMEM((2 * SSM_SW, SSM_FLAT), BF16)],
        compiler_params=_cparams("arbitrary", "arbitrary"),
        name="ssm",
    )(u, kt, bd, rv, cd, fw, ap, h0o)


def _oct_lanes(a):
    lead = a.shape[:-2]
    a = a.reshape(*lead, SSM_OCTS, SSM_SW)
    return jnp.moveaxis(a, -2, 0)


def _ssm_mats(a_re, a_im, log_dt, b_re, b_im, c_re, c_im):
    dt = jnp.exp(log_dt)[:, None]
    lre = a_re * dt
    lim = a_im * dt
    mag = jnp.exp(lre)
    ab_re = mag * jnp.cos(lim)
    ab_im = mag * jnp.sin(lim)
    den = a_re * a_re + a_im * a_im
    n_re = ab_re - 1.0
    f_re = (n_re * a_re + ab_im * a_im) / den
    f_im = (ab_im * a_re - n_re * a_im) / den
    bb_re = f_re[..., None] * b_re - f_im[..., None] * b_im
    bb_im = f_re[..., None] * b_im + f_im[..., None] * b_re

    def lam_pow(tau):
        tau = tau[:, None, None]
        m = jnp.exp(tau * lre)
        return m * jnp.cos(tau * lim), m * jnp.sin(tau * lim)

    pw_re, pw_im = lam_pow(jnp.arange(SSM_L + 1, dtype=F32))
    cp_re = c_re[None] * pw_re[:, :, None, :] - c_im[None] * pw_im[:, :, None, :]
    cp_im = c_re[None] * pw_im[:, :, None, :] + c_im[None] * pw_re[:, :, None, :]
    kern = jnp.sum(cp_re[..., None] * bb_re[None, :, None] - cp_im[..., None] * bb_im[None, :, None],
                   axis=3)
    eye = jnp.eye(SSM_OCT_G, dtype=F32)
    og = (SSM_OCTS, SSM_OCT_G)

    def slab_diag(m):
        lead = m.shape[:-3]
        a, b = m.shape[-2:]
        m = m.reshape(*lead, *og, a, b)
        m = m[..., :, None, :] * eye[:, None, :, None]
        return m.reshape(*lead, SSM_OCTS, SSM_OCT_G * a, SSM_OCT_G * b)

    kt = slab_diag(kern.transpose(0, 1, 3, 2)).astype(BF16).transpose(1, 0, 2, 3)
    bd = jnp.concatenate([slab_diag(bb_re.transpose(0, 2, 1)), slab_diag(bb_im.transpose(0, 2, 1))], axis=-1)
    rev = SSM_L - 1 - jnp.arange(SSM_L)
    rv = jnp.concatenate([_oct_lanes(pw_re[rev]), _oct_lanes(pw_im[rev])], axis=-1)
    cd = jnp.concatenate([slab_diag(c_re.transpose(0, 2, 1)), slab_diag(c_im.transpose(0, 2, 1))], axis=1)
    fw = jnp.concatenate([_oct_lanes(pw_re[1:]), _oct_lanes(pw_im[1:])], axis=-1).transpose(0, 2, 1)
    fw = jnp.pad(fw, ((0, 0), (0, 0), (0, LANES - SSM_L)))

    steps = float(SSM_L) * (2.0 ** jnp.arange(SSM_POW_ROWS, dtype=F32))
    ap_re, ap_im = lam_pow(steps)
    ap = jnp.concatenate([_oct_lanes(ap_re), _oct_lanes(ap_im)], axis=-1)
    return kt, bd, rv, cd, fw, ap


def _ssm_branch(u, mats, h0, nb, cb):
    bx = u.shape[0]
    h0o = jnp.concatenate([_oct_lanes(h0[..., 0]), _oct_lanes(h0[..., 1])], axis=-1)
    y, s = _ssm_call(u, mats, h0o.reshape(SSM_OCTS, bx, nb, 2 * SSM_SW), nb, cb)
    s = s.reshape(SSM_OCTS, bx * nb, 2, SSM_OCT_G, SSM_STATE).transpose(1, 0, 3, 4, 2)
    return y, s.reshape(bx * nb, SSM_GROUPS, SSM_STATE, 2)


def _attn_kernel(sink_ref, q_ref, ka_ref, kb_ref, va_ref, vb_ref, o_ref, *, banded):
    nk = 2 * LANES
    if banded:
        sub = 2 * CHUNK
        kfull = jnp.concatenate([ka_ref[...], kb_ref[...]], axis=0)
        vfull = jnp.concatenate([va_ref[...], vb_ref[...]], axis=0)
    else:
        sub = q_ref.shape[0]
        pad = jnp.zeros((nk - ka_ref.shape[0] - kb_ref.shape[0], KV_W), F32)
        kfull = jnp.concatenate([ka_ref[...], kb_ref[...], pad], axis=0)
        vfull = jnp.concatenate([va_ref[...], vb_ref[...], pad], axis=0)
    nsub = q_ref.shape[0] // sub

    r_i = lax.broadcasted_iota(jnp.int32, (sub, nk), 0)
    c_i = lax.broadcasted_iota(jnp.int32, (sub, nk), 1)
    if banded:
        lo = (r_i // CHUNK) * CHUNK
        valid = (c_i >= lo) & (c_i < lo + WINDOW + CHUNK)
        first_frame = jnp.where(pl.program_id(1) > 0, 0, WINDOW)
        valids = [valid & (c_i >= first_frame)] + [valid] * (nsub - 1)
    else:
        valids = [c_i < (ka_ref.shape[0] + kb_ref.shape[0])]

    lane = lax.broadcasted_iota(jnp.int32, kfull.shape, 1)
    low = lane < HEAD_DIM
    kroll = pltpu.roll(kfull, HEAD_DIM, axis=1)
    vroll = pltpu.roll(vfull, HEAD_DIM, axis=1)

    for g in range(N_KV_HEADS):
        k_lo, k_hi = (kfull, kroll) if g == 0 else (kroll, kfull)
        v_lo, v_hi = (vfull, vroll) if g == 0 else (vroll, vfull)
        kpad = (jnp.where(low, k_lo, 0.0).astype(BF16), jnp.where(low, 0.0, k_hi).astype(BF16))
        vpad = (jnp.where(low, v_lo, 0.0).astype(BF16), jnp.where(low, 0.0, v_hi).astype(BF16))
        for b in range(nsub):
            rs = slice(b * sub, (b + 1) * sub)
            ks = slice(b * sub, b * sub + nk)
            for pp in range(2):
                slab = 2 * g + pp
                qp = q_ref[rs, slab * LANES:(slab + 1) * LANES]
                acc = None
                for hh in range(2):
                    sk = sink_ref[2 * slab + hh]
                    s = jnp.where(valids[b], _dot_nt(qp, kpad[hh][ks]), NEG_INF)
                    m = jnp.maximum(jnp.max(s, axis=-1, keepdims=True), sk)
                    p = jnp.exp(s - m)
                    den = jnp.sum(p, axis=-1, keepdims=True) + jnp.exp(sk - m)
                    o = _dot(p.astype(BF16), vpad[hh][ks]) / den
                    acc = o if acc is None else acc + o
                o_ref[rs, slab * LANES:(slab + 1) * LANES] = acc.astype(BF16)


def _attn_call(sink, q, ka, kb, va, vb, banded):
    bx, tx, _ = q.shape
    if banded:
        rows = ATTN_ROWS
        nt = tx // rows
        grid = (bx, nt)
        q_spec = pl.BlockSpec((None, rows, ATTN_W), lambda b, i: (b, i, 0))
        prev = pl.BlockSpec((None, WINDOW, KV_W), lambda b, i: (b, jnp.maximum(i * (rows // WINDOW) - 1, 0), 0))
        cur = pl.BlockSpec((None, rows, KV_W), lambda b, i: (b, i, 0))
    else:
        rows = tx
        grid = (bx, 1)
        q_spec = pl.BlockSpec((None, rows, ATTN_W), lambda b, i: (b, 0, 0))
        prev = pl.BlockSpec((None, ka.shape[1], KV_W), lambda b, i: (b, 0, 0))
        cur = pl.BlockSpec((None, rows, KV_W), lambda b, i: (b, 0, 0))
    return pl.pallas_call(
        functools.partial(_attn_kernel, banded=banded),
        grid=grid,
        in_specs=[pl.BlockSpec(memory_space=pltpu.SMEM), q_spec, prev, cur, prev, cur],
        out_specs=q_spec,
        out_shape=jax.ShapeDtypeStruct((bx, tx, ATTN_W), BF16),
        compiler_params=_cparams("parallel", "parallel"),
        name="attn",
    )(sink, q, ka, kb, va, vb)


def _route(logits_t, bias_col, topi_ref, topw_ref):
    tt = logits_t.shape[1]
    scores = jax.nn.sigmoid(logits_t)
    biased = scores + bias_col
    sub = lax.broadcasted_iota(jnp.int32, (EXPERTS_PER_GROUP, tt), 0).astype(F32)
    ninf = float('-inf')
    blocks = [biased[EXPERTS_PER_GROUP * g:EXPERTS_PER_GROUP * (g + 1), :] for g in range(N_EXPERT_GROUPS)]

    gscore = jnp.zeros((N_EXPERT_GROUPS, tt), F32)
    for g in range(N_EXPERT_GROUPS):
        blk = blocks[g]
        m1 = jnp.max(blk, axis=0, keepdims=True)
        i1 = jnp.min(jnp.where(blk == m1, sub, float(EXPERTS_PER_GROUP)), axis=0, keepdims=True)
        m2 = jnp.max(jnp.where(sub == i1, ninf, blk), axis=0, keepdims=True)
        gscore = jnp.where(sub == float(g), jnp.broadcast_to(m1 + m2, gscore.shape), gscore)

    grank = jnp.zeros((N_EXPERT_GROUPS, tt), F32)
    for j in range(N_EXPERT_GROUPS):
        rj = jnp.broadcast_to(gscore[j:j + 1, :], gscore.shape)
        beats = (rj > gscore) | ((rj == gscore) & (sub > float(j)))
        grank = grank + jnp.where(beats, 1.0, 0.0)
    gsel = jnp.where(grank < float(TOPK_GROUPS), 1.0, 0.0)

    masked = []
    for g in range(N_EXPERT_GROUPS):
        on = jnp.broadcast_to(gsel[g:g + 1, :], blocks[g].shape) > 0.5
        masked.append(jnp.where(on, blocks[g], NEG_INF))

    ranks = [jnp.zeros((EXPERTS_PER_GROUP, tt), F32) for _ in range(N_EXPERT_GROUPS)]
    for jb in range(N_EXPERT_GROUPS):
        for jj in range(EXPERTS_PER_GROUP):
            rj = jnp.broadcast_to(masked[jb][jj:jj + 1, :], (EXPERTS_PER_GROUP, tt))
            for ib in range(N_EXPERT_GROUPS):
                mi = masked[ib]
                if ib < jb:
                    beats = rj > mi
                elif ib > jb:
                    beats = rj >= mi
                else:
                    beats = (rj > mi) | ((rj == mi) & (sub > float(jj)))
                ranks[ib] = ranks[ib] + jnp.where(beats, 1.0, 0.0)

    topi = jnp.zeros((TOP_K, tt), F32)
    topw = jnp.zeros((TOP_K, tt), F32)
    for r in range(TOP_K):
        ai = jnp.zeros((EXPERTS_PER_GROUP, tt), F32)
        aw = jnp.zeros((EXPERTS_PER_GROUP, tt), F32)
        for g in range(N_EXPERT_GROUPS):
            hit = ranks[g] == float(r)
            ai = ai + jnp.where(hit, sub + float(EXPERTS_PER_GROUP * g), 0.0)
            aw = aw + jnp.where(hit, scores[EXPERTS_PER_GROUP * g:EXPERTS_PER_GROUP * (g + 1), :], 0.0)
        ir = jnp.sum(ai, axis=0, keepdims=True)
        wr = jnp.sum(aw, axis=0, keepdims=True)
        topi = jnp.where(sub == float(r), jnp.broadcast_to(ir, topi.shape), topi)
        topw = jnp.where(sub == float(r), jnp.broadcast_to(wr, topw.shape), topw)
    den = jnp.sum(topw, axis=0, keepdims=True)
    topi_ref[...] = topi.astype(jnp.int32)
    topw_ref[...] = topw / den * ROUTED_SCALE


def _post_kernel(y_ref, u_ref, a_ref, gs_ref, ga_ref, x_ref, g1_ref, sh2_ref, sc2_ref, g2_ref,
                 d_ref, wglu_ref, bglu_ref, wbs_ref, wba_ref, wout_ref, nf_ref, rwt_ref, rb_ref,
                 swg_ref, swu_ref, swd_ref, xp_ref, h2s_ref, topi_ref, topw_ref):
    ys = y_ref[...] + d_ref[...] * u_ref[...]
    z = jax.nn.gelu(ys)
    y2 = z * jax.nn.sigmoid(_dot(z.astype(BF16), wglu_ref[...]) + bglu_ref[...])
    merged = (jax.nn.sigmoid(gs_ref[...]) * _dot(y2.astype(BF16), wbs_ref[...])
              + jax.nn.sigmoid(ga_ref[...]) * _dot(a_ref[...], wba_ref[...]))
    x1 = x_ref[...] + g1_ref[...] * _dot(merged.astype(BF16), wout_ref[...])

    h2 = _rmsnorm(x1, nf_ref[...]) * (1.0 + sc2_ref[...]) + sh2_ref[...]
    tt = h2.shape[0]
    for j in range(SLAB_ROWS):
        h2s_ref[pl.ds(j, tt, stride=SLAB_ROWS), :] = h2[:, j * LANES:(j + 1) * LANES]
    hh, hl = _split(h2)
    rh, rl = _split(rwt_ref[...])
    logits_t = _dot_nt(rh, hh) + (_dot_nt(rh, hl) + _dot_nt(rl, hh))
    _route(logits_t, rb_ref[...], topi_ref, topw_ref)

    shared = _dot((jax.nn.silu(_dot(hh, swg_ref[...])) * _dot(hh, swu_ref[...])).astype(BF16), swd_ref[...])
    xp_ref[...] = x1 + g2_ref[...] * shared


def _post_call(y, u, attn, gs, ga, x, mods, lw, tt):
    bx, tx, _ = x.shape
    nt = tx // tt
    per_tok = mods[0].shape[1] != 1
    mod_spec = (pl.BlockSpec((None, tt, D_MODEL), lambda b, i: (b, i, 0)) if per_tok
                else pl.BlockSpec((None, 1, D_MODEL), lambda b, i: (b, 0, 0)))

    def tok_spec(w):
        return pl.BlockSpec((None, tt, w), lambda b, i: (b, i, 0))

    def full(a):
        return pl.BlockSpec(a.shape, lambda b, i: (0,) * a.ndim)

    weights = [lw['ssm_d'], lw['w_glu'], lw['b_glu'], lw['w_bs'], lw['w_ba'], lw['w_out'], lw['norm_ffn_g'],
               lw['router_wt'], lw['router_bias'], lw['sh_wg'], lw['sh_wu'], lw['sh_wd']]
    return pl.pallas_call(
        _post_kernel,
        grid=(bx, nt),
        in_specs=[tok_spec(D_SSM), tok_spec(D_SSM), tok_spec(ATTN_W), tok_spec(D_MODEL), tok_spec(D_MODEL),
                  tok_spec(D_MODEL), mod_spec, mod_spec, mod_spec, mod_spec] + [full(w) for w in weights],
        out_specs=[tok_spec(D_MODEL),
                   pl.BlockSpec((tt * SLAB_ROWS, LANES), lambda b, i: (b * nt + i, 0)),
                   pl.BlockSpec((TOP_K, tt), lambda b, i: (0, b * nt + i)),
                   pl.BlockSpec((TOP_K, tt), lambda b, i: (0, b * nt + i))],
        out_shape=[jax.ShapeDtypeStruct((bx, tx, D_MODEL), F32),
                   jax.ShapeDtypeStruct((bx * tx * SLAB_ROWS, LANES), F32),
                   jax.ShapeDtypeStruct((TOP_K, bx * tx), jnp.int32),
                   jax.ShapeDtypeStruct((TOP_K, bx * tx), F32)],
        compiler_params=_cparams("parallel", "parallel"),
        name="post",
    )(y, u, attn, gs, ga, x, *mods, *weights)


def _moe_kernel(ptab_ref, ktab_ref, elist_ref, meta_ref, hs_ref, word_hbm, gate_hbm, wg_hbm, wu_hbm, wd_hbm, o_ref,
                word_s, gate_s, xt0, xt1, yt0, yt1, wg_buf, wu_buf, wd_buf, lsem, wsem, *, tm, bs, ptab_w, layer):
    i = pl.program_id(0)
    stride = bs + SUBLANES
    nslot = word_s.shape[0]
    ngate = gate_s.shape[0]
    acc_ref = o_ref
    slab_mask = -SLAB_ROWS
    gather_mask = (tm * SLAB_ROWS - 1) & slab_mask
    tab0 = i * ptab_w + 1
    npairs = meta_ref[2 * i]
    nrank = meta_ref[2 * i + 1]

    def list_copies():
        return (pltpu.make_async_copy(word_hbm.at[pl.ds(pl.multiple_of(i * nslot, LANES), nslot)],
                                      word_s, lsem.at[0]),
                pltpu.make_async_copy(gate_hbm.at[pl.ds(pl.multiple_of(i * ngate, LANES), ngate)],
                                      gate_s, lsem.at[1]))

    def weight_copies(k):
        ex = elist_ref[i * N_EXPERTS + k]
        slot = k % MOE_WSLOTS
        return (pltpu.make_async_copy(wg_hbm.at[layer, ex], wg_buf.at[slot], wsem.at[slot, 0]),
                pltpu.make_async_copy(wu_hbm.at[layer, ex], wu_buf.at[slot], wsem.at[slot, 1]),
                pltpu.make_async_copy(wd_hbm.at[layer, ex], wd_buf.at[slot], wsem.at[slot, 2]))

    def gather(base, xt_ref):
        for r in range(bs):
            row = pl.multiple_of(word_s[base + r] & gather_mask, SLAB_ROWS)
            xt_ref[pl.ds(r, SLAB_ROWS, stride=stride), :] = hs_ref[pl.ds(row, SLAB_ROWS), :]

    def expert(xt_ref, yt_ref, k):
        slot = k % MOE_WSLOTS
        x = jnp.concatenate([xt_ref[j * stride:j * stride + bs, :] for j in range(SLAB_ROWS)],
                            axis=1).astype(BF16)
        act = jax.nn.silu(_dot(x, wg_buf[slot])) * _dot(x, wu_buf[slot])
        y = _dot(act.astype(BF16), wd_buf[slot])
        for j in range(SLAB_ROWS):
            yt_ref[j * stride:j * stride + bs, :] = y[:, j * LANES:(j + 1) * LANES]

    def scatter(base, yt_ref):
        for r0 in range(0, bs, MOE_RMW_UNROLL):
            upd = []
            for r in range(r0, r0 + MOE_RMW_UNROLL):
                word = word_s[base + r]
                row = pl.multiple_of(word & slab_mask, SLAB_ROWS)
                upd.append((row, acc_ref[pl.ds(row, SLAB_ROWS), :]
                            + gate_s[word] * yt_ref[pl.ds(r, SLAB_ROWS, stride=stride), :]))
            for row, val in upd:
                acc_ref[pl.ds(row, SLAB_ROWS), :] = val

    for c in list_copies():
        c.start()
    ahead = MOE_WSLOTS - 2
    for r in range(ahead + 1):
        @pl.when(r < nrank)
        def _():
            for c in weight_copies(r):
                c.start()

    acc_ref[...] = jnp.zeros_like(acc_ref)
    yt1[...] = jnp.zeros_like(yt1)
    for c in list_copies():
        c.wait()
    gather(ptab_ref[tab0], xt0)

    def pair(j, carry):
        waited, started = carry
        t = tab0 + 2 * j
        k0 = ktab_ref[t]
        k1 = ktab_ref[t + 1]

        target = jnp.minimum(k1 + ahead, nrank - 1)
        for d in (1, 2):
            @pl.when(started + d <= target)
            def _():
                for c in weight_copies(started + d):
                    c.start()

        for d in (1, 2):
            @pl.when(waited + d <= k1)
            def _():
                for c in weight_copies(waited + d):
                    c.wait()

        gather(ptab_ref[t + 1], xt1)
        expert(xt0, yt0, k0)
        scatter(ptab_ref[t - 1], yt1)
        gather(ptab_ref[t + 2], xt0)
        expert(xt1, yt1, k1)
        scatter(ptab_ref[t], yt0)
        return jnp.maximum(waited, k1), jnp.maximum(started, target)

    lax.fori_loop(0, npairs, pair, (jnp.int32(-1), jnp.minimum(jnp.int32(ahead), nrank - 1)))
    scatter(ptab_ref[tab0 + 2 * npairs - 1], yt1)


def _moe_call(h2s, topi, topw, wg, wu, wd, layer, tm):
    n = h2s.shape[0] // SLAB_ROWS
    nt = n // tm
    nseg = nt * N_EXPERTS
    nreal = tm * TOP_K
    bs = MOE_BLOCK_LARGE if nreal // N_EXPERTS >= MOE_BLOCK_LARGE else MOE_BLOCK_SMALL
    npad = MOE_PAD
    assert npad >= bs - 1 and npad % LANES == 0
    nslot = nreal + (N_EXPERTS + 1) * npad
    null_base = nslot - npad
    ptab_w = nreal // bs + N_EXPERTS + 4

    i32 = jnp.int32
    experts = jnp.arange(N_EXPERTS, dtype=i32)
    counts = jnp.sum((topi.reshape(TOP_K, nt, tm)[..., None] == experts).astype(i32), axis=(0, 2))
    nblk = (counts + bs - 1) // bs
    first = jnp.cumsum(counts + npad, axis=1) - (counts + npad)
    pend = jnp.cumsum(nblk, axis=1)
    pstart = pend - nblk
    nonempty = (nblk > 0).astype(i32)
    krank = jnp.cumsum(nonempty, axis=1) - 1
    nrank = jnp.sum(nonempty, axis=1)
    elist = jnp.sum(jnp.where((nonempty[:, None, :] == 1) & (krank[:, None, :] == experts[None, :, None]),
                              experts[None, None, :], 0), axis=-1).astype(i32)
    pos = jnp.arange(ptab_w, dtype=i32) - 1
    e_of = jnp.sum((pend[:, None, :] <= pos[None, :, None]).astype(i32), axis=-1)
    e_cl = jnp.minimum(e_of, N_EXPERTS - 1)
    onehot = (e_cl[..., None] == experts).astype(i32)

    def at_expert(a):
        return jnp.sum(onehot * a[:, None, :], axis=-1)

    q = pos[None, :] - at_expert(pstart)
    real_blk = (pos[None, :] >= 0) & (e_of < N_EXPERTS)
    ptab = jnp.where(real_blk, at_expert(first) + q * bs, null_base).astype(i32)
    ktab = jnp.where(real_blk, at_expert(krank), nrank[:, None] - 1).astype(i32)
    meta = jnp.stack([(pend[:, -1] + 1) // 2, nrank], axis=1).astype(i32)

    assert TOP_K == SLAB_ROWS and nreal < (1 << MOE_WORD_BITS) and 2 * nseg + 2 < (1 << (31 - MOE_WORD_BITS))
    flat = jnp.arange(n * TOP_K, dtype=i32)
    slot = (flat % n) * TOP_K + flat // n
    key = 2 * ((slot // nreal) * N_EXPERTS + topi.reshape(n * TOP_K))
    pad_key = jnp.concatenate([jnp.repeat(2 * jnp.arange(nseg, dtype=i32) + 1, npad),
                               jnp.repeat(2 * (jnp.arange(nt, dtype=i32) * N_EXPERTS + N_EXPERTS - 1) + 1, npad)])
    packed = jnp.concatenate([(key << MOE_WORD_BITS) | (slot % nreal), (pad_key << MOE_WORD_BITS) | nreal])
    word_s = lax.sort(packed, is_stable=False) & ((1 << MOE_WORD_BITS) - 1)
    ngate = nreal + LANES
    gate_s = jnp.concatenate([topw.T.reshape(nt, nreal), jnp.zeros((nt, LANES), F32)], axis=1).reshape(nt * ngate)

    stage = pltpu.VMEM((SLAB_ROWS * (bs + SUBLANES), LANES), F32)
    hbm = pl.BlockSpec(memory_space=pl.ANY)
    grid_spec = pltpu.PrefetchScalarGridSpec(
        num_scalar_prefetch=4,
        grid=(nt,),
        in_specs=[
            pl.BlockSpec((tm * SLAB_ROWS, LANES), lambda i, *_: (i, 0), pipeline_mode=pl.Buffered(1)),
            hbm, hbm, hbm, hbm, hbm,
        ],
        out_specs=pl.BlockSpec((None, (tm + 1) * SLAB_ROWS, LANES), lambda i, *_: (i, 0, 0),
                               pipeline_mode=pl.Buffered(1)),
        scratch_shapes=[
            pltpu.SMEM((nslot,), i32),
            pltpu.SMEM((ngate,), F32),
            stage, stage, stage, stage,
            pltpu.VMEM((MOE_WSLOTS, D_MODEL, D_EXPERT), BF16),
            pltpu.VMEM((MOE_WSLOTS, D_MODEL, D_EXPERT), BF16),
            pltpu.VMEM((MOE_WSLOTS, D_EXPERT, D_MODEL), BF16),
            pltpu.SemaphoreType.DMA((2,)),
            pltpu.SemaphoreType.DMA((MOE_WSLOTS, 3)),
        ],
    )
    return pl.pallas_call(
        functools.partial(_moe_kernel, tm=tm, bs=bs, ptab_w=ptab_w, layer=layer),
        grid_spec=grid_spec,
        out_shape=jax.ShapeDtypeStruct((nt, (tm + 1) * SLAB_ROWS, LANES), F32),
        compiler_params=_cparams("arbitrary"),
        name="moe",
    )(ptab.reshape(nt * ptab_w), ktab.reshape(nt * ptab_w), elist.reshape(nseg), meta.reshape(nt * 2),
      h2s, word_s, gate_s, wg, wu, wd)


def _combine_kernel(r_ref, xp_ref, g2_ref, fg_ref, o_ref, *, final):
    tt = xp_ref.shape[0]
    routed = jnp.concatenate([r_ref[pl.ds(j, tt, stride=SLAB_ROWS), :] for j in range(SLAB_ROWS)], axis=1)
    x2 = xp_ref[...] + g2_ref[...] * routed
    if final:
        x2 = _rmsnorm(x2, fg_ref[...])
    o_ref[...] = x2


def _combine_call(routed_s, xp, g2, final_g, tt, final):
    bx, tx, _ = xp.shape
    nt = tx // tt
    per_tok = g2.shape[1] != 1
    g2_spec = (pl.BlockSpec((None, tt, D_MODEL), lambda b, i: (b, i, 0)) if per_tok
               else pl.BlockSpec((None, 1, D_MODEL), lambda b, i: (b, 0, 0)))
    tok_spec = pl.BlockSpec((None, tt, D_MODEL), lambda b, i: (b, i, 0))
    per_tile = (routed_s.shape[1] // SLAB_ROWS - 1) // tt
    routed_spec = pl.BlockSpec((None, tt * SLAB_ROWS, LANES),
                               lambda b, i: ((b * nt + i) // per_tile, (b * nt + i) % per_tile, 0))
    return pl.pallas_call(
        functools.partial(_combine_kernel, final=final),
        grid=(bx, nt),
        in_specs=[routed_spec, tok_spec, g2_spec,
                  pl.BlockSpec((1, D_MODEL), lambda b, i: (0, 0))],
        out_specs=tok_spec,
        out_shape=jax.ShapeDtypeStruct((bx, tx, D_MODEL), F32),
        compiler_params=_cparams("parallel", "parallel"),
        name="combine",
    )(routed_s, xp, g2, final_g)


def _rope_tables(pos):
    half = HEAD_DIM // 2
    inv_freq = ROPE_THETA ** (-jnp.arange(half, dtype=F32) / half)
    ang = pos.astype(F32)[:, None] * inv_freq[None, :]
    cos = jnp.cos(ang)
    sin = jnp.sin(ang)
    return jnp.tile(cos, (1, 4)), jnp.tile(jnp.concatenate([-sin, sin], axis=1), (1, 2))


def _trunk(x, mods, pos, past_k, past_v, ssm_h0, layers, experts, final_g, tt, tm, tok_batches):
    bx, tx, _ = x.shape
    b, t = tok_batches
    cos_t, sin_t = _rope_tables(pos)
    new_k, new_v, new_h = [], [], []
    for l, lw in enumerate(layers):
        sh1, sc1, g1, sh2, sc2, g2 = mods[l]
        u, q, k, v, gs, ga = _inproj_call(x, sh1, sc1, lw['norm_attn_g'], lw['w_in'], cos_t, sin_t, tt)
        ks = k.reshape(b, t, KV_W)
        vs = v.reshape(b, t, KV_W)
        qs = q.reshape(b, t, ATTN_W)
        if past_k is None:
            attn = _attn_call(lw['sink'], qs, ks, ks, vs, vs, True)
            new_k.append(ks[:, -WINDOW:].reshape(b, WINDOW, N_KV_HEADS, HEAD_DIM))
            new_v.append(vs[:, -WINDOW:].reshape(b, WINDOW, N_KV_HEADS, HEAD_DIM))
            h0 = jnp.zeros((b, SSM_GROUPS, SSM_STATE, 2), F32)
        else:
            pk = past_k[l].reshape(b, -1, KV_W)
            pv = past_v[l].reshape(b, -1, KV_W)
            attn = _attn_call(lw['sink'], qs, pk, ks, pv, vs, False)
            new_k.append(ks.reshape(b, t, N_KV_HEADS, HEAD_DIM))
            new_v.append(vs.reshape(b, t, N_KV_HEADS, HEAD_DIM))
            h0 = ssm_h0[l]
        y, h_last = _ssm_branch(u, lw['ssm_mats'], h0, b // bx, t // SSM_L)
        new_h.append(h_last)
        xp, h2s, topi, topw = _post_call(y, u, attn.reshape(bx, tx, ATTN_W), gs, ga, x,
                                         (g1, sh2, sc2, g2), lw, tt)
        routed_s = _moe_call(h2s, topi, topw, *experts, l, tm)
        x = _combine_call(routed_s, xp, g2, final_g, tt, l == len(layers) - 1)
    return x, jnp.stack(new_k), jnp.stack(new_v), jnp.stack(new_h)


def kernel(x_prompt, x_sample, cache_k, cache_v, state_ssm, c_prompt, c_sample, ada_w, ada_b, norm_attn_g,
           norm_ffn_g, w_in, ssm_a_re, ssm_a_im, ssm_log_dt, ssm_b_re, ssm_b_im, ssm_c_re, ssm_c_im, ssm_d,
           ssm_w_glu, ssm_b_glu, attn_sink, w_branch_ssm, w_branch_attn, w_out, router_w, router_bias,
           expert_w_gate, expert_w_up, expert_w_down, shared_w_gate, shared_w_up, shared_w_down, final_g):
    depth = ada_w.shape[0]
    bp, tp, _ = x_prompt.shape
    bs, ts, _ = x_sample.shape

    layers = []
    for l in range(depth):
        layers.append({
            'norm_attn_g': norm_attn_g[l][None], 'norm_ffn_g': norm_ffn_g[l][None],
            'w_in': w_in[l].astype(BF16), 'sink': attn_sink[l],
            'ssm_mats': _ssm_mats(ssm_a_re[l], ssm_a_im[l], ssm_log_dt[l], ssm_b_re[l], ssm_b_im[l],
                                  ssm_c_re[l], ssm_c_im[l]),
            'ssm_d': ssm_d[l][None], 'w_glu': ssm_w_glu[l].astype(BF16), 'b_glu': ssm_b_glu[l][None],
            'w_bs': w_branch_ssm[l].astype(BF16), 'w_ba': w_branch_attn[l].astype(BF16),
            'w_out': w_out[l].astype(BF16), 'router_wt': router_w[l].T, 'router_bias': router_bias[l][:, None],
            'sh_wg': shared_w_gate[l].astype(BF16), 'sh_wu': shared_w_up[l].astype(BF16),
            'sh_wd': shared_w_down[l].astype(BF16),
        })
    experts = (expert_w_gate.astype(BF16), expert_w_up.astype(BF16), expert_w_down.astype(BF16))
    fg = final_g[None]

    rows = 16
    c_all = jnp.concatenate([c_prompt, c_sample, jnp.zeros((rows - bp - bs, D_MODEL), F32)], axis=0)
    mod = _mod_call(c_all, ada_w, ada_b).reshape(depth, rows, 6, D_MODEL)
    mods_p = [[mod[l, :bp, j][:, None, :] for j in range(6)] for l in range(depth)]
    mods_s = [[jnp.repeat(mod[l, bp:bp + bs, j], ts, axis=0)[None] for j in range(6)] for l in range(depth)]

    pos_p = jnp.arange(tp, dtype=jnp.int32)
    pos_s = jnp.tile(PAST_LEN + jnp.arange(ts, dtype=jnp.int32), bs)

    y_p, k_p, v_p, h_p = _trunk(x_prompt, mods_p, pos_p, None, None, None, layers, experts, fg,
                                tt=512, tm=min(4096, bp * tp), tok_batches=(bp, tp))
    y_s, k_s, v_s, h_s = _trunk(x_sample.reshape(1, bs * ts, D_MODEL), mods_s, pos_s, cache_k, cache_v,
                                state_ssm, layers, experts, fg, tt=bs * ts, tm=bs * ts, tok_batches=(bs, ts))
    return (y_p, y_s.reshape(bs, ts, D_MODEL), k_p, v_p, h_p, k_s, v_s, h_s)
```

```python
import functools

import jax
import jax.numpy as jnp
from jax import lax
from jax.experimental import pallas as pl
from jax.experimental.pallas import tpu as pltpu

F32 = jnp.float32
BF16 = jnp.bfloat16

D_MODEL = 1024
DEPTH = 2
PAST_LEN = 4096
CHUNK = 64
N_HEADS = 8
N_KV_HEADS = 2
HEAD_DIM = 64
WINDOW = 128
ROPE_THETA = 10000.0
D_SSM = 512
SSM_GROUP_CH = 16
SSM_GROUPS = D_SSM // SSM_GROUP_CH
SSM_STATE = 64
N_EXPERTS = 64
N_EXPERT_GROUPS = 8
EXPERTS_PER_GROUP = N_EXPERTS // N_EXPERT_GROUPS
TOPK_GROUPS = 4
TOP_K = 8
D_EXPERT = 256
D_SHARED = 256
ROUTED_SCALE = 2.5
RMS_EPS = 1e-6
NEG_INF = -1e30
ATTN_W = N_HEADS * HEAD_DIM
KV_W = N_KV_HEADS * HEAD_DIM
IN_COLS = D_SSM + ATTN_W + 2 * KV_W + 2 * D_MODEL
IN_SPLITS = (0, D_SSM, D_SSM + ATTN_W, D_SSM + ATTN_W + KV_W, D_SSM + ATTN_W + 2 * KV_W,
             D_SSM + ATTN_W + 2 * KV_W + D_MODEL, IN_COLS)

LANES = 128
SSM_L = 16
SSM_OCT_G = LANES // SSM_GROUP_CH
SSM_OCTS = SSM_GROUPS // SSM_OCT_G
SSM_FLAT = SSM_L * LANES
SSM_SW = SSM_OCT_G * SSM_STATE
SSM_POW_ROWS = 16
VMEM_LIMIT = 56 * 1024 * 1024
SUBLANES = 8
ATTN_ROWS = 512
SLAB_ROWS = D_MODEL // LANES
MOE_BLOCK_LARGE = 192
MOE_BLOCK_SMALL = 128
MOE_PAD = 256
MOE_RMW_UNROLL = 4
MOE_WSLOTS = 4
MOE_WORD_BITS = 16


def _cparams(*sem):
    return pltpu.CompilerParams(dimension_semantics=sem, vmem_limit_bytes=VMEM_LIMIT)


def _dot(a, b):
    return jnp.dot(a, b, preferred_element_type=F32)


def _dot_nt(a, b):
    return lax.dot_general(a, b, (((1,), (1,)), ((), ())), preferred_element_type=F32)


def _split(x):
    hi = x.astype(BF16)
    lo = (x - hi.astype(F32)).astype(BF16)
    return hi, lo


def _rmsnorm(x, g):
    return x * lax.rsqrt(jnp.mean(x * x, axis=-1, keepdims=True) + RMS_EPS) * g


def _mod_kernel(c_ref, w_ref, b_ref, o_ref):
    cond = jax.nn.silu(c_ref[...])
    ch, cl = _split(cond)
    wh, wl = _split(w_ref[...])
    o_ref[...] = _dot(ch, wh) + (_dot(ch, wl) + _dot(cl, wh)) + b_ref[...]


def _mod_call(c_all, ada_w, ada_b):
    depth = ada_w.shape[0]
    rows = c_all.shape[0]
    nj = 6
    return pl.pallas_call(
        _mod_kernel,
        grid=(depth, nj),
        in_specs=[
            pl.BlockSpec((rows, D_MODEL), lambda l, j: (0, 0)),
            pl.BlockSpec((None, D_MODEL, D_MODEL), lambda l, j: (l, 0, j)),
            pl.BlockSpec((None, 1, D_MODEL), lambda l, j: (l, 0, j)),
        ],
        out_specs=pl.BlockSpec((None, rows, D_MODEL), lambda l, j: (l, 0, j)),
        out_shape=jax.ShapeDtypeStruct((depth, rows, nj * D_MODEL), F32),
        compiler_params=_cparams("parallel", "parallel"),
        name="mod",
    )(c_all, ada_w, ada_b.reshape(depth, 1, nj * D_MODEL))


def _rope2(t, cos, sin_signed, first_half):
    swapped = jnp.where(first_half, pltpu.roll(t, LANES - HEAD_DIM // 2, axis=1),
                        pltpu.roll(t, HEAD_DIM // 2, axis=1))
    return t * cos + swapped * sin_signed


def _inproj_kernel(*refs, resid_in):
    if resid_in:
        (r_ref, xp_ref, g2_ref, sh_ref, sc_ref, g_ref, w_ref, cos_ref, sin_ref,
         u_ref, q_ref, k_ref, v_ref, gs_ref, ga_ref, x_out_ref) = refs
        tt = xp_ref.shape[0]
        routed = jnp.concatenate([r_ref[pl.ds(j, tt, stride=SLAB_ROWS), :] for j in range(SLAB_ROWS)], axis=1)
        x = xp_ref[...] + g2_ref[...] * routed
        x_out_ref[...] = x
    else:
        (x_ref, sh_ref, sc_ref, g_ref, w_ref, cos_ref, sin_ref,
         u_ref, q_ref, k_ref, v_ref, gs_ref, ga_ref) = refs
        x = x_ref[...]
    h = _rmsnorm(x, g_ref[...]) * (1.0 + sc_ref[...]) + sh_ref[...]
    hb = h.astype(BF16)

    def proj(i):
        return _dot(hb, w_ref[:, IN_SPLITS[i]:IN_SPLITS[i + 1]])

    u_ref[...] = proj(0)
    cos = cos_ref[...]
    sin = sin_ref[...]
    lane = lax.broadcasted_iota(jnp.int32, cos.shape, 1)
    first_half = (lane % HEAD_DIM) < (HEAD_DIM // 2)
    q = proj(1)
    for j in range(ATTN_W // LANES):
        sl = slice(j * LANES, (j + 1) * LANES)
        q_ref[:, sl] = (_rope2(q[:, sl], cos, sin, first_half) * (HEAD_DIM ** -0.5)).astype(BF16)
    k_ref[...] = _rope2(proj(2), cos, sin, first_half)
    v_ref[...] = proj(3)
    gs_ref[...] = proj(4)
    ga_ref[...] = proj(5)


def _routed_spec(routed_s, tt, nt):
    per_tile = (routed_s.shape[1] // SLAB_ROWS - 1) // tt
    return pl.BlockSpec((None, tt * SLAB_ROWS, LANES),
                        lambda b, i: ((b * nt + i) // per_tile, (b * nt + i) % per_tile, 0))


def _inproj_call(x, sh1, sc1, norm_g, w_in_bf, cos_t, sin_t, tt):
    resid_in = isinstance(x, tuple)
    bx, tx, _ = (x[1] if resid_in else x).shape
    nt = tx // tt

    def mod_spec(m):
        return (pl.BlockSpec((None, tt, D_MODEL), lambda b, i: (b, i, 0)) if m.shape[1] != 1
                else pl.BlockSpec((None, 1, D_MODEL), lambda b, i: (b, 0, 0)))

    def tok_spec(w):
        return pl.BlockSpec((None, tt, w), lambda b, i: (b, i, 0))

    def tok_shape(w, dt):
        return jax.ShapeDtypeStruct((bx, tx, w), dt)

    if resid_in:
        stream = list(x)
        stream_specs = [_routed_spec(x[0], tt, nt), tok_spec(D_MODEL), mod_spec(x[2])]
    else:
        stream = [x]
        stream_specs = [tok_spec(D_MODEL)]
    out_specs = [tok_spec(D_SSM), tok_spec(ATTN_W), tok_spec(KV_W), tok_spec(KV_W),
                 tok_spec(D_MODEL), tok_spec(D_MODEL)]
    out_shape = [tok_shape(D_SSM, F32), tok_shape(ATTN_W, BF16), tok_shape(KV_W, F32),
                 tok_shape(KV_W, F32), tok_shape(D_MODEL, F32), tok_shape(D_MODEL, F32)]
    if resid_in:
        out_specs.append(tok_spec(D_MODEL))
        out_shape.append(tok_shape(D_MODEL, F32))
    return pl.pallas_call(
        functools.partial(_inproj_kernel, resid_in=resid_in),
        grid=(bx, nt),
        in_specs=stream_specs + [
            mod_spec(sh1), mod_spec(sc1),
            pl.BlockSpec((1, D_MODEL), lambda b, i: (0, 0)),
            pl.BlockSpec((D_MODEL, IN_COLS), lambda b, i: (0, 0)),
            pl.BlockSpec((tt, LANES), lambda b, i: (i, 0)),
            pl.BlockSpec((tt, LANES), lambda b, i: (i, 0)),
        ],
        out_specs=out_specs,
        out_shape=out_shape,
        compiler_params=_cparams("parallel", "parallel"),
        name="inproj",
    )(*stream, sh1, sc1, norm_g, w_in_bf, cos_t, sin_t)


def _ssm_kernel(u_ref, kt_ref, bd_ref, rv_ref, cd_ref, fw_ref, ap_ref, h0_ref, y_ref, s_ref,
                t_ref, bl_ref, cl_ref, *, nb, cb):
    nc = nb * cb
    sw = SSM_SW

    @pl.when(pl.program_id(1) == 0)
    def _():
        zero = jnp.zeros((LANES, LANES), BF16)
        for s in range(SSM_L):
            for t in range(SSM_L):
                t_ref[s * LANES:(s + 1) * LANES, t * LANES:(t + 1) * LANES] = kt_ref[t - s] if t >= s else zero
        bre = bd_ref[:, :sw]
        bim = bd_ref[:, sw:]
        for s in range(SSM_L):
            rre = rv_ref[s:s + 1, :sw]
            rim = rv_ref[s:s + 1, sw:]
            bl_ref[s * LANES:(s + 1) * LANES, :sw] = (bre * rre - bim * rim).astype(BF16)
            bl_ref[s * LANES:(s + 1) * LANES, sw:] = (bre * rim + bim * rre).astype(BF16)
        cre = cd_ref[:sw, :]
        cim = cd_ref[sw:, :]
        for t in range(SSM_L):
            fre = fw_ref[:sw, t:t + 1]
            fim = fw_ref[sw:, t:t + 1]
            cl_ref[:sw, t * LANES:(t + 1) * LANES] = (cre * fre - cim * fim).astype(BF16)
            cl_ref[sw:, t * LANES:(t + 1) * LANES] = (-(cre * fim + cim * fre)).astype(BF16)

    uf = jnp.concatenate([u_ref[pl.ds(s, nc, stride=SSM_L), :].astype(BF16) for s in range(SSM_L)], axis=1)
    y = _dot(uf, t_ref[...])
    v = _dot(uf, bl_ref[...])
    xre = v[:, :sw]
    xim = v[:, sw:]

    row = lax.broadcasted_iota(jnp.int32, (nc, sw), 0)
    cidx = row & (cb - 1)
    bidx = row // cb
    h0 = h0_ref[...]
    h0re = jnp.zeros((nc, sw), F32)
    h0im = jnp.zeros((nc, sw), F32)
    for b in range(nb):
        h0re = jnp.where(bidx == b, h0[b:b + 1, :sw], h0re)
        h0im = jnp.where(bidx == b, h0[b:b + 1, sw:], h0im)
    first = cidx == 0
    are = ap_ref[0:1, :sw]
    aim = ap_ref[0:1, sw:]
    xre = xre + jnp.where(first, are * h0re - aim * h0im, 0.0)
    xim = xim + jnp.where(first, are * h0im + aim * h0re, 0.0)

    d = 1
    k = 0
    while d < cb:
        are = ap_ref[k:k + 1, :sw]
        aim = ap_ref[k:k + 1, sw:]
        keep = cidx >= d
        sre = jnp.where(keep, pltpu.roll(xre, d, axis=0), 0.0)
        sim = jnp.where(keep, pltpu.roll(xim, d, axis=0), 0.0)
        xre, xim = xre + (are * sre - aim * sim), xim + (are * sim + aim * sre)
        d *= 2
        k += 1

    for b in range(nb):
        r = b * cb + cb - 1
        s_ref[b:b + 1, :sw] = xre[r:r + 1, :]
        s_ref[b:b + 1, sw:] = xim[r:r + 1, :]

    pre = jnp.where(first, h0re, pltpu.roll(xre, 1, axis=0))
    pim = jnp.where(first, h0im, pltpu.roll(xim, 1, axis=0))
    y = y + _dot(pre.astype(BF16), cl_ref[:sw, :]) + _dot(pim.astype(BF16), cl_ref[sw:, :])
    for t in range(SSM_L):
        y_ref[pl.ds(t, nc, stride=SSM_L), :] = y[:, t * LANES:(t + 1) * LANES]


def _ssm_call(u, mats, h0o, nb, cb):
    kt, bd, rv, cd, fw, ap = mats
    bx, tx, _ = u.shape

    def const_spec(a):
        return pl.BlockSpec((None,) + a.shape[1:], lambda o, b: (o,) + (0,) * (a.ndim - 1))

    tok_spec = pl.BlockSpec((None, tx, LANES), lambda o, b: (b, 0, o))
    st_spec = pl.BlockSpec((None, None, nb, 2 * SSM_SW), lambda o, b: (o, b, 0, 0))
    return pl.pallas_call(
        functools.partial(_ssm_kernel, nb=nb, cb=cb),
        grid=(SSM_OCTS, bx),
        in_specs=[tok_spec] + [const_spec(a) for a in (kt, bd, rv, cd, fw, ap)] + [st_spec],
        out_specs=[tok_spec, st_spec],
        out_shape=[jax.ShapeDtypeStruct((bx, tx, D_SSM), F32),
                   jax.ShapeDtypeStruct((SSM_OCTS, bx, nb, 2 * SSM_SW), F32)],
        scratch_shapes=[pltpu.VMEM((SSM_FLAT, SSM_FLAT), BF16), pltpu.VMEM((SSM_FLAT, 2 * SSM_SW), BF16),
                        pltpu.VMEM((2 * SSM_SW, SSM_FLAT), BF16)],
        compiler_params=_cparams("arbitrary", "arbitrary"),
        name="ssm",
    )(u, kt, bd, rv, cd, fw, ap, h0o)


def _oct_lanes(a):
    lead = a.shape[:-2]
    a = a.reshape(*lead, SSM_OCTS, SSM_SW)
    return jnp.moveaxis(a, -2, 0)


def _ssm_mats(a_re, a_im, log_dt, b_re, b_im, c_re, c_im):
    dt = jnp.exp(log_dt)[:, None]
    lre = a_re * dt
    lim = a_im * dt
    mag = jnp.exp(lre)
    ab_re = mag * jnp.cos(lim)
    ab_im = mag * jnp.sin(lim)
    den = a_re * a_re + a_im * a_im
    n_re = ab_re - 1.0
    f_re = (n_re * a_re + ab_im * a_im) / den
    f_im = (ab_im * a_re - n_re * a_im) / den
    bb_re = f_re[..., None] * b_re - f_im[..., None] * b_im
    bb_im = f_re[..., None] * b_im + f_im[..., None] * b_re

    def lam_pow(tau):
        tau = tau[:, None, None]
        m = jnp.exp(tau * lre)
        return m * jnp.cos(tau * lim), m * jnp.sin(tau * lim)

    pw_re, pw_im = lam_pow(jnp.arange(SSM_L + 1, dtype=F32))
    cp_re = c_re[None] * pw_re[:, :, None, :] - c_im[None] * pw_im[:, :, None, :]
    cp_im = c_re[None] * pw_im[:, :, None, :] + c_im[None] * pw_re[:, :, None, :]
    kern = jnp.sum(cp_re[..., None] * bb_re[None, :, None] - cp_im[..., None] * bb_im[None, :, None],
                   axis=3)
    eye = jnp.eye(SSM_OCT_G, dtype=F32)
    og = (SSM_OCTS, SSM_OCT_G)

    def slab_diag(m):
        lead = m.shape[:-3]
        a, b = m.shape[-2:]
        m = m.reshape(*lead, *og, a, b)
        m = m[..., :, None, :] * eye[:, None, :, None]
        return m.reshape(*lead, SSM_OCTS, SSM_OCT_G * a, SSM_OCT_G * b)

    kt = slab_diag(kern.transpose(0, 1, 3, 2)).astype(BF16).transpose(1, 0, 2, 3)
    bd = jnp.concatenate([slab_diag(bb_re.transpose(0, 2, 1)), slab_diag(bb_im.transpose(0, 2, 1))], axis=-1)
    rev = SSM_L - 1 - jnp.arange(SSM_L)
    rv = jnp.concatenate([_oct_lanes(pw_re[rev]), _oct_lanes(pw_im[rev])], axis=-1)
    cd = jnp.concatenate([slab_diag(c_re.transpose(0, 2, 1)), slab_diag(c_im.transpose(0, 2, 1))], axis=1)
    fw = jnp.concatenate([_oct_lanes(pw_re[1:]), _oct_lanes(pw_im[1:])], axis=-1).transpose(0, 2, 1)
    fw = jnp.pad(fw, ((0, 0), (0, 0), (0, LANES - SSM_L)))

    steps = float(SSM_L) * (2.0 ** jnp.arange(SSM_POW_ROWS, dtype=F32))
    ap_re, ap_im = lam_pow(steps)
    ap = jnp.concatenate([_oct_lanes(ap_re), _oct_lanes(ap_im)], axis=-1)
    return kt, bd, rv, cd, fw, ap


def _ssm_branch(u, mats, h0, nb, cb):
    bx = u.shape[0]
    h0o = jnp.concatenate([_oct_lanes(h0[..., 0]), _oct_lanes(h0[..., 1])], axis=-1)
    y, s = _ssm_call(u, mats, h0o.reshape(SSM_OCTS, bx, nb, 2 * SSM_SW), nb, cb)
    s = s.reshape(SSM_OCTS, bx * nb, 2, SSM_OCT_G, SSM_STATE).transpose(1, 0, 3, 4, 2)
    return y, s.reshape(bx * nb, SSM_GROUPS, SSM_STATE, 2)


def _attn_kernel(sink_ref, q_ref, ka_ref, kb_ref, va_ref, vb_ref, o_ref, *, banded):
    nk = 2 * LANES
    if banded:
        sub = 2 * CHUNK
        kfull = jnp.concatenate([ka_ref[...], kb_ref[...]], axis=0)
        vfull = jnp.concatenate([va_ref[...], vb_ref[...]], axis=0)
    else:
        sub = q_ref.shape[0]
        pad = jnp.zeros((nk - ka_ref.shape[0] - kb_ref.shape[0], KV_W), F32)
        kfull = jnp.concatenate([ka_ref[...], kb_ref[...], pad], axis=0)
        vfull = jnp.concatenate([va_ref[...], vb_ref[...], pad], axis=0)
    nsub = q_ref.shape[0] // sub

    r_i = lax.broadcasted_iota(jnp.int32, (sub, nk), 0)
    c_i = lax.broadcasted_iota(jnp.int32, (sub, nk), 1)
    if banded:
        lo = (r_i // CHUNK) * CHUNK
        valid = (c_i >= lo) & (c_i < lo + WINDOW + CHUNK)
        first_frame = jnp.where(pl.program_id(1) > 0, 0, WINDOW)
        valids = [valid & (c_i >= first_frame)] + [valid] * (nsub - 1)
    else:
        valids = [c_i < (ka_ref.shape[0] + kb_ref.shape[0])]

    lane = lax.broadcasted_iota(jnp.int32, kfull.shape, 1)
    low = lane < HEAD_DIM
    kroll = pltpu.roll(kfull, HEAD_DIM, axis=1)
    vroll = pltpu.roll(vfull, HEAD_DIM, axis=1)

    for g in range(N_KV_HEADS):
        k_lo, k_hi = (kfull, kroll) if g == 0 else (kroll, kfull)
        v_lo, v_hi = (vfull, vroll) if g == 0 else (vroll, vfull)
        kpad = (jnp.where(low, k_lo, 0.0).astype(BF16), jnp.where(low, 0.0, k_hi).astype(BF16))
        vpad = (jnp.where(low, v_lo, 0.0).astype(BF16), jnp.where(low, 0.0, v_hi).astype(BF16))
        for b in range(nsub):
            rs = slice(b * sub, (b + 1) * sub)
            ks = slice(b * sub, b * sub + nk)
            for pp in range(2):
                slab = 2 * g + pp
                qp = q_ref[rs, slab * LANES:(slab + 1) * LANES]
                acc = None
                for hh in range(2):
                    sk = sink_ref[2 * slab + hh]
                    s = jnp.where(valids[b], _dot_nt(qp, kpad[hh][ks]), NEG_INF)
                    m = jnp.maximum(jnp.max(s, axis=-1, keepdims=True), sk)
                    p = jnp.exp(s - m)
                    den = jnp.sum(p, axis=-1, keepdims=True) + jnp.exp(sk - m)
                    o = _dot(p.astype(BF16), vpad[hh][ks]) / den
                    acc = o if acc is None else acc + o
                o_ref[rs, slab * LANES:(slab + 1) * LANES] = acc.astype(BF16)


def _attn_call(sink, q, ka, kb, va, vb, banded):
    bx, tx, _ = q.shape
    if banded:
        rows = ATTN_ROWS
        nt = tx // rows
        grid = (bx, nt)
        q_spec = pl.BlockSpec((None, rows, ATTN_W), lambda b, i: (b, i, 0))
        prev = pl.BlockSpec((None, WINDOW, KV_W), lambda b, i: (b, jnp.maximum(i * (rows // WINDOW) - 1, 0), 0))
        cur = pl.BlockSpec((None, rows, KV_W), lambda b, i: (b, i, 0))
    else:
        rows = tx
        grid = (bx, 1)
        q_spec = pl.BlockSpec((None, rows, ATTN_W), lambda b, i: (b, 0, 0))
        prev = pl.BlockSpec((None, ka.shape[1], KV_W), lambda b, i: (b, 0, 0))
        cur = pl.BlockSpec((None, rows, KV_W), lambda b, i: (b, 0, 0))
    return pl.pallas_call(
        functools.partial(_attn_kernel, banded=banded),
        grid=grid,
        in_specs=[pl.BlockSpec(memory_space=pltpu.SMEM), q_spec, prev, cur, prev, cur],
        out_specs=q_spec,
        out_shape=jax.ShapeDtypeStruct((bx, tx, ATTN_W), BF16),
        compiler_params=_cparams("parallel", "parallel"),
        name="attn",
    )(sink, q, ka, kb, va, vb)


def _route(logits_t, bias_col, topi_ref, topw_ref):
    tt = logits_t.shape[1]
    scores = jax.nn.sigmoid(logits_t)
    biased = scores + bias_col
    sub = lax.broadcasted_iota(jnp.int32, (EXPERTS_PER_GROUP, tt), 0).astype(F32)
    ninf = float('-inf')
    blocks = [biased[EXPERTS_PER_GROUP * g:EXPERTS_PER_GROUP * (g + 1), :] for g in range(N_EXPERT_GROUPS)]

    gscore = jnp.zeros((N_EXPERT_GROUPS, tt), F32)
    for g in range(N_EXPERT_GROUPS):
        blk = blocks[g]
        m1 = jnp.max(blk, axis=0, keepdims=True)
        i1 = jnp.min(jnp.where(blk == m1, sub, float(EXPERTS_PER_GROUP)), axis=0, keepdims=True)
        m2 = jnp.max(jnp.where(sub == i1, ninf, blk), axis=0, keepdims=True)
        gscore = jnp.where(sub == float(g), jnp.broadcast_to(m1 + m2, gscore.shape), gscore)

    grank = jnp.zeros((N_EXPERT_GROUPS, tt), F32)
    for j in range(N_EXPERT_GROUPS):
        rj = jnp.broadcast_to(gscore[j:j + 1, :], gscore.shape)
        beats = (rj > gscore) | ((rj == gscore) & (sub > float(j)))
        grank = grank + jnp.where(beats, 1.0, 0.0)
    gsel = jnp.where(grank < float(TOPK_GROUPS), 1.0, 0.0)

    masked = []
    for g in range(N_EXPERT_GROUPS):
        on = jnp.broadcast_to(gsel[g:g + 1, :], blocks[g].shape) > 0.5
        masked.append(jnp.where(on, blocks[g], NEG_INF))

    ranks = [jnp.zeros((EXPERTS_PER_GROUP, tt), F32) for _ in range(N_EXPERT_GROUPS)]
    for jb in range(N_EXPERT_GROUPS):
        for jj in range(EXPERTS_PER_GROUP):
            rj = jnp.broadcast_to(masked[jb][jj:jj + 1, :], (EXPERTS_PER_GROUP, tt))
            for ib in range(N_EXPERT_GROUPS):
                mi = masked[ib]
                if ib < jb:
                    beats = rj > mi
                elif ib > jb:
                    beats = rj >= mi
                else:
                    beats = (rj > mi) | ((rj == mi) & (sub > float(jj)))
                ranks[ib] = ranks[ib] + jnp.where(beats, 1.0, 0.0)

    topi = jnp.zeros((TOP_K, tt), F32)
    topw = jnp.zeros((TOP_K, tt), F32)
    for r in range(TOP_K):
        ai = jnp.zeros((EXPERTS_PER_GROUP, tt), F32)
        aw = jnp.zeros((EXPERTS_PER_GROUP, tt), F32)
        for g in range(N_EXPERT_GROUPS):
            hit = ranks[g] == float(r)
            ai = ai + jnp.where(hit, sub + float(EXPERTS_PER_GROUP * g), 0.0)
            aw = aw + jnp.where(hit, scores[EXPERTS_PER_GROUP * g:EXPERTS_PER_GROUP * (g + 1), :], 0.0)
        ir = jnp.sum(ai, axis=0, keepdims=True)
        wr = jnp.sum(aw, axis=0, keepdims=True)
        topi = jnp.where(sub == float(r), jnp.broadcast_to(ir, topi.shape), topi)
        topw = jnp.where(sub == float(r), jnp.broadcast_to(wr, topw.shape), topw)
    den = jnp.sum(topw, axis=0, keepdims=True)
    topi_ref[...] = topi.astype(jnp.int32)
    topw_ref[...] = topw / den * ROUTED_SCALE


def _post_kernel(y_ref, u_ref, a_ref, gs_ref, ga_ref, x_ref, g1_ref, sh2_ref, sc2_ref, g2_ref,
                 d_ref, wglu_ref, bglu_ref, wbs_ref, wba_ref, wout_ref, nf_ref, rwt_ref, rb_ref,
                 swg_ref, swu_ref, swd_ref, xp_ref, h2s_ref, topi_ref, topw_ref):
    ys = y_ref[...] + d_ref[...] * u_ref[...]
    z = jax.nn.gelu(ys)
    y2 = z * jax.nn.sigmoid(_dot(z.astype(BF16), wglu_ref[...]) + bglu_ref[...])
    merged = (jax.nn.sigmoid(gs_ref[...]) * _dot(y2.astype(BF16), wbs_ref[...])
              + jax.nn.sigmoid(ga_ref[...]) * _dot(a_ref[...], wba_ref[...]))
    x1 = x_ref[...] + g1_ref[...] * _dot(merged.astype(BF16), wout_ref[...])

    h2 = _rmsnorm(x1, nf_ref[...]) * (1.0 + sc2_ref[...]) + sh2_ref[...]
    tt = h2.shape[0]
    for j in range(SLAB_ROWS):
        h2s_ref[pl.ds(j, tt, stride=SLAB_ROWS), :] = h2[:, j * LANES:(j + 1) * LANES]
    hh, hl = _split(h2)
    rh, rl = _split(rwt_ref[...])
    logits_t = _dot_nt(rh, hh) + (_dot_nt(rh, hl) + _dot_nt(rl, hh))
    _route(logits_t, rb_ref[...], topi_ref, topw_ref)

    shared = _dot((jax.nn.silu(_dot(hh, swg_ref[...])) * _dot(hh, swu_ref[...])).astype(BF16), swd_ref[...])
    xp_ref[...] = x1 + g2_ref[...] * shared


def _post_call(y, u, attn, gs, ga, x, mods, lw, tt):
    bx, tx, _ = x.shape
    nt = tx // tt
    per_tok = mods[0].shape[1] != 1
    mod_spec = (pl.BlockSpec((None, tt, D_MODEL), lambda b, i: (b, i, 0)) if per_tok
                else pl.BlockSpec((None, 1, D_MODEL), lambda b, i: (b, 0, 0)))

    def tok_spec(w):
        return pl.BlockSpec((None, tt, w), lambda b, i: (b, i, 0))

    def full(a):
        return pl.BlockSpec(a.shape, lambda b, i: (0,) * a.ndim)

    weights = [lw['ssm_d'], lw['w_glu'], lw['b_glu'], lw['w_bs'], lw['w_ba'], lw['w_out'], lw['norm_ffn_g'],
               lw['router_wt'], lw['router_bias'], lw['sh_wg'], lw['sh_wu'], lw['sh_wd']]
    return pl.pallas_call(
        _post_kernel,
        grid=(bx, nt),
        in_specs=[tok_spec(D_SSM), tok_spec(D_SSM), tok_spec(ATTN_W), tok_spec(D_MODEL), tok_spec(D_MODEL),
                  tok_spec(D_MODEL), mod_spec, mod_spec, mod_spec, mod_spec] + [full(w) for w in weights],
        out_specs=[tok_spec(D_MODEL),
                   pl.BlockSpec((tt * SLAB_ROWS, LANES), lambda b, i: (b * nt + i, 0)),
                   pl.BlockSpec((TOP_K, tt), lambda b, i: (0, b * nt + i)),
                   pl.BlockSpec((TOP_K, tt), lambda b, i: (0, b * nt + i))],
        out_shape=[jax.ShapeDtypeStruct((bx, tx, D_MODEL), F32),
                   jax.ShapeDtypeStruct((bx * tx * SLAB_ROWS, LANES), F32),
                   jax.ShapeDtypeStruct((TOP_K, bx * tx), jnp.int32),
                   jax.ShapeDtypeStruct((TOP_K, bx * tx), F32)],
        compiler_params=_cparams("parallel", "parallel"),
        name="post",
    )(y, u, attn, gs, ga, x, *mods, *weights)


def _moe_kernel(ptab_ref, ktab_ref, elist_ref, meta_ref, hs_ref, word_hbm, gate_hbm, wg_hbm, wu_hbm, wd_hbm, o_ref,
                word_s, gate_s, xt0, xt1, yt0, yt1, wg_buf, wu_buf, wd_buf, lsem, wsem, *, tm, bs, ptab_w, layer):
    i = pl.program_id(0)
    stride = bs + SUBLANES
    nslot = word_s.shape[0]
    ngate = gate_s.shape[0]
    acc_ref = o_ref
    slab_mask = -SLAB_ROWS
    gather_mask = (tm * SLAB_ROWS - 1) & slab_mask
    tab0 = i * ptab_w + 1
    npairs = meta_ref[2 * i]
    nrank = meta_ref[2 * i + 1]

    def list_copies():
        return (pltpu.make_async_copy(word_hbm.at[pl.ds(pl.multiple_of(i * nslot, LANES), nslot)],
                                      word_s, lsem.at[0]),
                pltpu.make_async_copy(gate_hbm.at[pl.ds(pl.multiple_of(i * ngate, LANES), ngate)],
                                      gate_s, lsem.at[1]))

    def weight_copies(k):
        ex = elist_ref[i * N_EXPERTS + k]
        slot = k % MOE_WSLOTS
        return (pltpu.make_async_copy(wg_hbm.at[layer, ex], wg_buf.at[slot], wsem.at[slot, 0]),
                pltpu.make_async_copy(wu_hbm.at[layer, ex], wu_buf.at[slot], wsem.at[slot, 1]),
                pltpu.make_async_copy(wd_hbm.at[layer, ex], wd_buf.at[slot], wsem.at[slot, 2]))

    def gather(base, xt_ref):
        for r in range(bs):
            row = pl.multiple_of(word_s[base + r] & gather_mask, SLAB_ROWS)
            xt_ref[pl.ds(r, SLAB_ROWS, stride=stride), :] = hs_ref[pl.ds(row, SLAB_ROWS), :]

    def expert(xt_ref, yt_ref, k):
        slot = k % MOE_WSLOTS
        x = jnp.concatenate([xt_ref[j * stride:j * stride + bs, :] for j in range(SLAB_ROWS)],
                            axis=1).astype(BF16)
        act = jax.nn.silu(_dot(x, wg_buf[slot])) * _dot(x, wu_buf[slot])
        y = _dot(act.astype(BF16), wd_buf[slot])
        for j in range(SLAB_ROWS):
            yt_ref[j * stride:j * stride + bs, :] = y[:, j * LANES:(j + 1) * LANES]

    def scatter(base, yt_ref):
        for r0 in range(0, bs, MOE_RMW_UNROLL):
            upd = []
            for r in range(r0, r0 + MOE_RMW_UNROLL):
                word = word_s[base + r]
                row = pl.multiple_of(word & slab_mask, SLAB_ROWS)
                upd.append((row, acc_ref[pl.ds(row, SLAB_ROWS), :]
                            + gate_s[word] * yt_ref[pl.ds(r, SLAB_ROWS, stride=stride), :]))
            for row, val in upd:
                acc_ref[pl.ds(row, SLAB_ROWS), :] = val

    for c in list_copies():
        c.start()
    ahead = MOE_WSLOTS - 2
    for r in range(ahead + 1):
        @pl.when(r < nrank)
        def _():
            for c in weight_copies(r):
                c.start()

    acc_ref[...] = jnp.zeros_like(acc_ref)
    yt1[...] = jnp.zeros_like(yt1)
    for c in list_copies():
        c.wait()
    gather(ptab_ref[tab0], xt0)

    def pair(j, carry):
        waited, started = carry
        t = tab0 + 2 * j
        k0 = ktab_ref[t]
        k1 = ktab_ref[t + 1]

        target = jnp.minimum(k1 + ahead, nrank - 1)
        for d in (1, 2):
            @pl.when(started + d <= target)
            def _():
                for c in weight_copies(started + d):
                    c.start()

        for d in (1, 2):
            @pl.when(waited + d <= k1)
            def _():
                for c in weight_copies(waited + d):
                    c.wait()

        gather(ptab_ref[t + 1], xt1)
        expert(xt0, yt0, k0)
        scatter(ptab_ref[t - 1], yt1)
        gather(ptab_ref[t + 2], xt0)
        expert(xt1, yt1, k1)
        scatter(ptab_ref[t], yt0)
        return jnp.maximum(waited, k1), jnp.maximum(started, target)

    lax.fori_loop(0, npairs, pair, (jnp.int32(-1), jnp.minimum(jnp.int32(ahead), nrank - 1)))
    scatter(ptab_ref[tab0 + 2 * npairs - 1], yt1)


def _moe_call(h2s, topi, topw, wg, wu, wd, layer, tm):
    n = h2s.shape[0] // SLAB_ROWS
    nt = n // tm
    nseg = nt * N_EXPERTS
    nreal = tm * TOP_K
    bs = MOE_BLOCK_LARGE if nreal // N_EXPERTS >= MOE_BLOCK_LARGE else MOE_BLOCK_SMALL
    npad = MOE_PAD
    assert npad >= bs - 1 and npad % LANES == 0
    nslot = nreal + (N_EXPERTS + 1) * npad
    null_base = nslot - npad
    ptab_w = nreal // bs + N_EXPERTS + 4

    i32 = jnp.int32
    experts = jnp.arange(N_EXPERTS, dtype=i32)
    counts = jnp.sum((topi.reshape(TOP_K, nt, tm)[..., None] == experts).astype(i32), axis=(0, 2))
    nblk = (counts + bs - 1) // bs
    first = jnp.cumsum(counts + npad, axis=1) - (counts + npad)
    pend = jnp.cumsum(nblk, axis=1)
    pstart = pend - nblk
    nonempty = (nblk > 0).astype(i32)
    krank = jnp.cumsum(nonempty, axis=1) - 1
    nrank = jnp.sum(nonempty, axis=1)
    elist = jnp.sum(jnp.where((nonempty[:, None, :] == 1) & (krank[:, None, :] == experts[None, :, None]),
                              experts[None, None, :], 0), axis=-1).astype(i32)
    pos = jnp.arange(ptab_w, dtype=i32) - 1
    e_of = jnp.sum((pend[:, None, :] <= pos[None, :, None]).astype(i32), axis=-1)
    e_cl = jnp.minimum(e_of, N_EXPERTS - 1)
    onehot = (e_cl[..., None] == experts).astype(i32)

    def at_expert(a):
        return jnp.sum(onehot * a[:, None, :], axis=-1)

    q = pos[None, :] - at_expert(pstart)
    real_blk = (pos[None, :] >= 0) & (e_of < N_EXPERTS)
    ptab = jnp.where(real_blk, at_expert(first) + q * bs, null_base).astype(i32)
    ktab = jnp.where(real_blk, at_expert(krank), nrank[:, None] - 1).astype(i32)
    meta = jnp.stack([(pend[:, -1] + 1) // 2, nrank], axis=1).astype(i32)

    assert TOP_K == SLAB_ROWS and nreal < (1 << MOE_WORD_BITS) and 2 * nseg + 2 < (1 << (31 - MOE_WORD_BITS))
    flat = jnp.arange(n * TOP_K, dtype=i32)
    slot = (flat % n) * TOP_K + flat // n
    key = 2 * ((slot // nreal) * N_EXPERTS + topi.reshape(n * TOP_K))
    pad_key = jnp.concatenate([jnp.repeat(2 * jnp.arange(nseg, dtype=i32) + 1, npad),
                               jnp.repeat(2 * (jnp.arange(nt, dtype=i32) * N_EXPERTS + N_EXPERTS - 1) + 1, npad)])
    packed = jnp.concatenate([(key << MOE_WORD_BITS) | (slot % nreal), (pad_key << MOE_WORD_BITS) | nreal])
    word_s = lax.sort(packed, is_stable=False) & ((1 << MOE_WORD_BITS) - 1)
    ngate = nreal + LANES
    gate_s = jnp.concatenate([topw.T.reshape(nt, nreal), jnp.zeros((nt, LANES), F32)], axis=1).reshape(nt * ngate)

    stage = pltpu.VMEM((SLAB_ROWS * (bs + SUBLANES), LANES), F32)
    hbm = pl.BlockSpec(memory_space=pl.ANY)
    grid_spec = pltpu.PrefetchScalarGridSpec(
        num_scalar_prefetch=4,
        grid=(nt,),
        in_specs=[
            pl.BlockSpec((tm * SLAB_ROWS, LANES), lambda i, *_: (i, 0), pipeline_mode=pl.Buffered(1)),
            hbm, hbm, hbm, hbm, hbm,
        ],
        out_specs=pl.BlockSpec((None, (tm + 1) * SLAB_ROWS, LANES), lambda i, *_: (i, 0, 0),
                               pipeline_mode=pl.Buffered(1)),
        scratch_shapes=[
            pltpu.SMEM((nslot,), i32),
            pltpu.SMEM((ngate,), F32),
            stage, stage, stage, stage,
            pltpu.VMEM((MOE_WSLOTS, D_MODEL, D_EXPERT), BF16),
            pltpu.VMEM((MOE_WSLOTS, D_MODEL, D_EXPERT), BF16),
            pltpu.VMEM((MOE_WSLOTS, D_EXPERT, D_MODEL), BF16),
            pltpu.SemaphoreType.DMA((2,)),
            pltpu.SemaphoreType.DMA((MOE_WSLOTS, 3)),
        ],
    )
    return pl.pallas_call(
        functools.partial(_moe_kernel, tm=tm, bs=bs, ptab_w=ptab_w, layer=layer),
        grid_spec=grid_spec,
        out_shape=jax.ShapeDtypeStruct((nt, (tm + 1) * SLAB_ROWS, LANES), F32),
        compiler_params=_cparams("arbitrary"),
        name="moe",
    )(ptab.reshape(nt * ptab_w), ktab.reshape(nt * ptab_w), elist.reshape(nseg), meta.reshape(nt * 2),
      h2s, word_s, gate_s, wg, wu, wd)


def _combine_kernel(r_ref, xp_ref, g2_ref, fg_ref, o_ref, *, final):
    tt = xp_ref.shape[0]
    routed = jnp.concatenate([r_ref[pl.ds(j, tt, stride=SLAB_ROWS), :] for j in range(SLAB_ROWS)], axis=1)
    x2 = xp_ref[...] + g2_ref[...] * routed
    if final:
        x2 = _rmsnorm(x2, fg_ref[...])
    o_ref[...] = x2


def _combine_call(routed_s, xp, g2, final_g, tt, final):
    bx, tx, _ = xp.shape
    nt = tx // tt
    per_tok = g2.shape[1] != 1
    g2_spec = (pl.BlockSpec((None, tt, D_MODEL), lambda b, i: (b, i, 0)) if per_tok
               else pl.BlockSpec((None, 1, D_MODEL), lambda b, i: (b, 0, 0)))
    tok_spec = pl.BlockSpec((None, tt, D_MODEL), lambda b, i: (b, i, 0))
    return pl.pallas_call(
        functools.partial(_combine_kernel, final=final),
        grid=(bx, nt),
        in_specs=[_routed_spec(routed_s, tt, nt), tok_spec, g2_spec,
                  pl.BlockSpec((1, D_MODEL), lambda b, i: (0, 0))],
        out_specs=tok_spec,
        out_shape=jax.ShapeDtypeStruct((bx, tx, D_MODEL), F32),
        compiler_params=_cparams("parallel", "parallel"),
        name="combine",
    )(routed_s, xp, g2, final_g)


def _rope_tables(pos):
    half = HEAD_DIM // 2
    inv_freq = ROPE_THETA ** (-jnp.arange(half, dtype=F32) / half)
    ang = pos.astype(F32)[:, None] * inv_freq[None, :]
    cos = jnp.cos(ang)
    sin = jnp.sin(ang)
    return jnp.tile(cos, (1, 4)), jnp.tile(jnp.concatenate([-sin, sin], axis=1), (1, 2))


def _trunk(x, mods, pos, past_k, past_v, ssm_h0, layers, experts, final_g, tt, tm, tok_batches):
    bx, tx, _ = x.shape
    b, t = tok_batches
    cos_t, sin_t = _rope_tables(pos)
    new_k, new_v, new_h = [], [], []
    for l, lw in enumerate(layers):
        sh1, sc1, g1, sh2, sc2, g2 = mods[l]
        proj = _inproj_call(x, sh1, sc1, lw['norm_attn_g'], lw['w_in'], cos_t, sin_t, tt)
        u, q, k, v, gs, ga = proj[:6]
        if isinstance(x, tuple):
            x = proj[6]
        ks = k.reshape(b, t, KV_W)
        vs = v.reshape(b, t, KV_W)
        qs = q.reshape(b, t, ATTN_W)
        if past_k is None:
            attn = _attn_call(lw['sink'], qs, ks, ks, vs, vs, True)
            new_k.append(ks[:, -WINDOW:].reshape(b, WINDOW, N_KV_HEADS, HEAD_DIM))
            new_v.append(vs[:, -WINDOW:].reshape(b, WINDOW, N_KV_HEADS, HEAD_DIM))
            h0 = jnp.zeros((b, SSM_GROUPS, SSM_STATE, 2), F32)
        else:
            pk = past_k[l].reshape(b, -1, KV_W)
            pv = past_v[l].reshape(b, -1, KV_W)
            attn = _attn_call(lw['sink'], qs, pk, ks, pv, vs, False)
            new_k.append(ks.reshape(b, t, N_KV_HEADS, HEAD_DIM))
            new_v.append(vs.reshape(b, t, N_KV_HEADS, HEAD_DIM))
            h0 = ssm_h0[l]
        y, h_last = _ssm_branch(u, lw['ssm_mats'], h0, b // bx, t // SSM_L)
        new_h.append(h_last)
        xp, h2s, topi, topw = _post_call(y, u, attn.reshape(bx, tx, ATTN_W), gs, ga, x,
                                         (g1, sh2, sc2, g2), lw, tt)
        routed_s = _moe_call(h2s, topi, topw, *experts, l, tm)
        x = (routed_s, xp, g2)
    x = _combine_call(routed_s, xp, g2, final_g, tt, True)
    return x, jnp.stack(new_k), jnp.stack(new_v), jnp.stack(new_h)


def kernel(x_prompt, x_sample, cache_k, cache_v, state_ssm, c_prompt, c_sample, ada_w, ada_b, norm_attn_g,
           norm_ffn_g, w_in, ssm_a_re, ssm_a_im, ssm_log_dt, ssm_b_re, ssm_b_im, ssm_c_re, ssm_c_im, ssm_d,
           ssm_w_glu, ssm_b_glu, attn_sink, w_branch_ssm, w_branch_attn, w_out, router_w, router_bias,
           expert_w_gate, expert_w_up, expert_w_down, shared_w_gate, shared_w_up, shared_w_down, final_g):
    depth = ada_w.shape[0]
    bp, tp, _ = x_prompt.shape
    bs, ts, _ = x_sample.shape

    layers = []
    for l in range(depth):
        layers.append({
            'norm_attn_g': norm_attn_g[l][None], 'norm_ffn_g': norm_ffn_g[l][None],
            'w_in': w_in[l].astype(BF16), 'sink': attn_sink[l],
            'ssm_mats': _ssm_mats(ssm_a_re[l], ssm_a_im[l], ssm_log_dt[l], ssm_b_re[l], ssm_b_im[l],
                                  ssm_c_re[l], ssm_c_im[l]),
            'ssm_d': ssm_d[l][None], 'w_glu': ssm_w_glu[l].astype(BF16), 'b_glu': ssm_b_glu[l][None],
            'w_bs': w_branch_ssm[l].astype(BF16), 'w_ba': w_branch_attn[l].astype(BF16),
            'w_out': w_out[l].astype(BF16), 'router_wt': router_w[l].T, 'router_bias': router_bias[l][:, None],
            'sh_wg': shared_w_gate[l].astype(BF16), 'sh_wu': shared_w_up[l].astype(BF16),
            'sh_wd': shared_w_down[l].astype(BF16),
        })
    experts = (expert_w_gate.astype(BF16), expert_w_up.astype(BF16), expert_w_down.astype(BF16))
    fg = final_g[None]

    rows = 16
    c_all = jnp.concatenate([c_prompt, c_sample, jnp.zeros((rows - bp - bs, D_MODEL), F32)], axis=0)
    mod = _mod_call(c_all, ada_w, ada_b).reshape(depth, rows, 6, D_MODEL)
    mods_p = [[mod[l, :bp, j][:, None, :] for j in range(6)] for l in range(depth)]
    mods_s = [[jnp.repeat(mod[l, bp:bp + bs, j], ts, axis=0)[None] for j in range(6)] for l in range(depth)]

    pos_p = jnp.arange(tp, dtype=jnp.int32)
    pos_s = jnp.tile(PAST_LEN + jnp.arange(ts, dtype=jnp.int32), bs)

    y_p, k_p, v_p, h_p = _trunk(x_prompt, mods_p, pos_p, None, None, None, layers, experts, fg,
                                tt=512, tm=min(4096, bp * tp), tok_batches=(bp, tp))
    y_s, k_s, v_s, h_s = _trunk(x_sample.reshape(1, bs * ts, D_MODEL), mods_s, pos_s, cache_k, cache_v,
                                state_ssm, layers, experts, fg, tt=bs * ts, tm=bs * ts, tok_batches=(bs, ts))
    return (y_p, y_s.reshape(bs, ts, D_MODEL), k_p, v_p, h_p, k_s, v_s, h_s)
```
